```python
import math
import jax, jax.numpy as jnp
from jax import lax
import numpy as np

D_MODEL = 1024
BATCH = 16
SEQ = 2048
DEPTH = 2

CTX_LEN = 256
GRID_W = 64
EPS = 1e-6

ATT_HEADS = 8
ATT_HEAD_DIM = 64
ATT_V_DIM = 2 * ATT_HEAD_DIM
ATT_QK_WIDTH = ATT_HEADS * 2 * ATT_HEAD_DIM
ATT_WIDTH = ATT_HEADS * ATT_V_DIM
Q_BLOCK = 128
ROPE_BASE = 10000.0

REC_HEADS = 8
REC_KEY_DIM = 128
REC_VAL_DIM = D_MODEL // REC_HEADS
REC_KEY_WIDTH = REC_HEADS * REC_KEY_DIM
REC_WIDTH = REC_HEADS * REC_VAL_DIM
CHUNK = 64

SPLITS = (ATT_QK_WIDTH, ATT_QK_WIDTH, ATT_WIDTH, REC_KEY_WIDTH, REC_KEY_WIDTH, REC_KEY_WIDTH, REC_WIDTH, REC_WIDTH, D_MODEL, D_MODEL)
N_IN = sum(SPLITS)

N_EXPERTS = 16
N_GROUPS = 4
EXPERTS_PER_GROUP = N_EXPERTS // N_GROUPS
TOP_K = 2
D_EXPERT = 1024

kernel_name = 'hybrid_diffattn_hgrn2_groupmoe_dit'


def rmsnorm(x, g):
    xf = x.astype(jnp.float32)
    y = xf * lax.rsqrt(jnp.mean(xf * xf, axis=-1, keepdims=True) + EPS)
    return (y * g.astype(jnp.float32)).astype(x.dtype)


def modulate(x, shift, scale):
    return x * (1 + scale) + shift


def project(h, w):
    outs, start = [], 0
    for width in SPLITS:
        outs.append(h @ w[:, start:start + width])
        start += width
    return outs


def axial_angles(n_tok):
    rows = n_tok // GRID_W
    row = jnp.repeat(jnp.arange(rows, dtype=jnp.float32), GRID_W)
    col = jnp.tile(jnp.arange(GRID_W, dtype=jnp.float32), rows)
    n_freq = ATT_HEAD_DIM // 4
    inv = ROPE_BASE ** (-jnp.arange(n_freq, dtype=jnp.float32) / n_freq)
    return row[:, None] * inv, col[:, None] * inv


def _rotate(x, ang):
    x1, x2 = jnp.split(x, 2, axis=-1)
    cos, sin = jnp.cos(ang), jnp.sin(ang)
    return jnp.concatenate([x1 * cos - x2 * sin, x1 * sin + x2 * cos], axis=-1)


def axial_rope(x, ang_r, ang_c):
    xf = x.astype(jnp.float32)
    xr, xc = jnp.split(xf, 2, axis=-1)
    ar, ac = ang_r[:, None, None, :], ang_c[:, None, None, :]
    return jnp.concatenate([_rotate(xr, ar), _rotate(xc, ac)], axis=-1).astype(x.dtype)


def diff_attend(q, k, v, lam):
    s = jnp.einsum('bhqmd,bhkmd->bhmqk', q.astype(jnp.float32), k.astype(jnp.float32)) * (ATT_HEAD_DIM ** -0.5)
    p = jax.nn.softmax(s, axis=-1)
    w = p[:, :, 0] - lam * p[:, :, 1]
    return jnp.einsum('bhqk,bhkv->bhqv', w, v.astype(jnp.float32))


def diff_attention_latent(q, k, v, lam):
    b, h, n_tok = q.shape[:3]
    nb = n_tok // Q_BLOCK
    qb = q.reshape(b, h, nb, Q_BLOCK, 2, ATT_HEAD_DIM).transpose(2, 0, 1, 3, 4, 5)
    ob = lax.map(lambda blk: diff_attend(blk, k, v, lam), qb)
    return ob.transpose(1, 2, 0, 3, 4).reshape(b, h, n_tok, ATT_V_DIM)


def gla_chunk_scan(q, k, v, log_f, s0):
    b, h, n_tok, _ = q.shape
    dv = v.shape[-1]
    n_chunks = n_tok // CHUNK

    def chunks(a):
        return a.reshape(b, h, n_chunks, CHUNK, a.shape[-1]).transpose(2, 0, 1, 3, 4)

    lower = jnp.tril(jnp.ones((CHUNK, CHUNK), dtype=bool))[:, :, None]

    def step(state, inp):
        qc, kc, vc, gc = inp
        g_cum = jnp.cumsum(gc, axis=2)
        inter = jnp.einsum('bhtd,bhde->bhte', qc * jnp.exp(g_cum), state)
        rel = g_cum[:, :, :, None, :] - g_cum[:, :, None, :, :]
        decay = jnp.exp(jnp.where(lower, rel, -jnp.inf))
        scores = jnp.einsum('bhtd,bhsd,bhtsd->bhts', qc, kc, decay)
        intra = jnp.einsum('bhts,bhse->bhte', scores, vc)
        g_last = g_cum[:, :, -1:, :]
        new_state = (jnp.exp(g_last[:, :, 0, :, None]) * state
                     + jnp.einsum('bhsd,bhse->bhde', kc * jnp.exp(g_last - g_cum), vc))
        return new_state, inter + intra

    s_fin, out = lax.scan(step, s0, (chunks(q), chunks(k), chunks(v), chunks(log_f)))
    return out.transpose(1, 2, 0, 3, 4).reshape(b, h, n_tok, dv), s_fin


def hgrn2_forget(z, lb):
    log_f = jnp.logaddexp(jnp.log(lb), jnp.log1p(-lb) + jax.nn.log_sigmoid(z))
    k = (1.0 - lb) * jax.nn.sigmoid(-z)
    return k, log_f


def bidir_hgrn2(q, k_f, k_b, v, log_f_f, log_f_b, s0_f, s0_b):
    o_f, s_f = gla_chunk_scan(q, k_f, v, log_f_f, s0_f)
    o_b, s_b = gla_chunk_scan(jnp.flip(q, 2), jnp.flip(k_b, 2), jnp.flip(v, 2), jnp.flip(log_f_b, 2), s0_b)
    return o_f + jnp.flip(o_b, 2), s_f, s_b


def merge_branches(o_att, o_rec, rec_gate, gate_att, gate_rec, lam_init, g_subln, g_rec, w_br_a, w_br_r, w_out, dt):
    b, _, n_tok, _ = o_att.shape
    a = rmsnorm(o_att.transpose(0, 2, 1, 3), g_subln) * (1.0 - lam_init)
    a = a.reshape(b, n_tok, ATT_WIDTH).astype(dt)
    r = rmsnorm(o_rec.transpose(0, 2, 1, 3), g_rec).reshape(b, n_tok, REC_WIDTH)
    r = (r * jax.nn.silu(rec_gate.astype(jnp.float32))).astype(dt)
    mixed = jax.nn.sigmoid(gate_att) * (a @ w_br_a) + jax.nn.sigmoid(gate_rec) * (r @ w_br_r)
    return mixed @ w_out


def token_mixers(h, hc, w_in, lam, lam_init, g_subln, lb, g_rec, w_br_a, w_br_r, w_out, ang_r, ang_c, need_ctx):
    b, n_lat, _ = h.shape
    n_ctx = hc.shape[1]
    dt = h.dtype
    aq, ak, av, rq, rff, rfb, ri, rg, ga, gr = project(h, w_in)
    aqc, akc, avc, rqc, rffc, rfbc, ric, rgc, gac, grc = project(hc, w_in)

    def att_qk(a, n):
        return a.reshape(b, n, ATT_HEADS, 2, ATT_HEAD_DIM)

    def att_v(a, n):
        return a.reshape(b, n, ATT_HEADS, ATT_V_DIM).transpose(0, 2, 1, 3)

    q_l = axial_rope(att_qk(aq, n_lat), ang_r, ang_c).transpose(0, 2, 1, 3, 4)
    k_l = axial_rope(att_qk(ak, n_lat), ang_r, ang_c).transpose(0, 2, 1, 3, 4)
    q_c = att_qk(aqc, n_ctx).transpose(0, 2, 1, 3, 4)
    k_c = att_qk(akc, n_ctx).transpose(0, 2, 1, 3, 4)
    v_l, v_c = att_v(av, n_lat), att_v(avc, n_ctx)
    k_all = jnp.concatenate([k_c, k_l], axis=2)
    v_all = jnp.concatenate([v_c, v_l], axis=2)
    o_att = diff_attention_latent(q_l, k_all, v_all, lam)

    def rec(a, n, dim):
        return a.reshape(b, n, REC_HEADS, dim).transpose(0, 2, 1, 3).astype(jnp.float32)

    lb = lb.reshape(REC_HEADS, 1, REC_KEY_DIM)
    kcf, gcf = hgrn2_forget(rec(rffc, n_ctx, REC_KEY_DIM), lb)
    kcb, gcb = hgrn2_forget(rec(rfbc, n_ctx, REC_KEY_DIM), lb)
    s0 = jnp.zeros((b, REC_HEADS, REC_KEY_DIM, REC_VAL_DIM), jnp.float32)
    o_rec_c, s_f, s_b = bidir_hgrn2(rec(rqc, n_ctx, REC_KEY_DIM), kcf, kcb, rec(ric, n_ctx, REC_VAL_DIM), gcf, gcb, s0, s0)
    klf, glf = hgrn2_forget(rec(rff, n_lat, REC_KEY_DIM), lb)
    klb, glb = hgrn2_forget(rec(rfb, n_lat, REC_KEY_DIM), lb)
    o_rec, _, _ = bidir_hgrn2(rec(rq, n_lat, REC_KEY_DIM), klf, klb, rec(ri, n_lat, REC_VAL_DIM), glf, glb, s_f, s_b)

    y = merge_branches(o_att, o_rec, rg, ga, gr, lam_init, g_subln, g_rec, w_br_a, w_br_r, w_out, dt)
    yc = None
    if need_ctx:
        o_att_c = diff_attend(q_c, k_c, v_c, lam)
        yc = merge_branches(o_att_c, o_rec_c, rgc, gac, grc, lam_init, g_subln, g_rec, w_br_a, w_br_r, w_out, dt)
    return y, yc


def moe_ffn(h, w_router, b_router, w_gate, w_up, w_down):
    scores = jax.nn.sigmoid(h.astype(jnp.float32) @ w_router.astype(jnp.float32))
    biased = (scores + b_router.astype(jnp.float32)).reshape(-1, N_GROUPS, EXPERTS_PER_GROUP)
    group_score = jnp.sum(lax.top_k(biased, TOP_K)[0], axis=-1)
    group = jnp.argmax(group_score, axis=-1)
    in_group = jnp.einsum('tge,tg->te', biased, jax.nn.one_hot(group, N_GROUPS, dtype=jnp.float32))
    _, local = lax.top_k(in_group, TOP_K)
    expert = group[:, None] * EXPERTS_PER_GROUP + local
    w = jnp.take_along_axis(scores, expert, axis=1)
    w = w / jnp.sum(w, axis=-1, keepdims=True)
    gates = jnp.einsum('tk,tke->te', w, jax.nn.one_hot(expert, N_EXPERTS, dtype=jnp.float32))
    y = jnp.zeros(h.shape, jnp.float32)
    for e in range(N_EXPERTS):
        hid = jax.nn.silu(h @ w_gate[e]) * (h @ w_up[e])
        y = y + gates[:, e:e + 1] * (hid @ w_down[e]).astype(jnp.float32)
    return y.astype(h.dtype)


def setup_inputs(seed: int = 0) -> dict:
    key = jax.random.key(seed)
    ks = jax.random.split(key, 25)
    D = D_MODEL

    def nrm(k, shape, s):
        return jax.random.normal(k, shape, jnp.float32) * s

    return {
        'x': nrm(ks[0], (BATCH, SEQ, D), 1.0),
        'c': nrm(ks[1], (BATCH, D), 1.0),
        'ctx': nrm(ks[2], (BATCH, CTX_LEN, D), 1.0),
        'c_ctx': nrm(ks[3], (D,), 1.0),
        'w_mod': nrm(ks[4], (DEPTH, D, 6 * D), 0.5 * D ** -0.5),
        'b_mod': nrm(ks[5], (DEPTH, 6 * D), 0.01),
        'g_norm1': 1.0 + nrm(ks[6], (DEPTH, D), 0.02),
        'g_norm2': 1.0 + nrm(ks[7], (DEPTH, D), 0.02),
        'w_in': nrm(ks[8], (DEPTH, D, N_IN), D ** -0.5),
        'lambda_q1': nrm(ks[9], (DEPTH, ATT_HEAD_DIM), 0.1),
        'lambda_k1': nrm(ks[10], (DEPTH, ATT_HEAD_DIM), 0.1),
        'lambda_q2': nrm(ks[11], (DEPTH, ATT_HEAD_DIM), 0.1),
        'lambda_k2': nrm(ks[12], (DEPTH, ATT_HEAD_DIM), 0.1),
        'g_subln': 1.0 + nrm(ks[13], (DEPTH, ATT_V_DIM), 0.02),
        'lb_logits': nrm(ks[14], (DEPTH, REC_KEY_WIDTH), 0.5),
        'g_rec_norm': 1.0 + nrm(ks[15], (DEPTH, REC_VAL_DIM), 0.02),
        'w_br_attn': nrm(ks[16], (DEPTH, ATT_WIDTH, D), ATT_WIDTH ** -0.5),
        'w_br_rec': nrm(ks[17], (DEPTH, REC_WIDTH, D), REC_WIDTH ** -0.5),
        'w_out': nrm(ks[18], (DEPTH, D, D), D ** -0.5),
        'w_router': nrm(ks[19], (D, N_EXPERTS), D ** -0.5),
        'b_router': nrm(ks[20], (N_EXPERTS,), 0.01),
        'w_gate': nrm(ks[21], (DEPTH, N_EXPERTS, D, D_EXPERT), D ** -0.5),
        'w_up': nrm(ks[22], (DEPTH, N_EXPERTS, D, D_EXPERT), D ** -0.5),
        'w_down': nrm(ks[23], (DEPTH, N_EXPERTS, D_EXPERT, D), D_EXPERT ** -0.5),
        'g_final': 1.0 + nrm(ks[24], (D,), 0.02),
    }


def reference(x, c, ctx, c_ctx, w_mod, b_mod, g_norm1, g_norm2, w_in, lambda_q1, lambda_k1, lambda_q2, lambda_k2,
              g_subln, lb_logits, g_rec_norm, w_br_attn, w_br_rec, w_out, w_router, b_router, w_gate, w_up, w_down, g_final):
    b, n_lat, d = x.shape
    ang_r, ang_c = axial_angles(n_lat)
    lbs = jnp.cumsum(jax.nn.softmax(lb_logits.astype(jnp.float32), axis=0), axis=0)
    lbs = lbs - lbs[:1]
    xc = ctx
    for l in range(DEPTH):
        last = l == DEPTH - 1
        lam_init = 0.8 - 0.6 * math.exp(-0.3 * l)
        lam = (jnp.exp(jnp.sum(lambda_q1[l].astype(jnp.float32) * lambda_k1[l].astype(jnp.float32)))
               - jnp.exp(jnp.sum(lambda_q2[l].astype(jnp.float32) * lambda_k2[l].astype(jnp.float32))) + lam_init)
        mod = jax.nn.silu(c) @ w_mod[l] + b_mod[l]
        sh1, sc1, gt1, sh2, sc2, gt2 = [m[:, None, :] for m in jnp.split(mod, 6, axis=-1)]
        mod_c = jax.nn.silu(c_ctx) @ w_mod[l] + b_mod[l]
        sh1c, sc1c, gt1c, sh2c, sc2c, gt2c = jnp.split(mod_c, 6, axis=-1)

        h = modulate(rmsnorm(x, g_norm1[l]), sh1, sc1)
        hc = modulate(rmsnorm(xc, g_norm1[l]), sh1c, sc1c)
        y, yc = token_mixers(h, hc, w_in[l], lam, lam_init, g_subln[l], lbs[l], g_rec_norm[l],
                             w_br_attn[l], w_br_rec[l], w_out[l], ang_r, ang_c, not last)
        x = x + gt1 * y
        h2 = modulate(rmsnorm(x, g_norm2[l]), sh2, sc2)
        if not last:
            xc = xc + gt1c * yc
            h2c = modulate(rmsnorm(xc, g_norm2[l]), sh2c, sc2c)
            n_ctx_tok = b * xc.shape[1]
            tokens = jnp.concatenate([h2c.reshape(-1, d), h2.reshape(-1, d)], axis=0)
            out = moe_ffn(tokens, w_router, b_router, w_gate[l], w_up[l], w_down[l])
            xc = xc + gt2c * out[:n_ctx_tok].reshape(xc.shape)
            x = x + gt2 * out[n_ctx_tok:].reshape(x.shape)
        else:
            x = x + gt2 * moe_ffn(h2.reshape(-1, d), w_router, b_router, w_gate[l], w_up[l], w_down[l]).reshape(x.shape)
    return rmsnorm(x, g_final)
```

```python
import functools
import math

import numpy as np
import jax
import jax.numpy as jnp
from jax import lax
from jax.experimental import pallas as pl
from jax.experimental.pallas import tpu as pltpu

EPS = 1e-6
GRID_W = 64
ROPE_BASE = 10000.0
ATT_HEADS = 8
ATT_HEAD_DIM = 64
HEAD_W = 2 * ATT_HEAD_DIM
REC_HEADS = 8
N_EXPERTS = 16
N_GROUPS = 4
EXPERTS_PER_GROUP = N_EXPERTS // N_GROUPS
N_PAIR_CLASSES = N_GROUPS * 6
N_SPLITS = 10

LANES = 128
TOKEN_TILE = 256
PROJ_ROWS = 256
GLA_CHUNK = 64
MOE_TILE = 256
VMEM_LIMIT = 56 * 1024 * 1024

F32 = jnp.float32
BF16 = jnp.bfloat16


def _sigmoid(x):
    return 1.0 / (1.0 + jnp.exp(-x))


def _cparams(sem):
    return pltpu.CompilerParams(dimension_semantics=sem, vmem_limit_bytes=VMEM_LIMIT)


def _mod_kernel(c_ref, w_ref, b_ref, o_ref):
    c = c_ref[...]
    s = c * _sigmoid(c)
    o_ref[...] = jnp.dot(s, w_ref[...], precision=lax.Precision.HIGHEST,
                         preferred_element_type=F32) + b_ref[...]


def _modulation(c_all, w_mod, b_mod):
    depth, d, n = w_mod.shape
    bp = c_all.shape[0]
    tn = 1024
    return pl.pallas_call(
        _mod_kernel,
        out_shape=jax.ShapeDtypeStruct((depth, bp, n), F32),
        grid=(depth, n // tn),
        in_specs=[pl.BlockSpec((bp, d), lambda l, j: (0, 0)),
                  pl.BlockSpec((None, d, tn), lambda l, j: (l, 0, j)),
                  pl.BlockSpec((None, 1, tn), lambda l, j: (l, 0, j))],
        out_specs=pl.BlockSpec((None, bp, tn), lambda l, j: (l, 0, j)),
        compiler_params=_cparams(("arbitrary", "arbitrary")),
        name="adaln_mod",
    )(c_all, w_mod, b_mod.reshape(depth, 1, n))


def _norm_mod(x, g, sh, sc):
    y = x * lax.rsqrt(jnp.mean(x * x, axis=-1, keepdims=True) + EPS) * g
    return y * (1.0 + sc) + sh


def _mod_spec(piece, d):
    return pl.BlockSpec((None, 1, d), lambda b, t: (2 * b + jnp.minimum(t, 1), 0, piece))


def _mod_spec_lat(piece, d):
    return pl.BlockSpec((None, 1, d), lambda b, t: (2 * b + 1, 0, piece))


def _prenorm_kernel(ctx_ref, x_ref, g_ref, sh_ref, sc_ref, h_ref):
    t = pl.program_id(1)

    def emit(xv):
        h_ref[...] = _norm_mod(xv, g_ref[...], sh_ref[...], sc_ref[...]).astype(BF16)

    @pl.when(t == 0)
    def _():
        emit(ctx_ref[...])

    @pl.when(t > 0)
    def _():
        emit(x_ref[...])


def _prenorm(ctx, x, g, mods):
    b, c, d = ctx.shape
    s = x.shape[1]
    tm = TOKEN_TILE
    nt = (c + s) // tm
    return pl.pallas_call(
        _prenorm_kernel,
        out_shape=jax.ShapeDtypeStruct((b, c + s, d), BF16),
        grid=(b, nt),
        in_specs=[pl.BlockSpec((None, tm, d), lambda i, t: (i, 0, 0)),
                  pl.BlockSpec((None, tm, d), lambda i, t: (i, jnp.maximum(t - 1, 0), 0)),
                  pl.BlockSpec((1, d), lambda i, t: (0, 0)),
                  _mod_spec(0, d), _mod_spec(1, d)],
        out_specs=pl.BlockSpec((None, tm, d), lambda i, t: (i, t, 0)),
        compiler_params=_cparams(("parallel", "arbitrary")),
        name="prenorm",
    )(ctx, x, g.reshape(1, d), mods, mods)


def _swap16(xs, first16):
    return jnp.where(first16, pltpu.roll(xs, LANES - 16, axis=1), pltpu.roll(xs, 16, axis=1))


def _proj_rope_kernel(h_ref, w_ref, cos_ref, sin_ref, o_ref):
    j = pl.program_id(0)
    scale = jnp.where(j == 0, ATT_HEAD_DIM ** -0.5, 1.0).astype(F32)
    n_rows = h_ref.shape[0]
    lane = lax.broadcasted_iota(jnp.int32, (PROJ_ROWS, LANES), 1)
    first16 = (lane % 32) < 16

    def body(r, carry):
        r0 = pl.multiple_of(r * PROJ_ROWS, PROJ_ROWS)
        acc = jnp.dot(h_ref[pl.ds(r0, PROJ_ROWS), :], w_ref[...], preferred_element_type=F32)
        cos = cos_ref[pl.ds(r0, PROJ_ROWS), :] * scale
        sin = sin_ref[pl.ds(r0, PROJ_ROWS), :] * scale
        for k in range(acc.shape[1] // LANES):
            xs = acc[:, k * LANES:(k + 1) * LANES]
            o_ref[pl.ds(r0, PROJ_ROWS), k * LANES:(k + 1) * LANES] = (
                xs * cos + _swap16(xs, first16) * sin).astype(o_ref.dtype)
        return carry

    lax.fori_loop(0, n_rows // PROJ_ROWS, body, 0)


def _proj_plain_kernel(h_ref, w_ref, o_ref):
    n_rows = h_ref.shape[0]

    def body(r, carry):
        r0 = pl.multiple_of(r * PROJ_ROWS, PROJ_ROWS)
        acc = jnp.dot(h_ref[pl.ds(r0, PROJ_ROWS), :], w_ref[...], preferred_element_type=F32)
        o_ref[pl.ds(r0, PROJ_ROWS), :] = acc.astype(o_ref.dtype)
        return carry

    lax.fori_loop(0, n_rows // PROJ_ROWS, body, 0)


def _proj_forget_kernel(h_ref, w_ref, lb_ref, k_ref, g_ref):
    n_rows = h_ref.shape[0]
    lb = lb_ref[...]
    log_lb = jnp.log(lb)
    log_1m_lb = jnp.log1p(-lb)

    def body(r, carry):
        r0 = pl.multiple_of(r * PROJ_ROWS, PROJ_ROWS)
        z = jnp.dot(h_ref[pl.ds(r0, PROJ_ROWS), :], w_ref[...], preferred_element_type=F32)
        e = jnp.exp(-jnp.abs(z))
        log_sig = jnp.minimum(z, 0.0) - jnp.log1p(e)
        sig_neg = jnp.where(z >= 0.0, e, 1.0) / (1.0 + e)
        bterm = log_1m_lb + log_sig
        mx = jnp.maximum(log_lb, bterm)
        g_ref[pl.ds(r0, PROJ_ROWS), :] = mx + jnp.log1p(jnp.exp(-jnp.abs(log_lb - bterm)))
        k_ref[pl.ds(r0, PROJ_ROWS), :] = ((1.0 - lb) * sig_neg).astype(k_ref.dtype)
        return carry

    lax.fori_loop(0, n_rows // PROJ_ROWS, body, 0)


def _proj_gates_kernel(h_ref, w_ref, o_ref):
    j = pl.program_id(0)
    n_rows = h_ref.shape[0]

    def body(r, carry):
        r0 = pl.multiple_of(r * PROJ_ROWS, PROJ_ROWS)
        z = jnp.dot(h_ref[pl.ds(r0, PROJ_ROWS), :], w_ref[...], preferred_element_type=F32)
        sg = _sigmoid(z)
        o_ref[pl.ds(r0, PROJ_ROWS), :] = jnp.where(j == 0, z * sg, sg).astype(o_ref.dtype)
        return carry

    lax.fori_loop(0, n_rows // PROJ_ROWS, body, 0)


def _project(kernel, h, w_bf, first_split, n_splits, extra_inputs, extra_specs, out_dtypes, name):
    b, l, d = h.shape
    wcol = w_bf.shape[1] // N_SPLITS
    out_shape = [jax.ShapeDtypeStruct((b, l, n_splits * wcol), dt) for dt in out_dtypes]
    out_specs = [pl.BlockSpec((None, l, wcol), lambda j, i: (i, 0, j)) for _ in out_dtypes]
    res = pl.pallas_call(
        kernel,
        out_shape=out_shape,
        grid=(n_splits, b),
        in_specs=[pl.BlockSpec((None, l, d), lambda j, i: (i, 0, 0)),
                  pl.BlockSpec((d, wcol), lambda j, i: (0, first_split + j))] + extra_specs,
        out_specs=out_specs,
        compiler_params=_cparams(("arbitrary", "arbitrary")),
        name=name,
    )(h, w_bf, *extra_inputs)
    return res


def _attn_kernel(lam_ref, q_ref, k_ref, v_ref, g_ref, o_ref, *, n_ctx, ctx_tile, out_scale):
    lam = lam_ref[0]
    q = q_ref[...]
    lane = lax.broadcasted_iota(jnp.int32, q.shape, 1)
    zero = jnp.zeros_like(q)
    q1 = jnp.where(lane < ATT_HEAD_DIM, q, zero)
    q2 = jnp.where(lane < ATT_HEAD_DIM, zero, q)
    nt = (((1,), (1,)), ((), ()))

    def attend(k, v):
        def softmax_map(qm):
            s = lax.dot_general(qm, k, nt, preferred_element_type=F32)
            p = jnp.exp(s - jnp.max(s, axis=-1, keepdims=True))
            return p, jnp.sum(p, axis=-1, keepdims=True)

        p1, l1 = softmax_map(q1)
        p2, l2 = softmax_map(q2)
        w = p1 * (1.0 / l1) - p2 * (lam / l2)
        o = jnp.dot(w.astype(BF16), v, preferred_element_type=F32)
        o = o * lax.rsqrt(jnp.mean(o * o, axis=-1, keepdims=True) + EPS) * g_ref[...] * out_scale
        o_ref[...] = o.astype(o_ref.dtype)

    if ctx_tile:
        t = pl.program_id(2)

        @pl.when(t == 0)
        def _():
            attend(k_ref[0:n_ctx, :], v_ref[0:n_ctx, :])

        @pl.when(t > 0)
        def _():
            attend(k_ref[...], v_ref[...])
    else:
        attend(k_ref[...], v_ref[...])


def _attention(qk, vq, lam, g_subln, lam_init, n_ctx, with_ctx):
    b, l, _ = qk.shape
    nh = ATT_HEADS
    tq = TOKEN_TILE
    off = 0 if with_ctx else n_ctx // tq
    n_out = l if with_ctx else l - n_ctx
    kern = functools.partial(_attn_kernel, n_ctx=n_ctx, ctx_tile=with_ctx, out_scale=1.0 - lam_init)
    return pl.pallas_call(
        kern,
        out_shape=jax.ShapeDtypeStruct((b, n_out, nh * HEAD_W), BF16),
        grid=(b, nh, n_out // tq),
        in_specs=[pl.BlockSpec(memory_space=pltpu.SMEM),
                  pl.BlockSpec((None, tq, HEAD_W), lambda i, h, t: (i, t + off, h)),
                  pl.BlockSpec((None, l, HEAD_W), lambda i, h, t: (i, 0, nh + h)),
                  pl.BlockSpec((None, l, HEAD_W), lambda i, h, t: (i, 0, h)),
                  pl.BlockSpec((1, HEAD_W), lambda i, h, t: (0, 0))],
        out_specs=pl.BlockSpec((None, tq, HEAD_W), lambda i, h, t: (i, t, h)),
        compiler_params=_cparams(("parallel", "arbitrary", "arbitrary")),
        name="diff_attention",
    )(lam.reshape(1), qk, qk, vq, g_subln.reshape(1, HEAD_W))


def _gla_tables(chunk):
    c = chunk
    levels = []
    b = c // 2
    while b >= 1:
        levels.append(b)
        b //= 2
    t = np.arange(c)
    u = np.arange(c)[None, :]

    def build(forward):
        mats, qrows, pmask = [], [], []
        for b in levels:
            pair = t // (2 * b)
            late = (t % (2 * b)) >= b
            m = pair * 2 * b + b - 1
            if forward:
                is_q = late
                lo = np.where(is_q, m + 1, t + 1)[:, None]
                hi = np.where(is_q, t, m)[:, None]
            else:
                is_q = ~late
                lo = np.where(is_q, t, m + 1)[:, None]
                hi = np.where(is_q, m, t - 1)[:, None]
            mats.append(((u >= lo) & (u <= hi)).astype(np.float32))
            qrows.append(np.repeat(is_q[:, None], LANES, 1).astype(np.float32))
            pmask.append(((pair[:, None] == pair[None, :]) & is_q[:, None] & (~is_q)[None, :]).astype(np.float32))
        tt = t[:, None]
        if forward:
            q_state = (u <= tt)
            k_state = (u > tt)
        else:
            q_state = (u >= tt)
            k_state = (u < tt)
        mats.append(q_state.astype(np.float32))
        mats.append(k_state.astype(np.float32))
        return np.concatenate(mats, 0), np.stack(qrows), np.stack(pmask)

    mf, qf, pf = build(True)
    mb, qb, pb = build(False)
    return len(levels), np.stack([mf, mb]), np.stack([qf, qb]), np.stack([pf, pb])


def _gla_kernel(q_ref, kf_ref, kb_ref, v_ref, gf_ref, gb_ref, gate_ref, gn_ref, m_ref, qrow_ref, pm_ref,
                o_ref, of_ref, ob_ref, sf_ref, sb_ref, *, n_levels, n_ctx_chunks, n_chunks):
    c = GLA_CHUNK
    sf_ref[...] = jnp.zeros_like(sf_ref)
    sb_ref[...] = jnp.zeros_like(sb_ref)
    tn = (((0,), (0,)), ((), ()))
    nt = (((1,), (1,)), ((), ()))

    def chunk_step(direction, r0, k_ref, g_ref, s_ref, out_ref):
        q = q_ref[pl.ds(r0, c), :].astype(F32)
        k = k_ref[pl.ds(r0, c), :].astype(F32)
        v = v_ref[pl.ds(r0, c), :]
        g = g_ref[pl.ds(r0, c), :]
        g_hi = g.astype(BF16)
        r1 = g - g_hi.astype(F32)
        g_mid = r1.astype(BF16)
        g_lo = (r1 - g_mid.astype(F32)).astype(BF16)
        m = m_ref[direction]
        expo = (jnp.dot(m, g_hi, preferred_element_type=F32)
                + jnp.dot(m, g_mid, preferred_element_type=F32)
                + jnp.dot(m, g_lo, preferred_element_type=F32))
        dec = jnp.exp(expo)
        out = jnp.sum(q * k, axis=-1, keepdims=True) * v.astype(F32)
        scores = jnp.zeros((c, c), F32)
        for lv in range(n_levels):
            is_q = qrow_ref[direction, lv] > 0.5
            x = (jnp.where(is_q, q, k) * dec[lv * c:(lv + 1) * c, :]).astype(BF16)
            scores = scores + pm_ref[direction, lv] * lax.dot_general(x, x, nt, preferred_element_type=F32)
        out = out + jnp.dot(scores.astype(BF16), v, preferred_element_type=F32)
        q_dec = dec[n_levels * c:(n_levels + 1) * c, :]
        k_dec = dec[(n_levels + 1) * c:(n_levels + 2) * c, :]
        state_t = s_ref[...]
        out = out + lax.dot_general((q * q_dec).astype(BF16), state_t.astype(BF16), nt,
                                    preferred_element_type=F32)
        out_ref[pl.ds(r0, c), :] = out
        total = q_dec[c - 1:c, :] if direction == 0 else q_dec[0:1, :]
        vk = lax.dot_general(v, (k * k_dec).astype(BF16), tn, preferred_element_type=F32)
        s_ref[...] = state_t * total + vk

    def body(i, carry):
        cb = jnp.where(i < n_ctx_chunks, n_ctx_chunks - 1 - i, n_chunks - 1 - (i - n_ctx_chunks))
        chunk_step(0, pl.multiple_of(i * c, c), kf_ref, gf_ref, sf_ref, of_ref)
        chunk_step(1, pl.multiple_of(cb * c, c), kb_ref, gb_ref, sb_ref, ob_ref)
        return carry

    lax.fori_loop(0, n_chunks, body, 0)
    o = of_ref[...] + ob_ref[...]
    o = o * lax.rsqrt(jnp.mean(o * o, axis=-1, keepdims=True) + EPS) * gn_ref[...]
    o_ref[...] = (o * gate_ref[...].astype(F32)).astype(o_ref.dtype)


def _recurrence(vq, kk, gg, ri, gates, g_rec, n_ctx):
    b, l, _ = vq.shape
    c = GLA_CHUNK
    n_levels, m_np, qrow_np, pm_np = _gla_tables(c)
    nh = REC_HEADS
    kern = functools.partial(_gla_kernel, n_levels=n_levels, n_ctx_chunks=n_ctx // c, n_chunks=l // c)
    head = lambda off: pl.BlockSpec((None, l, LANES), lambda i, h: (i, 0, h + off))
    const = lambda a: pl.BlockSpec(a.shape, lambda i, h: (0,) * a.ndim)
    m_all = jnp.asarray(m_np, BF16)
    qrow = jnp.asarray(qrow_np, F32)
    pm = jnp.asarray(pm_np, F32)
    return pl.pallas_call(
        kern,
        out_shape=jax.ShapeDtypeStruct((b, l, nh * LANES), BF16),
        grid=(b, nh),
        in_specs=[head(nh), head(0), head(nh), head(0), head(0), head(nh), head(0),
                  pl.BlockSpec((1, LANES), lambda i, h: (0, 0)),
                  const(m_all), const(qrow), const(pm)],
        out_specs=head(0),
        scratch_shapes=[pltpu.VMEM((l, LANES), F32), pltpu.VMEM((l, LANES), F32),
                        pltpu.VMEM((LANES, LANES), F32), pltpu.VMEM((LANES, LANES), F32)],
        compiler_params=_cparams(("parallel", "arbitrary")),
        name="hgrn2_recurrence",
    )(vq, kk, kk, ri, gg, gg, gates, g_rec.reshape(1, LANES), m_all, qrow, pm)


def _route(logits_t, bias):
    scores = _sigmoid(logits_t)
    biased = scores + bias
    row = lambda a, e: a[e:e + 1, :]
    epg = EXPERTS_PER_GROUP
    pairs = [(i, j) for i in range(epg) for j in range(i + 1, epg)]
    best_gs, grp = None, None
    for gi in range(N_GROUPS):
        gs = None
        for (i, j) in pairs:
            s = row(biased, gi * epg + i) + row(biased, gi * epg + j)
            gs = s if gs is None else jnp.maximum(gs, s)
        if best_gs is None:
            best_gs, grp = gs, jnp.zeros_like(gs, dtype=jnp.int32)
        else:
            better = gs > best_gs
            grp = jnp.where(better, gi, grp)
            best_gs = jnp.maximum(best_gs, gs)

    def in_group(a, j):
        out = row(a, j)
        for gi in range(1, N_GROUPS):
            out = jnp.where(grp == gi, row(a, gi * epg + j), out)
        return out

    u = [in_group(biased, j) for j in range(epg)]
    sc = [in_group(scores, j) for j in range(epg)]
    b1, i1 = u[0], jnp.zeros_like(grp)
    for j in range(1, epg):
        better = u[j] > b1
        i1 = jnp.where(better, j, i1)
        b1 = jnp.maximum(b1, u[j])
    i2 = jnp.where(i1 == 0, 1, 0)
    b2 = jnp.where(i1 == 0, u[1], u[0])
    for j in range(1, epg):
        better = (i1 != j) & (u[j] > b2)
        i2 = jnp.where(better, j, i2)
        b2 = jnp.where(better, u[j], b2)
    pick = lambda idx: sum(jnp.where(idx == j, sc[j], 0.0) for j in range(epg))
    w1, w2 = pick(i1), pick(i2)
    tot = w1 + w2
    w1, w2 = w1 / tot, w2 / tot
    la, lb2 = jnp.minimum(i1, i2), jnp.maximum(i1, i2)
    wa = jnp.where(i1 < i2, w1, w2)
    wb = jnp.where(i1 < i2, w2, w1)
    pair_idx = jnp.where(la == 0, lb2 - 1, jnp.where(la == 1, lb2 + 1, 5))
    cls = grp * 6 + pair_idx
    ea = grp * epg + la
    eb = grp * epg + lb2
    zeros = jnp.zeros_like(wa)
    return jnp.concatenate([cls.astype(F32), ea.astype(F32), eb.astype(F32), wa, wb, zeros, zeros, zeros], axis=0)


def _merge_kernel(*refs, two_sources):
    if two_sources:
        (ctx_ref, x_ref, a_ref, r_ref, ga_ref, gr_ref, gt_ref, sh_ref, sc_ref, g2_ref,
         wa_ref, wr_ref, wo_ref, wrt_ref, br_ref, xn_ref, h2_ref, route_ref) = refs
    else:
        (x_ref, a_ref, r_ref, ga_ref, gr_ref, gt_ref, sh_ref, sc_ref, g2_ref,
         wa_ref, wr_ref, wo_ref, wrt_ref, br_ref, xn_ref, h2_ref, route_ref) = refs

    def emit(xv):
        ua = jnp.dot(a_ref[...], wa_ref[...], preferred_element_type=F32)
        ur = jnp.dot(r_ref[...], wr_ref[...], preferred_element_type=F32)
        mixed = ga_ref[...].astype(F32) * ua + gr_ref[...].astype(F32) * ur
        y = jnp.dot(mixed.astype(BF16), wo_ref[...], preferred_element_type=F32)
        xn = xv + gt_ref[...] * y
        xn_ref[...] = xn
        h2 = _norm_mod(xn, g2_ref[...], sh_ref[...], sc_ref[...])
        h2_ref[...] = h2
        logits_t = lax.dot_general(wrt_ref[...], h2, (((1,), (1,)), ((), ())),
                                   precision=lax.Precision.HIGHEST, preferred_element_type=F32)
        route_ref[...] = _route(logits_t, br_ref[...])

    if two_sources:
        t = pl.program_id(1)

        @pl.when(t == 0)
        def _():
            emit(ctx_ref[...])

        @pl.when(t > 0)
        def _():
            emit(x_ref[...])
    else:
        emit(x_ref[...])


def _merge(res_inputs, a, r, gates, mods, g2, wa, wr, wo, wrt, br, n_ctx, with_ctx):
    b, l_a, d = a.shape
    l = gates.shape[1]
    tm = TOKEN_TILE
    off = 0 if with_ctx else n_ctx // tm
    nt = l // tm - off
    n_tok = b * nt * tm
    full = lambda shape: pl.BlockSpec(shape, lambda i, t: (0,) * len(shape))
    tile = lambda o: pl.BlockSpec((None, tm, d), lambda i, t: (i, t + o, 0))
    two_sources = len(res_inputs) == 2 and with_ctx
    if two_sources:
        res_specs = [pl.BlockSpec((None, tm, d), lambda i, t: (i, 0, 0)),
                     pl.BlockSpec((None, tm, d), lambda i, t: (i, jnp.maximum(t - 1, 0), 0))]
    elif len(res_inputs) == 2:
        res_inputs = res_inputs[1:]
        res_specs = [tile(0)]
    else:
        res_specs = [tile(off)]
    mspec = _mod_spec if with_ctx else _mod_spec_lat
    a_off = off - (l - l_a) // tm
    kern = functools.partial(_merge_kernel, two_sources=two_sources)
    return pl.pallas_call(
        kern,
        out_shape=[jax.ShapeDtypeStruct((b, nt * tm, d), F32),
                   jax.ShapeDtypeStruct((n_tok, d), F32),
                   jax.ShapeDtypeStruct((8, n_tok), F32)],
        grid=(b, nt),
        in_specs=res_specs + [
            tile(a_off), tile(off),
            pl.BlockSpec((None, tm, d), lambda i, t: (i, t + off, 1)),
            pl.BlockSpec((None, tm, d), lambda i, t: (i, t + off, 2)),
            mspec(2, d), mspec(3, d), mspec(4, d), full((1, d)),
            full((d, d)), full((d, d)), full((d, d)), full((N_EXPERTS, d)), full((N_EXPERTS, 1))],
        out_specs=[pl.BlockSpec((None, tm, d), lambda i, t: (i, t, 0)),
                   pl.BlockSpec((tm, d), lambda i, t: (i * nt + t, 0)),
                   pl.BlockSpec((8, tm), lambda i, t: (0, i * nt + t))],
        compiler_params=_cparams(("parallel", "arbitrary")),
        name="merge_route",
    )(*res_inputs, a, r, gates, gates, mods, mods, mods, g2.reshape(1, d), wa, wr, wo, wrt, br)


def _moe_kernel(tile_a_ref, tile_b_ref, tile_on_ref, tok_ref, h_hbm, wts_ref,
                wga_ref, wua_ref, wda_ref, wgb_ref, wub_ref, wdb_ref, out_hbm,
                xbuf, ybuf, sem_in, sem_out, *, n_tok):
    i = pl.program_id(0)
    tm = MOE_TILE
    base = i * tm

    def gather_copy(r):
        src = jnp.minimum(tok_ref[base + r], n_tok - 1)
        return pltpu.make_async_copy(h_hbm.at[pl.ds(src, 1), :], xbuf.at[pl.ds(r, 1), :], sem_in)

    def scatter_copy(r):
        return pltpu.make_async_copy(ybuf.at[pl.ds(r, 1), :], out_hbm.at[pl.ds(tok_ref[base + r], 1), :], sem_out)

    @pl.when(i == 0)
    def _():
        ybuf[...] = jnp.zeros_like(ybuf)
        pad_fill = pltpu.make_async_copy(ybuf, out_hbm.at[pl.ds(n_tok, tm), :], sem_out)
        pad_fill.start()
        pad_fill.wait()

    @pl.when(tile_on_ref[i] > 0)
    def _():
        def start_in(r, c):
            gather_copy(r).start()
            return c

        def wait_in(r, c):
            gather_copy(r).wait()
            return c

        lax.fori_loop(0, tm, start_in, 0)
        lax.fori_loop(0, tm, wait_in, 0)
        x = xbuf[...].astype(BF16)

        def expert(wg_ref, wu_ref, wd_ref):
            gate = jnp.dot(x, wg_ref[...], preferred_element_type=F32)
            up = jnp.dot(x, wu_ref[...], preferred_element_type=F32)
            hid = (gate * _sigmoid(gate) * up).astype(BF16)
            return jnp.dot(hid, wd_ref[...], preferred_element_type=F32)

        wts = wts_ref[...]
        ybuf[...] = (wts[:, 0:1] * expert(wga_ref, wua_ref, wda_ref)
                     + wts[:, 1:2] * expert(wgb_ref, wub_ref, wdb_ref))

        def start_out(r, c):
            scatter_copy(r).start()
            return c

        def wait_out(r, c):
            scatter_copy(r).wait()
            return c

        lax.fori_loop(0, tm, start_out, 0)
        lax.fori_loop(0, tm, wait_out, 0)


def _moe_plan(route, n_tok):
    tm = MOE_TILE
    cls = route[0].astype(jnp.int32)
    order = jnp.argsort(cls).astype(jnp.int32)
    counts = jnp.zeros((N_PAIR_CLASSES,), jnp.int32).at[cls].add(1)
    starts = jnp.cumsum(counts) - counts
    padded = ((counts + tm - 1) // tm) * tm
    pends = jnp.cumsum(padded)
    pstarts = pends - padded
    n_tiles = n_tok // tm + N_PAIR_CLASSES
    rows = jnp.arange(n_tiles * tm, dtype=jnp.int32)
    rcls = jnp.minimum(jnp.searchsorted(pends, rows, side="right"), N_PAIR_CLASSES - 1).astype(jnp.int32)
    rank = rows - pstarts[rcls]
    valid = (rows < pends[-1]) & (rank < counts[rcls])
    tok = jnp.where(valid, order[jnp.clip(starts[rcls] + rank, 0, n_tok - 1)], n_tok + rows % tm)
    safe = jnp.minimum(tok, n_tok - 1)
    wts = jnp.where(valid[:, None], jnp.stack([route[3][safe], route[4][safe]], axis=1), 0.0)
    tile_rows = rows[::tm]
    tile_on = valid[::tm].astype(jnp.int32)
    first = safe[::tm]
    ea = route[1][first].astype(jnp.int32)
    eb = route[2][first].astype(jnp.int32)
    last_on = jnp.maximum(jnp.sum(tile_on) - 1, 0)
    ea = jnp.where(tile_on > 0, ea, ea[last_on])
    eb = jnp.where(tile_on > 0, eb, eb[last_on])
    del tile_rows
    return ea, eb, tile_on, tok, wts, n_tiles


def _moe(h2, route, wg, wu, wd):
    n_tok, d = h2.shape
    de = wg.shape[2]
    tm = MOE_TILE
    ea, eb, tile_on, tok, wts, n_tiles = _moe_plan(route, n_tok)
    wspec_a = lambda r, c: pl.BlockSpec((None, r, c), lambda i, ta, tb, on, tk: (ta[i], 0, 0))
    wspec_b = lambda r, c: pl.BlockSpec((None, r, c), lambda i, ta, tb, on, tk: (tb[i], 0, 0))
    grid_spec = pltpu.PrefetchScalarGridSpec(
        num_scalar_prefetch=4,
        grid=(n_tiles,),
        in_specs=[pl.BlockSpec(memory_space=pl.ANY),
                  pl.BlockSpec((tm, 2), lambda i, ta, tb, on, tk: (i, 0)),
                  wspec_a(d, de), wspec_a(d, de), wspec_a(de, d),
                  wspec_b(d, de), wspec_b(d, de), wspec_b(de, d)],
        out_specs=pl.BlockSpec(memory_space=pl.ANY),
        scratch_shapes=[pltpu.VMEM((tm, d), F32), pltpu.VMEM((tm, d), F32),
                        pltpu.SemaphoreType.DMA, pltpu.SemaphoreType.DMA],
    )
    return pl.pallas_call(
        functools.partial(_moe_kernel, n_tok=n_tok),
        out_shape=jax.ShapeDtypeStruct((n_tok + tm, d), F32),
        grid_spec=grid_spec,
        compiler_params=pltpu.CompilerParams(dimension_semantics=("arbitrary",), vmem_limit_bytes=VMEM_LIMIT,
                                             has_side_effects=True),
        name="moe_pairs",
    )(ea, eb, tile_on, tok, h2, wts, wg, wu, wd, wg, wu, wd)


def _post_kernel(x_ref, y_ref, gt_ref, g_ref, sh_ref, sc_ref, xo_ref, h_ref):
    xn = x_ref[...] + gt_ref[...] * y_ref[...]
    xo_ref[...] = xn
    h_ref[...] = _norm_mod(xn, g_ref[...], sh_ref[...], sc_ref[...]).astype(h_ref.dtype)


def _final_kernel(x_ref, y_ref, gt_ref, g_ref, o_ref):
    xn = x_ref[...] + gt_ref[...] * y_ref[...]
    o_ref[...] = xn * lax.rsqrt(jnp.mean(xn * xn, axis=-1, keepdims=True) + EPS) * g_ref[...]


def _post(x_all, moe_out, mods, mods_next, g_next):
    b, l, d = x_all.shape
    tm = TOKEN_TILE
    nt = l // tm
    tile = pl.BlockSpec((None, tm, d), lambda i, t: (i, t, 0))
    return pl.pallas_call(
        _post_kernel,
        out_shape=[jax.ShapeDtypeStruct((b, l, d), F32), jax.ShapeDtypeStruct((b, l, d), BF16)],
        grid=(b, nt),
        in_specs=[tile, pl.BlockSpec((tm, d), lambda i, t: (i * nt + t, 0)),
                  _mod_spec(5, d), pl.BlockSpec((1, d), lambda i, t: (0, 0)), _mod_spec(0, d), _mod_spec(1, d)],
        out_specs=[tile, tile],
        compiler_params=_cparams(("parallel", "arbitrary")),
        name="moe_residual_prenorm",
    )(x_all, moe_out, mods, g_next.reshape(1, d), mods_next, mods_next)


def _final(x_lat, moe_out, mods, g_final):
    b, s, d = x_lat.shape
    tm = TOKEN_TILE
    nt = s // tm
    tile = pl.BlockSpec((None, tm, d), lambda i, t: (i, t, 0))
    return pl.pallas_call(
        _final_kernel,
        out_shape=jax.ShapeDtypeStruct((b, s, d), F32),
        grid=(b, nt),
        in_specs=[tile, pl.BlockSpec((tm, d), lambda i, t: (i * nt + t, 0)),
                  _mod_spec_lat(5, d), pl.BlockSpec((1, d), lambda i, t: (0, 0))],
        out_specs=tile,
        compiler_params=_cparams(("parallel", "arbitrary")),
        name="moe_residual_final_norm",
    )(x_lat, moe_out, mods, g_final.reshape(1, d))


def _rope_tables(n_ctx, n_lat):
    t = np.arange(n_lat)
    n_freq = ATT_HEAD_DIM // 4
    inv = (ROPE_BASE ** (-np.arange(n_freq, dtype=np.float32) / n_freq)).astype(np.float32)
    ang_r = (t // GRID_W).astype(np.float32)[:, None] * inv
    ang_c = (t % GRID_W).astype(np.float32)[:, None] * inv
    ang = np.concatenate([ang_r, ang_r, ang_c, ang_c], axis=1)
    ang = np.concatenate([ang, ang], axis=1)
    sign = np.where((np.arange(HEAD_W) % 32) < 16, -1.0, 1.0).astype(np.float32)
    cos = np.concatenate([np.ones((n_ctx, HEAD_W), np.float32), np.cos(ang)], axis=0)
    sin = np.concatenate([np.zeros((n_ctx, HEAD_W), np.float32), np.sin(ang) * sign], axis=0)
    return jnp.asarray(cos, F32), jnp.asarray(sin, F32)


def kernel(x, c, ctx, c_ctx, w_mod, b_mod, g_norm1, g_norm2, w_in, lambda_q1, lambda_k1, lambda_q2, lambda_k2,
           g_subln, lb_logits, g_rec_norm, w_br_attn, w_br_rec, w_out, w_router, b_router, w_gate, w_up, w_down,
           g_final):
    b, s, d = x.shape
    n_ctx = ctx.shape[1]
    depth = w_in.shape[0]
    assert n_ctx == TOKEN_TILE and s % TOKEN_TILE == 0 and s % GRID_W == 0
    assert w_in.shape[2] == N_SPLITS * d

    bp = -(-(b + 1) // 8) * 8
    c_all = jnp.concatenate([c, c_ctx[None], jnp.zeros((bp - b - 1, d), F32)], axis=0)
    mod_all = _modulation(c_all, w_mod, b_mod)
    mods = [jnp.stack([jnp.broadcast_to(mod_all[l, b], (b, 6 * d)), mod_all[l, :b]], axis=1)
            .reshape(2 * b, 1, 6 * d) for l in range(depth)]

    lbs = jnp.cumsum(jax.nn.softmax(lb_logits.astype(F32), axis=0), axis=0)
    lbs = lbs - lbs[:1]
    cos_t, sin_t = _rope_tables(n_ctx, s)
    wrt = jnp.transpose(w_router).astype(F32)
    br = b_router.astype(F32).reshape(N_EXPERTS, 1)
    l_all = n_ctx + s
    full2 = lambda shape: pl.BlockSpec(shape, lambda j, i: (0, 0))

    x_all = None
    out = None
    h = _prenorm(ctx, x, g_norm1[0], mods[0])
    for l in range(depth):
        last = l == depth - 1
        lam_init = 0.8 - 0.6 * math.exp(-0.3 * l)
        lam = (jnp.exp(jnp.sum(lambda_q1[l].astype(F32) * lambda_k1[l].astype(F32)))
               - jnp.exp(jnp.sum(lambda_q2[l].astype(F32) * lambda_k2[l].astype(F32))) + lam_init)
        w_bf = w_in[l].astype(BF16)
        (qk,) = _project(_proj_rope_kernel, h, w_bf, 0, 2, [cos_t, sin_t],
                         [full2((l_all, HEAD_W)), full2((l_all, HEAD_W))], [BF16], "proj_rope")
        (vq,) = _project(_proj_plain_kernel, h, w_bf, 2, 2, [], [], [BF16], "proj_v_rq")
        kk, gg = _project(_proj_forget_kernel, h, w_bf, 4, 2, [lbs[l].reshape(1, d)],
                          [full2((1, d))], [BF16, F32], "proj_forget")
        (ri,) = _project(_proj_plain_kernel, h, w_bf, 6, 1, [], [], [BF16], "proj_ri")
        (gates,) = _project(_proj_gates_kernel, h, w_bf, 7, 3, [], [], [BF16], "proj_gates")

        a = _attention(qk, vq, lam, g_subln[l], lam_init, n_ctx, with_ctx=not last)
        r = _recurrence(vq, kk, gg, ri, gates, g_rec_norm[l], n_ctx)

        res = (ctx, x) if l == 0 else (x_all,)
        xn, h2, route = _merge(res, a, r, gates, mods[l], g_norm2[l], w_br_attn[l].astype(BF16),
                               w_br_rec[l].astype(BF16), w_out[l].astype(BF16), wrt, br, n_ctx,
                               with_ctx=not last)
        moe_out = _moe(h2, route, w_gate[l].astype(BF16), w_up[l].astype(BF16), w_down[l].astype(BF16))
        if last:
            out = _final(xn, moe_out, mods[l], g_final)
        else:
            x_all, h = _post(xn, moe_out, mods[l], mods[l + 1], g_norm1[l + 1])
    return out
```

```python
import functools
import math

import numpy as np
import jax
import jax.numpy as jnp
from jax import lax
from jax.experimental import pallas as pl
from jax.experimental.pallas import tpu as pltpu

EPS = 1e-6
GRID_W = 64
ROPE_BASE = 10000.0
ATT_HEADS = 8
ATT_HEAD_DIM = 64
HEAD_W = 2 * ATT_HEAD_DIM
REC_HEADS = 8
N_EXPERTS = 16
N_GROUPS = 4
EXPERTS_PER_GROUP = N_EXPERTS // N_GROUPS
N_PAIR_CLASSES = N_GROUPS * 6
N_SPLITS = 10

LANES = 128
TOKEN_TILE = 256
PROJ_ROWS = 256
ATT_HEADS_PER_STEP = 2
GLA_CHUNK = 64
GLA_GROUP = 4
GLA_MIN_BCAST_BLOCK = 4
MOE_TILE = 256
VMEM_LIMIT = 56 * 1024 * 1024

F32 = jnp.float32
BF16 = jnp.bfloat16


def _sigmoid(x):
    return 1.0 / (1.0 + jnp.exp(-x))


def _cparams(sem):
    return pltpu.CompilerParams(dimension_semantics=sem, vmem_limit_bytes=VMEM_LIMIT)


def _mod_kernel(c_ref, w_ref, b_ref, o_ref):
    c = c_ref[...]
    s = c * _sigmoid(c)
    o_ref[...] = jnp.dot(s, w_ref[...], precision=lax.Precision.HIGHEST,
                         preferred_element_type=F32) + b_ref[...]


def _modulation(c_all, w_mod, b_mod):
    depth, d, n = w_mod.shape
    bp = c_all.shape[0]
    tn = 1024
    return pl.pallas_call(
        _mod_kernel,
        out_shape=jax.ShapeDtypeStruct((depth, bp, n), F32),
        grid=(depth, n // tn),
        in_specs=[pl.BlockSpec((bp, d), lambda l, j: (0, 0)),
                  pl.BlockSpec((None, d, tn), lambda l, j: (l, 0, j)),
                  pl.BlockSpec((None, 1, tn), lambda l, j: (l, 0, j))],
        out_specs=pl.BlockSpec((None, bp, tn), lambda l, j: (l, 0, j)),
        compiler_params=_cparams(("arbitrary", "arbitrary")),
        name="adaln_mod",
    )(c_all, w_mod, b_mod.reshape(depth, 1, n))


def _norm_mod(x, g, sh, sc):
    y = x * lax.rsqrt(jnp.mean(x * x, axis=-1, keepdims=True) + EPS) * g
    return y * (1.0 + sc) + sh


def _mod_spec(piece, d):
    return pl.BlockSpec((None, 1, d), lambda b, t: (2 * b + jnp.minimum(t, 1), 0, piece))


def _mod_spec_lat(piece, d):
    return pl.BlockSpec((None, 1, d), lambda b, t: (2 * b + 1, 0, piece))


def _prenorm_kernel(ctx_ref, x_ref, g_ref, sh_ref, sc_ref, h_ref):
    t = pl.program_id(1)

    def emit(xv):
        h_ref[...] = _norm_mod(xv, g_ref[...], sh_ref[...], sc_ref[...]).astype(BF16)

    @pl.when(t == 0)
    def _():
        emit(ctx_ref[...])

    @pl.when(t > 0)
    def _():
        emit(x_ref[...])


def _prenorm(ctx, x, g, mods):
    b, c, d = ctx.shape
    s = x.shape[1]
    tm = TOKEN_TILE
    nt = (c + s) // tm
    return pl.pallas_call(
        _prenorm_kernel,
        out_shape=jax.ShapeDtypeStruct((b, c + s, d), BF16),
        grid=(b, nt),
        in_specs=[pl.BlockSpec((None, tm, d), lambda i, t: (i, 0, 0)),
                  pl.BlockSpec((None, tm, d), lambda i, t: (i, jnp.maximum(t - 1, 0), 0)),
                  pl.BlockSpec((1, d), lambda i, t: (0, 0)),
                  _mod_spec(0, d), _mod_spec(1, d)],
        out_specs=pl.BlockSpec((None, tm, d), lambda i, t: (i, t, 0)),
        compiler_params=_cparams(("parallel", "arbitrary")),
        name="prenorm",
    )(ctx, x, g.reshape(1, d), mods, mods)


def _swap16(xs, first16):
    return jnp.where(first16, pltpu.roll(xs, LANES - 16, axis=1), pltpu.roll(xs, 16, axis=1))


def _proj_rope_kernel(h_ref, w_ref, cos_ref, sin_ref, o_ref):
    j = pl.program_id(0)
    scale = jnp.where(j == 0, ATT_HEAD_DIM ** -0.5 * math.log2(math.e), 1.0).astype(F32)
    n_rows = h_ref.shape[0]
    lane = lax.broadcasted_iota(jnp.int32, (PROJ_ROWS, LANES), 1)
    first16 = (lane % 32) < 16

    def body(r, carry):
        r0 = pl.multiple_of(r * PROJ_ROWS, PROJ_ROWS)
        acc = jnp.dot(h_ref[pl.ds(r0, PROJ_ROWS), :], w_ref[...], preferred_element_type=F32)
        cos = cos_ref[pl.ds(r0, PROJ_ROWS), :] * scale
        sin = sin_ref[pl.ds(r0, PROJ_ROWS), :] * scale
        for k in range(acc.shape[1] // LANES):
            xs = acc[:, k * LANES:(k + 1) * LANES]
            o_ref[pl.ds(r0, PROJ_ROWS), k * LANES:(k + 1) * LANES] = (
                xs * cos + _swap16(xs, first16) * sin).astype(o_ref.dtype)
        return carry

    lax.fori_loop(0, n_rows // PROJ_ROWS, body, 0)


def _proj_plain_kernel(h_ref, w_ref, o_ref):
    n_rows = h_ref.shape[0]

    def body(r, carry):
        r0 = pl.multiple_of(r * PROJ_ROWS, PROJ_ROWS)
        acc = jnp.dot(h_ref[pl.ds(r0, PROJ_ROWS), :], w_ref[...], preferred_element_type=F32)
        o_ref[pl.ds(r0, PROJ_ROWS), :] = acc.astype(o_ref.dtype)
        return carry

    lax.fori_loop(0, n_rows // PROJ_ROWS, body, 0)


def _proj_forget_kernel(h_ref, w_ref, lb_ref, k_ref, g_ref):
    n_rows = h_ref.shape[0]
    lb = lb_ref[...]
    log_lb = jnp.log(lb)
    log_1m_lb = jnp.log1p(-lb)

    def body(r, carry):
        r0 = pl.multiple_of(r * PROJ_ROWS, PROJ_ROWS)
        z = jnp.dot(h_ref[pl.ds(r0, PROJ_ROWS), :], w_ref[...], preferred_element_type=F32)
        e = jnp.exp(-jnp.abs(z))
        log_sig = jnp.minimum(z, 0.0) - jnp.log1p(e)
        sig_neg = jnp.where(z >= 0.0, e, 1.0) / (1.0 + e)
        bterm = log_1m_lb + log_sig
        mx = jnp.maximum(log_lb, bterm)
        g_ref[pl.ds(r0, PROJ_ROWS), :] = mx + jnp.log1p(jnp.exp(-jnp.abs(log_lb - bterm)))
        k_ref[pl.ds(r0, PROJ_ROWS), :] = ((1.0 - lb) * sig_neg).astype(k_ref.dtype)
        return carry

    lax.fori_loop(0, n_rows // PROJ_ROWS, body, 0)


def _proj_gates_kernel(h_ref, w_ref, o_ref):
    j = pl.program_id(0)
    n_rows = h_ref.shape[0]

    def body(r, carry):
        r0 = pl.multiple_of(r * PROJ_ROWS, PROJ_ROWS)
        z = jnp.dot(h_ref[pl.ds(r0, PROJ_ROWS), :], w_ref[...], preferred_element_type=F32)
        sg = _sigmoid(z)
        o_ref[pl.ds(r0, PROJ_ROWS), :] = jnp.where(j == 0, z * sg, sg).astype(o_ref.dtype)
        return carry

    lax.fori_loop(0, n_rows // PROJ_ROWS, body, 0)


def _project(kernel, h, w_bf, first_split, n_splits, extra_inputs, extra_specs, out_dtypes, name):
    b, l, d = h.shape
    wcol = w_bf.shape[1] // N_SPLITS
    out_shape = [jax.ShapeDtypeStruct((b, l, n_splits * wcol), dt) for dt in out_dtypes]
    out_specs = [pl.BlockSpec((None, l, wcol), lambda j, i: (i, 0, j)) for _ in out_dtypes]
    res = pl.pallas_call(
        kernel,
        out_shape=out_shape,
        grid=(n_splits, b),
        in_specs=[pl.BlockSpec((None, l, d), lambda j, i: (i, 0, 0)),
                  pl.BlockSpec((d, wcol), lambda j, i: (0, first_split + j))] + extra_specs,
        out_specs=out_specs,
        compiler_params=_cparams(("arbitrary", "arbitrary")),
        name=name,
    )(h, w_bf, *extra_inputs)
    return res


def _attn_kernel(lam_ref, q_ref, k_ref, v_ref, g_ref, o_ref, *, n_ctx, ctx_tile, out_scale):
    lam = lam_ref[0]
    nt = (((1,), (1,)), ((), ()))
    heads = range(ATT_HEADS_PER_STEP)
    col = lambda hh: slice(hh * HEAD_W, (hh + 1) * HEAD_W)

    def attend(n_keys):
        scores = []
        for hh in heads:
            q = q_ref[:, col(hh)]
            k = k_ref[0:n_keys, col(hh)]
            lane = lax.broadcasted_iota(jnp.int32, q.shape, 1)
            zero = jnp.zeros_like(q)
            q1 = jnp.where(lane < ATT_HEAD_DIM, q, zero)
            q2 = jnp.where(lane < ATT_HEAD_DIM, zero, q)
            scores.append((lax.dot_general(q1, k, nt, preferred_element_type=F32),
                           lax.dot_general(q2, k, nt, preferred_element_type=F32)))
        for hh in heads:
            s1, s2 = scores[hh]
            p1 = jnp.exp2(s1 - jnp.max(s1, axis=-1, keepdims=True))
            p2 = jnp.exp2(s2 - jnp.max(s2, axis=-1, keepdims=True))
            l1 = jnp.sum(p1, axis=-1, keepdims=True)
            l2 = jnp.sum(p2, axis=-1, keepdims=True)
            w = p1 - (lam * l1 / l2) * p2
            o = jnp.dot(w.astype(BF16), v_ref[0:n_keys, col(hh)], preferred_element_type=F32) * (1.0 / l1)
            o = o * lax.rsqrt(jnp.mean(o * o, axis=-1, keepdims=True) + EPS) * g_ref[...] * out_scale
            o_ref[:, col(hh)] = o.astype(o_ref.dtype)

    n_all = k_ref.shape[0]
    if ctx_tile:
        t = pl.program_id(2)

        @pl.when(t == 0)
        def _():
            attend(n_ctx)

        @pl.when(t > 0)
        def _():
            attend(n_all)
    else:
        attend(n_all)


def _attention(qk, vq, lam, g_subln, lam_init, n_ctx, with_ctx):
    b, l, _ = qk.shape
    hp = ATT_HEADS_PER_STEP
    ng = ATT_HEADS // hp
    tq = TOKEN_TILE
    off = 0 if with_ctx else n_ctx // tq
    n_out = l if with_ctx else l - n_ctx
    kern = functools.partial(_attn_kernel, n_ctx=n_ctx, ctx_tile=with_ctx, out_scale=1.0 - lam_init)
    return pl.pallas_call(
        kern,
        out_shape=jax.ShapeDtypeStruct((b, n_out, ATT_HEADS * HEAD_W), BF16),
        grid=(b, ng, n_out // tq),
        in_specs=[pl.BlockSpec(memory_space=pltpu.SMEM),
                  pl.BlockSpec((None, tq, hp * HEAD_W), lambda i, h, t: (i, t + off, h)),
                  pl.BlockSpec((None, l, hp * HEAD_W), lambda i, h, t: (i, 0, ng + h)),
                  pl.BlockSpec((None, l, hp * HEAD_W), lambda i, h, t: (i, 0, h)),
                  pl.BlockSpec((1, HEAD_W), lambda i, h, t: (0, 0))],
        out_specs=pl.BlockSpec((None, tq, hp * HEAD_W), lambda i, h, t: (i, t, h)),
        compiler_params=_cparams(("parallel", "arbitrary", "arbitrary")),
        name="diff_attention",
    )(lam.reshape(1), qk, qk, vq, g_subln.reshape(1, HEAD_W))


def _gla_tables(chunk):
    c = chunk
    levels = []
    b = c // 2
    while b >= 1:
        levels.append(b)
        b //= 2
    t = np.arange(c)
    u = np.arange(c)[None, :]

    def build(forward):
        mats, qrows, pmask = [], [], []
        tt = t[:, None]
        mats.append((u <= tt) if forward else (u >= tt))
        for b in levels:
            pair = t // (2 * b)
            late = (t % (2 * b)) >= b
            m = pair * 2 * b + b - 1
            if forward:
                is_q = late
                lo = np.where(is_q, m + 1, t + 1)[:, None]
                hi = np.where(is_q, t, m)[:, None]
            else:
                is_q = ~late
                lo = np.where(is_q, t, m + 1)[:, None]
                hi = np.where(is_q, m, t - 1)[:, None]
            if b < GLA_MIN_BCAST_BLOCK:
                mats.append((u >= lo) & (u <= hi))
            qrows.append(np.repeat(is_q[:, None], LANES, 1).astype(np.float32))
            pmask.append(((pair[:, None] == pair[None, :]) & is_q[:, None] & (~is_q)[None, :]).astype(np.float32))
        return np.concatenate(mats, 0).astype(np.float32), np.stack(qrows), np.stack(pmask)

    mf, qf, pf = build(True)
    mb, qb, pb = build(False)
    return levels, np.stack([mf, mb]), np.stack([qf, qb]), np.stack([pf, pb])


def _gla_kernel(q_ref, kf_ref, kb_ref, v_ref, gf_ref, gb_ref, gate_ref, gn_ref, m_ref, qrow_ref, pm_ref,
                o_ref, of_ref, ob_ref, sf_ref, sb_ref, *, levels, n_ctx_chunks, n_chunks):
    c = GLA_CHUNK
    u_n = GLA_GROUP
    n_levels = len(levels)
    sf_ref[...] = jnp.zeros_like(sf_ref)
    sb_ref[...] = jnp.zeros_like(sb_ref)
    tn = (((0,), (0,)), ((), ()))
    nt = (((1,), (1,)), ((), ()))

    def decays(direction, r0, g_ref):
        rows = [pl.ds(r0 + u * c, c) for u in range(u_n)]
        g = jnp.concatenate([g_ref[rw, :] for rw in rows], axis=1)
        g_hi = g.astype(BF16)
        r1 = g - g_hi.astype(F32)
        g_mid = r1.astype(BF16)
        g_lo = (r1 - g_mid.astype(F32)).astype(BF16)
        m = m_ref[direction]
        sums = (jnp.dot(m, g_hi, preferred_element_type=F32) + jnp.dot(m, g_mid, preferred_element_type=F32)
                + jnp.dot(m, g_lo, preferred_element_type=F32))
        run = sums[0:c, :]

        def to_row(blk, row):
            return -jnp.abs(blk - jnp.broadcast_to(run[row:row + 1, :], blk.shape))

        expo, n_direct = [], 0
        for b in levels:
            if b >= GLA_MIN_BCAST_BLOCK:
                parts = [to_row(run[p * 2 * b:(p + 1) * 2 * b, :], p * 2 * b + (b - 1 if direction == 0 else b))
                         for p in range(c // (2 * b))]
                expo.append(parts[0] if len(parts) == 1 else jnp.concatenate(parts, axis=0))
            else:
                n_direct += 1
                expo.append(sums[n_direct * c:(n_direct + 1) * c, :])
        expo.append(run)
        expo.append(to_row(run, c - 1 if direction == 0 else 0))
        return [jnp.exp(e) for e in expo]

    def group_pair(r0s):
        k_refs, g_refs, s_refs, out_refs = (kf_ref, kb_ref), (gf_ref, gb_ref), (sf_ref, sb_ref), (of_ref, ob_ref)
        dec = [decays(d, r0s[d], g_refs[d]) for d in (0, 1)]
        work = []
        for d in (0, 1):
            for u in range(u_n):
                rw = pl.ds(r0s[d] + u * c, c)
                q = q_ref[rw, :].astype(F32)
                k = k_refs[d][rw, :].astype(F32)
                v = v_ref[rw, :]
                du = [e[:, u * LANES:(u + 1) * LANES] for e in dec[d]]
                level_prods = []
                for lv in range(n_levels):
                    is_q = qrow_ref[d, lv] > 0.5
                    x = (jnp.where(is_q, q, k) * du[lv]).astype(BF16)
                    level_prods.append(lax.dot_general(x, x, nt, preferred_element_type=F32))
                q_dec, k_dec = du[n_levels], du[n_levels + 1]
                work.append(dict(d=d, rw=rw, v=v, prods=level_prods,
                                 diag=jnp.sum(q * k, axis=-1, keepdims=True) * v.astype(F32),
                                 q_state=(q * q_dec).astype(BF16), k_state=(k * k_dec).astype(BF16),
                                 total=q_dec[c - 1:c, :] if d == 0 else q_dec[0:1, :]))
        for w in work:
            scores = sum(pm_ref[w["d"], lv] * w["prods"][lv] for lv in range(n_levels))
            w["intra"] = w["diag"] + jnp.dot(scores.astype(BF16), w["v"], preferred_element_type=F32)
            w["vk"] = lax.dot_general(w["v"], w["k_state"], tn, preferred_element_type=F32)
        for d in (0, 1):
            state_t = s_refs[d][...]
            mine = [w for w in work if w["d"] == d]
            for w in (mine if d == 0 else mine[::-1]):
                out_refs[d][w["rw"], :] = w["intra"] + lax.dot_general(
                    w["q_state"], state_t.astype(BF16), nt, preferred_element_type=F32)
                state_t = state_t * w["total"] + w["vk"]
            s_refs[d][...] = state_t

    n_ctx_groups = n_ctx_chunks // u_n

    def body(i, carry):
        top = jnp.where(i < n_ctx_groups, n_ctx_chunks - i * u_n, n_chunks - (i - n_ctx_groups) * u_n)
        group_pair((pl.multiple_of(i * (u_n * c), u_n * c), pl.multiple_of((top - u_n) * c, u_n * c)))
        return carry

    lax.fori_loop(0, n_chunks // u_n, body, 0)
    o = of_ref[...] + ob_ref[...]
    o = o * lax.rsqrt(jnp.mean(o * o, axis=-1, keepdims=True) + EPS) * gn_ref[...]
    o_ref[...] = (o * gate_ref[...].astype(F32)).astype(o_ref.dtype)


def _recurrence(vq, kk, gg, ri, gates, g_rec, n_ctx):
    b, l, _ = vq.shape
    c = GLA_CHUNK
    levels, m_np, qrow_np, pm_np = _gla_tables(c)
    nh = REC_HEADS
    assert (n_ctx // c) % GLA_GROUP == 0 and ((l - n_ctx) // c) % GLA_GROUP == 0
    kern = functools.partial(_gla_kernel, levels=tuple(levels), n_ctx_chunks=n_ctx // c, n_chunks=l // c)
    head = lambda off: pl.BlockSpec((None, l, LANES), lambda i, h: (i, 0, h + off))
    const = lambda a: pl.BlockSpec(a.shape, lambda i, h: (0,) * a.ndim)
    m_all = jnp.asarray(m_np, BF16)
    qrow = jnp.asarray(qrow_np, F32)
    pm = jnp.asarray(pm_np, F32)
    return pl.pallas_call(
        kern,
        out_shape=jax.ShapeDtypeStruct((b, l, nh * LANES), BF16),
        grid=(b, nh),
        in_specs=[head(nh), head(0), head(nh), head(0), head(0), head(nh), head(0),
                  pl.BlockSpec((1, LANES), lambda i, h: (0, 0)),
                  const(m_all), const(qrow), const(pm)],
        out_specs=head(0),
        scratch_shapes=[pltpu.VMEM((l, LANES), F32), pltpu.VMEM((l, LANES), F32),
                        pltpu.VMEM((LANES, LANES), F32), pltpu.VMEM((LANES, LANES), F32)],
        compiler_params=_cparams(("parallel", "arbitrary")),
        name="hgrn2_recurrence",
    )(vq, kk, kk, ri, gg, gg, gates, g_rec.reshape(1, LANES), m_all, qrow, pm)


def _route(logits_t, bias):
    scores = _sigmoid(logits_t)
    biased = scores + bias
    row = lambda a, e: a[e:e + 1, :]
    epg = EXPERTS_PER_GROUP
    pairs = [(i, j) for i in range(epg) for j in range(i + 1, epg)]
    best_gs, grp = None, None
    for gi in range(N_GROUPS):
        gs = None
        for (i, j) in pairs:
            s = row(biased, gi * epg + i) + row(biased, gi * epg + j)
            gs = s if gs is None else jnp.maximum(gs, s)
        if best_gs is None:
            best_gs, grp = gs, jnp.zeros_like(gs, dtype=jnp.int32)
        else:
            better = gs > best_gs
            grp = jnp.where(better, gi, grp)
            best_gs = jnp.maximum(best_gs, gs)

    def in_group(a, j):
        out = row(a, j)
        for gi in range(1, N_GROUPS):
            out = jnp.where(grp == gi, row(a, gi * epg + j), out)
        return out

    u = [in_group(biased, j) for j in range(epg)]
    sc = [in_group(scores, j) for j in range(epg)]
    b1, i1 = u[0], jnp.zeros_like(grp)
    for j in range(1, epg):
        better = u[j] > b1
        i1 = jnp.where(better, j, i1)
        b1 = jnp.maximum(b1, u[j])
    i2 = jnp.where(i1 == 0, 1, 0)
    b2 = jnp.where(i1 == 0, u[1], u[0])
    for j in range(1, epg):
        better = (i1 != j) & (u[j] > b2)
        i2 = jnp.where(better, j, i2)
        b2 = jnp.where(better, u[j], b2)
    pick = lambda idx: sum(jnp.where(idx == j, sc[j], 0.0) for j in range(epg))
    w1, w2 = pick(i1), pick(i2)
    tot = w1 + w2
    w1, w2 = w1 / tot, w2 / tot
    la, lb2 = jnp.minimum(i1, i2), jnp.maximum(i1, i2)
    wa = jnp.where(i1 < i2, w1, w2)
    wb = jnp.where(i1 < i2, w2, w1)
    pair_idx = jnp.where(la == 0, lb2 - 1, jnp.where(la == 1, lb2 + 1, 5))
    cls = grp * 6 + pair_idx
    ea = grp * epg + la
    eb = grp * epg + lb2
    zeros = jnp.zeros_like(wa)
    return jnp.concatenate([cls.astype(F32), ea.astype(F32), eb.astype(F32), wa, wb, zeros, zeros, zeros], axis=0)


def _merge_kernel(*refs, two_sources):
    if two_sources:
        (ctx_ref, x_ref, a_ref, r_ref, ga_ref, gr_ref, gt_ref, sh_ref, sc_ref, g2_ref,
         wa_ref, wr_ref, wo_ref, wrt_ref, br_ref, xn_ref, h2_ref, route_ref) = refs
    else:
        (x_ref, a_ref, r_ref, ga_ref, gr_ref, gt_ref, sh_ref, sc_ref, g2_ref,
         wa_ref, wr_ref, wo_ref, wrt_ref, br_ref, xn_ref, h2_ref, route_ref) = refs

    def emit(xv):
        ua = jnp.dot(a_ref[...], wa_ref[...], preferred_element_type=F32)
        ur = jnp.dot(r_ref[...], wr_ref[...], preferred_element_type=F32)
        mixed = ga_ref[...].astype(F32) * ua + gr_ref[...].astype(F32) * ur
        y = jnp.dot(mixed.astype(BF16), wo_ref[...], preferred_element_type=F32)
        xn = xv + gt_ref[...] * y
        xn_ref[...] = xn
        h2 = _norm_mod(xn, g2_ref[...], sh_ref[...], sc_ref[...])
        h2_ref[...] = h2
        logits_t = lax.dot_general(wrt_ref[...], h2, (((1,), (1,)), ((), ())),
                                   precision=lax.Precision.HIGHEST, preferred_element_type=F32)
        route_ref[...] = _route(logits_t, br_ref[...])

    if two_sources:
        t = pl.program_id(1)

        @pl.when(t == 0)
        def _():
            emit(ctx_ref[...])

        @pl.when(t > 0)
        def _():
            emit(x_ref[...])
    else:
        emit(x_ref[...])


def _merge(res_inputs, a, r, gates, mods, g2, wa, wr, wo, wrt, br, n_ctx, with_ctx):
    b, l_a, d = a.shape
    l = gates.shape[1]
    tm = TOKEN_TILE
    off = 0 if with_ctx else n_ctx // tm
    nt = l // tm - off
    n_tok = b * nt * tm
    full = lambda shape: pl.BlockSpec(shape, lambda i, t: (0,) * len(shape))
    tile = lambda o: pl.BlockSpec((None, tm, d), lambda i, t: (i, t + o, 0))
    two_sources = len(res_inputs) == 2 and with_ctx
    if two_sources:
        res_specs = [pl.BlockSpec((None, tm, d), lambda i, t: (i, 0, 0)),
                     pl.BlockSpec((None, tm, d), lambda i, t: (i, jnp.maximum(t - 1, 0), 0))]
    elif len(res_inputs) == 2:
        res_inputs = res_inputs[1:]
        res_specs = [tile(0)]
    else:
        res_specs = [tile(off)]
    mspec = _mod_spec if with_ctx else _mod_spec_lat
    a_off = off - (l - l_a) // tm
    kern = functools.partial(_merge_kernel, two_sources=two_sources)
    return pl.pallas_call(
        kern,
        out_shape=[jax.ShapeDtypeStruct((b, nt * tm, d), F32),
                   jax.ShapeDtypeStruct((n_tok, d), F32),
                   jax.ShapeDtypeStruct((8, n_tok), F32)],
        grid=(b, nt),
        in_specs=res_specs + [
            tile(a_off), tile(off),
            pl.BlockSpec((None, tm, d), lambda i, t: (i, t + off, 1)),
            pl.BlockSpec((None, tm, d), lambda i, t: (i, t + off, 2)),
            mspec(2, d), mspec(3, d), mspec(4, d), full((1, d)),
            full((d, d)), full((d, d)), full((d, d)), full((N_EXPERTS, d)), full((N_EXPERTS, 1))],
        out_specs=[pl.BlockSpec((None, tm, d), lambda i, t: (i, t, 0)),
                   pl.BlockSpec((tm, d), lambda i, t: (i * nt + t, 0)),
                   pl.BlockSpec((8, tm), lambda i, t: (0, i * nt + t))],
        compiler_params=_cparams(("parallel", "arbitrary")),
        name="merge_route",
    )(*res_inputs, a, r, gates, gates, mods, mods, mods, g2.reshape(1, d), wa, wr, wo, wrt, br)


def _moe_kernel(tile_a_ref, tile_b_ref, tile_on_ref, tok_ref, h_hbm, wts_ref,
                wga_ref, wua_ref, wda_ref, wgb_ref, wub_ref, wdb_ref, out_hbm,
                xbuf, ybuf, sem_in, sem_out, *, n_tok):
    i = pl.program_id(0)
    tm = MOE_TILE
    base = i * tm

    def gather_copy(r):
        src = jnp.minimum(tok_ref[base + r], n_tok - 1)
        return pltpu.make_async_copy(h_hbm.at[pl.ds(src, 1), :], xbuf.at[pl.ds(r, 1), :], sem_in)

    def scatter_copy(r):
        return pltpu.make_async_copy(ybuf.at[pl.ds(r, 1), :], out_hbm.at[pl.ds(tok_ref[base + r], 1), :], sem_out)

    @pl.when(i == 0)
    def _():
        ybuf[...] = jnp.zeros_like(ybuf)
        pad_fill = pltpu.make_async_copy(ybuf, out_hbm.at[pl.ds(n_tok, tm), :], sem_out)
        pad_fill.start()
        pad_fill.wait()

    @pl.when(tile_on_ref[i] > 0)
    def _():
        def start_in(r, c):
            gather_copy(r).start()
            return c

        def wait_in(r, c):
            gather_copy(r).wait()
            return c

        lax.fori_loop(0, tm, start_in, 0)
        lax.fori_loop(0, tm, wait_in, 0)
        x = xbuf[...].astype(BF16)

        def expert(wg_ref, wu_ref, wd_ref):
            gate = jnp.dot(x, wg_ref[...], preferred_element_type=F32)
            up = jnp.dot(x, wu_ref[...], preferred_element_type=F32)
            hid = (gate * _sigmoid(gate) * up).astype(BF16)
            return jnp.dot(hid, wd_ref[...], preferred_element_type=F32)

        wts = wts_ref[...]
        ybuf[...] = (wts[:, 0:1] * expert(wga_ref, wua_ref, wda_ref)
                     + wts[:, 1:2] * expert(wgb_ref, wub_ref, wdb_ref))

        def start_out(r, c):
            scatter_copy(r).start()
            return c

        def wait_out(r, c):
            scatter_copy(r).wait()
            return c

        lax.fori_loop(0, tm, start_out, 0)
        lax.fori_loop(0, tm, wait_out, 0)


def _pair_class_experts():
    ea, eb = [], []
    for g in range(N_GROUPS):
        for i in range(EXPERTS_PER_GROUP):
            for j in range(i + 1, EXPERTS_PER_GROUP):
                ea.append(g * EXPERTS_PER_GROUP + i)
                eb.append(g * EXPERTS_PER_GROUP + j)
    return np.asarray(ea, np.int32), np.asarray(eb, np.int32)


def _moe_plan(route, n_tok):
    tm = MOE_TILE
    cls = route[0].astype(jnp.int32)
    onehot = (cls[:, None] == jnp.arange(N_PAIR_CLASSES, dtype=jnp.int32)[None, :]).astype(jnp.int32)
    csum = jnp.cumsum(onehot, axis=0)
    counts = csum[-1]
    padded = ((counts + tm - 1) // tm) * tm
    pends = jnp.cumsum(padded)
    pstarts = pends - padded
    pos = jnp.sum(onehot * (csum - 1 + pstarts[None, :]), axis=1)
    n_tiles = n_tok // tm + N_PAIR_CLASSES
    rows = jnp.arange(n_tiles * tm, dtype=jnp.int32)
    tok = (n_tok + rows % tm).at[pos].set(jnp.arange(n_tok, dtype=jnp.int32))
    wts = jnp.zeros((n_tiles * tm, 2), F32).at[pos].set(jnp.stack([route[3], route[4]], axis=1))
    tile_rows = rows[::tm]
    tile_on = (tile_rows < pends[-1]).astype(jnp.int32)
    tile_cls = jnp.sum((jnp.minimum(tile_rows, pends[-1] - 1)[:, None] >= pends[None, :]).astype(jnp.int32), axis=1)
    tile_cls = jnp.minimum(tile_cls, N_PAIR_CLASSES - 1)
    ea_np, eb_np = _pair_class_experts()
    ea = jnp.asarray(ea_np)[tile_cls]
    eb = jnp.asarray(eb_np)[tile_cls]
    return ea, eb, tile_on, tok, wts, n_tiles


def _moe(h2, route, wg, wu, wd):
    n_tok, d = h2.shape
    de = wg.shape[2]
    tm = MOE_TILE
    ea, eb, tile_on, tok, wts, n_tiles = _moe_plan(route, n_tok)
    wspec_a = lambda r, c: pl.BlockSpec((None, r, c), lambda i, ta, tb, on, tk: (ta[i], 0, 0))
    wspec_b = lambda r, c: pl.BlockSpec((None, r, c), lambda i, ta, tb, on, tk: (tb[i], 0, 0))
    grid_spec = pltpu.PrefetchScalarGridSpec(
        num_scalar_prefetch=4,
        grid=(n_tiles,),
        in_specs=[pl.BlockSpec(memory_space=pl.ANY),
                  pl.BlockSpec((tm, 2), lambda i, ta, tb, on, tk: (i, 0)),
                  wspec_a(d, de), wspec_a(d, de), wspec_a(de, d),
                  wspec_b(d, de), wspec_b(d, de), wspec_b(de, d)],
        out_specs=pl.BlockSpec(memory_space=pl.ANY),
        scratch_shapes=[pltpu.VMEM((tm, d), F32), pltpu.VMEM((tm, d), F32),
                        pltpu.SemaphoreType.DMA, pltpu.SemaphoreType.DMA],
    )
    return pl.pallas_call(
        functools.partial(_moe_kernel, n_tok=n_tok),
        out_shape=jax.ShapeDtypeStruct((n_tok + tm, d), F32),
        grid_spec=grid_spec,
        compiler_params=pltpu.CompilerParams(dimension_semantics=("arbitrary",), vmem_limit_bytes=VMEM_LIMIT,
                                             has_side_effects=True),
        name="moe_pairs",
    )(ea, eb, tile_on, tok, h2, wts, wg, wu, wd, wg, wu, wd)


def _post_kernel(x_ref, y_ref, gt_ref, g_ref, sh_ref, sc_ref, xo_ref, h_ref):
    xn = x_ref[...] + gt_ref[...] * y_ref[...]
    xo_ref[...] = xn
    h_ref[...] = _norm_mod(xn, g_ref[...], sh_ref[...], sc_ref[...]).astype(h_ref.dtype)


def _final_kernel(x_ref, y_ref, gt_ref, g_ref, o_ref):
    xn = x_ref[...] + gt_ref[...] * y_ref[...]
    o_ref[...] = xn * lax.rsqrt(jnp.mean(xn * xn, axis=-1, keepdims=True) + EPS) * g_ref[...]


def _post(x_all, moe_out, mods, mods_next, g_next):
    b, l, d = x_all.shape
    tm = TOKEN_TILE
    nt = l // tm
    tile = pl.BlockSpec((None, tm, d), lambda i, t: (i, t, 0))
    return pl.pallas_call(
        _post_kernel,
        out_shape=[jax.ShapeDtypeStruct((b, l, d), F32), jax.ShapeDtypeStruct((b, l, d), BF16)],
        grid=(b, nt),
        in_specs=[tile, pl.BlockSpec((tm, d), lambda i, t: (i * nt + t, 0)),
                  _mod_spec(5, d), pl.BlockSpec((1, d), lambda i, t: (0, 0)), _mod_spec(0, d), _mod_spec(1, d)],
        out_specs=[tile, tile],
        compiler_params=_cparams(("parallel", "arbitrary")),
        name="moe_residual_prenorm",
    )(x_all, moe_out, mods, g_next.reshape(1, d), mods_next, mods_next)


def _final(x_lat, moe_out, mods, g_final):
    b, s, d = x_lat.shape
    tm = TOKEN_TILE
    nt = s // tm
    tile = pl.BlockSpec((None, tm, d), lambda i, t: (i, t, 0))
    return pl.pallas_call(
        _final_kernel,
        out_shape=jax.ShapeDtypeStruct((b, s, d), F32),
        grid=(b, nt),
        in_specs=[tile, pl.BlockSpec((tm, d), lambda i, t: (i * nt + t, 0)),
                  _mod_spec_lat(5, d), pl.BlockSpec((1, d), lambda i, t: (0, 0))],
        out_specs=tile,
        compiler_params=_cparams(("parallel", "arbitrary")),
        name="moe_residual_final_norm",
    )(x_lat, moe_out, mods, g_final.reshape(1, d))


def _rope_tables(n_ctx, n_lat):
    t = np.arange(n_lat)
    n_freq = ATT_HEAD_DIM // 4
    inv = (ROPE_BASE ** (-np.arange(n_freq, dtype=np.float32) / n_freq)).astype(np.float32)
    ang_r = (t // GRID_W).astype(np.float32)[:, None] * inv
    ang_c = (t % GRID_W).astype(np.float32)[:, None] * inv
    ang = np.concatenate([ang_r, ang_r, ang_c, ang_c], axis=1)
    ang = np.concatenate([ang, ang], axis=1)
    sign = np.where((np.arange(HEAD_W) % 32) < 16, -1.0, 1.0).astype(np.float32)
    cos = np.concatenate([np.ones((n_ctx, HEAD_W), np.float32), np.cos(ang)], axis=0)
    sin = np.concatenate([np.zeros((n_ctx, HEAD_W), np.float32), np.sin(ang) * sign], axis=0)
    return jnp.asarray(cos, F32), jnp.asarray(sin, F32)


def kernel(x, c, ctx, c_ctx, w_mod, b_mod, g_norm1, g_norm2, w_in, lambda_q1, lambda_k1, lambda_q2, lambda_k2,
           g_subln, lb_logits, g_rec_norm, w_br_attn, w_br_rec, w_out, w_router, b_router, w_gate, w_up, w_down,
           g_final):
    b, s, d = x.shape
    n_ctx = ctx.shape[1]
    depth = w_in.shape[0]
    assert n_ctx == TOKEN_TILE and s % TOKEN_TILE == 0 and s % GRID_W == 0
    assert w_in.shape[2] == N_SPLITS * d

    bp = -(-(b + 1) // 8) * 8
    c_all = jnp.concatenate([c, c_ctx[None], jnp.zeros((bp - b - 1, d), F32)], axis=0)
    mod_all = _modulation(c_all, w_mod, b_mod)
    mods = [jnp.stack([jnp.broadcast_to(mod_all[l, b], (b, 6 * d)), mod_all[l, :b]], axis=1)
            .reshape(2 * b, 1, 6 * d) for l in range(depth)]

    lbs = jnp.cumsum(jax.nn.softmax(lb_logits.astype(F32), axis=0), axis=0)
    lbs = lbs - lbs[:1]
    cos_t, sin_t = _rope_tables(n_ctx, s)
    wrt = jnp.transpose(w_router).astype(F32)
    br = b_router.astype(F32).reshape(N_EXPERTS, 1)
    l_all = n_ctx + s
    full2 = lambda shape: pl.BlockSpec(shape, lambda j, i: (0, 0))

    x_all = None
    out = None
    h = _prenorm(ctx, x, g_norm1[0], mods[0])
    for l in range(depth):
        last = l == depth - 1
        lam_init = 0.8 - 0.6 * math.exp(-0.3 * l)
        lam = (jnp.exp(jnp.sum(lambda_q1[l].astype(F32) * lambda_k1[l].astype(F32)))
               - jnp.exp(jnp.sum(lambda_q2[l].astype(F32) * lambda_k2[l].astype(F32))) + lam_init)
        w_bf = w_in[l].astype(BF16)
        (qk,) = _project(_proj_rope_kernel, h, w_bf, 0, 2, [cos_t, sin_t],
                         [full2((l_all, HEAD_W)), full2((l_all, HEAD_W))], [BF16], "proj_rope")
        (vq,) = _project(_proj_plain_kernel, h, w_bf, 2, 2, [], [], [BF16], "proj_v_rq")
        kk, gg = _project(_proj_forget_kernel, h, w_bf, 4, 2, [lbs[l].reshape(1, d)],
                          [full2((1, d))], [BF16, F32], "proj_forget")
        (ri,) = _project(_proj_plain_kernel, h, w_bf, 6, 1, [], [], [BF16], "proj_ri")
        (gates,) = _project(_proj_gates_kernel, h, w_bf, 7, 3, [], [], [BF16], "proj_gates")

        a = _attention(qk, vq, lam, g_subln[l], lam_init, n_ctx, with_ctx=not last)
        r = _recurrence(vq, kk, gg, ri, gates, g_rec_norm[l], n_ctx)

        res = (ctx, x) if l == 0 else (x_all,)
        xn, h2, route = _merge(res, a, r, gates, mods[l], g_norm2[l], w_br_attn[l].astype(BF16),
                               w_br_rec[l].astype(BF16), w_out[l].astype(BF16), wrt, br, n_ctx,
                               with_ctx=not last)
        moe_out = _moe(h2, route, w_gate[l].astype(BF16), w_up[l].astype(BF16), w_down[l].astype(BF16))
        if last:
            out = _final(xn, moe_out, mods[l], g_final)
        else:
            x_all, h = _post(xn, moe_out, mods[l], mods[l + 1], g_norm1[l + 1])
    return out
```

```python
import functools
import math

import numpy as np
import jax
import jax.numpy as jnp
from jax import lax
from jax.experimental import pallas as pl
from jax.experimental.pallas import tpu as pltpu

EPS = 1e-6
GRID_W = 64
ROPE_BASE = 10000.0
ATT_HEADS = 8
ATT_HEAD_DIM = 64
HEAD_W = 2 * ATT_HEAD_DIM
REC_HEADS = 8
N_EXPERTS = 16
N_GROUPS = 4
EXPERTS_PER_GROUP = N_EXPERTS // N_GROUPS
N_PAIR_CLASSES = N_GROUPS * 6
N_SPLITS = 10

LANES = 128
SUBLANES = 8
TOKEN_TILE = 256
PROJ_ROWS = 768
ATT_HEADS_PER_STEP = 2
GLA_CHUNK = 64
GLA_GROUP = 4
GLA_MIN_BCAST_BLOCK = 4
MOE_TILE = 256
VMEM_LIMIT = 56 * 1024 * 1024

F32 = jnp.float32
BF16 = jnp.bfloat16


def _sigmoid(x):
    return 1.0 / (1.0 + jnp.exp(-x))


def _cparams(sem):
    return pltpu.CompilerParams(dimension_semantics=sem, vmem_limit_bytes=VMEM_LIMIT)


def _mod_kernel(c_ref, w_ref, b_ref, o_ref):
    c = c_ref[...]
    s = c * _sigmoid(c)
    o_ref[...] = jnp.dot(s, w_ref[...], precision=lax.Precision.HIGHEST,
                         preferred_element_type=F32) + b_ref[...]


def _modulation(c_all, w_mod, b_mod):
    depth, d, n = w_mod.shape
    bp = c_all.shape[0]
    tn = 1024
    return pl.pallas_call(
        _mod_kernel,
        out_shape=jax.ShapeDtypeStruct((depth, bp, n), F32),
        grid=(depth, n // tn),
        in_specs=[pl.BlockSpec((bp, d), lambda l, j: (0, 0)),
                  pl.BlockSpec((None, d, tn), lambda l, j: (l, 0, j)),
                  pl.BlockSpec((None, 1, tn), lambda l, j: (l, 0, j))],
        out_specs=pl.BlockSpec((None, bp, tn), lambda l, j: (l, 0, j)),
        compiler_params=_cparams(("arbitrary", "arbitrary")),
        name="adaln_mod",
    )(c_all, w_mod, b_mod.reshape(depth, 1, n))


def _norm_mod(x, g, sh, sc):
    y = x * lax.rsqrt(jnp.mean(x * x, axis=-1, keepdims=True) + EPS) * g
    return y * (1.0 + sc) + sh


def _mod_spec(piece, d):
    return pl.BlockSpec((None, 1, d), lambda b, t: (2 * b + jnp.minimum(t, 1), 0, piece))


def _mod_spec_lat(piece, d):
    return pl.BlockSpec((None, 1, d), lambda b, t: (2 * b + 1, 0, piece))


def _prenorm_kernel(ctx_ref, x_ref, g_ref, sh_ref, sc_ref, h_ref):
    t = pl.program_id(1)

    def emit(xv):
        h_ref[...] = _norm_mod(xv, g_ref[...], sh_ref[...], sc_ref[...]).astype(BF16)

    @pl.when(t == 0)
    def _():
        emit(ctx_ref[...])

    @pl.when(t > 0)
    def _():
        emit(x_ref[...])


def _prenorm(ctx, x, g, mods):
    b, c, d = ctx.shape
    s = x.shape[1]
    tm = TOKEN_TILE
    nt = (c + s) // tm
    return pl.pallas_call(
        _prenorm_kernel,
        out_shape=jax.ShapeDtypeStruct((b, c + s, d), BF16),
        grid=(b, nt),
        in_specs=[pl.BlockSpec((None, tm, d), lambda i, t: (i, 0, 0)),
                  pl.BlockSpec((None, tm, d), lambda i, t: (i, jnp.maximum(t - 1, 0), 0)),
                  pl.BlockSpec((1, d), lambda i, t: (0, 0)),
                  _mod_spec(0, d), _mod_spec(1, d)],
        out_specs=pl.BlockSpec((None, tm, d), lambda i, t: (i, t, 0)),
        compiler_params=_cparams(("parallel", "arbitrary")),
        name="prenorm",
    )(ctx, x, g.reshape(1, d), mods, mods)


def _swap16(xs, first16):
    return jnp.where(first16, pltpu.roll(xs, LANES - 16, axis=1), pltpu.roll(xs, 16, axis=1))


def _proj_chunks(h_ref, w_ref, epilogue):
    n = h_ref.shape[0] // PROJ_ROWS
    rows = [slice(r * PROJ_ROWS, (r + 1) * PROJ_ROWS) for r in range(n)]
    mm = lambda r: jnp.dot(h_ref[rows[r], :], w_ref[...], preferred_element_type=F32)
    acc = mm(0)
    for r in range(n):
        nxt = mm(r + 1) if r + 1 < n else None
        epilogue(rows[r], acc)
        acc = nxt


def _proj_rope_kernel(h_ref, w_ref, cos_ref, sin_ref, o_ref):
    j = pl.program_id(0)
    scale = jnp.where(j == 0, ATT_HEAD_DIM ** -0.5 * math.log2(math.e), 1.0).astype(F32)
    lane = lax.broadcasted_iota(jnp.int32, (PROJ_ROWS, LANES), 1)
    first16 = (lane % 32) < 16

    def epilogue(rows, acc):
        cos = cos_ref[rows, :] * scale
        sin = sin_ref[rows, :] * scale
        for k in range(acc.shape[1] // LANES):
            xs = acc[:, k * LANES:(k + 1) * LANES]
            o_ref[rows, k * LANES:(k + 1) * LANES] = (xs * cos + _swap16(xs, first16) * sin).astype(o_ref.dtype)

    _proj_chunks(h_ref, w_ref, epilogue)


def _proj_plain_kernel(h_ref, w_ref, o_ref):
    def epilogue(rows, acc):
        o_ref[rows, :] = acc.astype(o_ref.dtype)

    _proj_chunks(h_ref, w_ref, epilogue)


def _proj_forget_kernel(h_ref, w_ref, lb_ref, k_ref, g_ref):
    lb = lb_ref[...]
    log_lb = jnp.log(lb)
    log_1m_lb = jnp.log1p(-lb)

    def epilogue(rows, z):
        log_sig = jnp.minimum(z, 0.0) - jnp.log(1.0 + jnp.exp(-jnp.abs(z)))
        bterm = log_1m_lb + log_sig
        mx = jnp.maximum(log_lb, bterm)
        mn = jnp.minimum(log_lb, bterm)
        log_f = mx + jnp.log(1.0 + jnp.exp(mn - mx))
        g_ref[rows, :] = log_f
        k_ref[rows, :] = (1.0 - jnp.exp(log_f)).astype(k_ref.dtype)

    _proj_chunks(h_ref, w_ref, epilogue)


def _proj_gates_kernel(h_ref, w_ref, o_ref):
    j = pl.program_id(0)

    def epilogue(rows, z):
        sg = _sigmoid(z)
        o_ref[rows, :] = jnp.where(j == 0, z * sg, sg).astype(o_ref.dtype)

    _proj_chunks(h_ref, w_ref, epilogue)


def _project(kernel, h, w_bf, first_split, n_splits, extra_inputs, extra_specs, out_dtypes, name):
    b, l, d = h.shape
    wcol = w_bf.shape[1] // N_SPLITS
    out_shape = [jax.ShapeDtypeStruct((b, l, n_splits * wcol), dt) for dt in out_dtypes]
    out_specs = [pl.BlockSpec((None, l, wcol), lambda j, i: (i, 0, j)) for _ in out_dtypes]
    res = pl.pallas_call(
        kernel,
        out_shape=out_shape,
        grid=(n_splits, b),
        in_specs=[pl.BlockSpec((None, l, d), lambda j, i: (i, 0, 0)),
                  pl.BlockSpec((d, wcol), lambda j, i: (0, first_split + j))] + extra_specs,
        out_specs=out_specs,
        compiler_params=_cparams(("arbitrary", "arbitrary")),
        name=name,
    )(h, w_bf, *extra_inputs)
    return res


def _attn_kernel(lam_ref, q_ref, k_ref, v_ref, g_ref, o_ref, *, n_ctx, ctx_tile, out_scale):
    lam = lam_ref[0]
    nt = (((1,), (1,)), ((), ()))
    heads = range(ATT_HEADS_PER_STEP)
    col = lambda hh: slice(hh * HEAD_W, (hh + 1) * HEAD_W)

    def attend(n_keys):
        scores = []
        for hh in heads:
            q = q_ref[:, col(hh)]
            k = k_ref[0:n_keys, col(hh)]
            lane = lax.broadcasted_iota(jnp.int32, q.shape, 1)
            zero = jnp.zeros_like(q)
            q1 = jnp.where(lane < ATT_HEAD_DIM, q, zero)
            q2 = jnp.where(lane < ATT_HEAD_DIM, zero, q)
            scores.append((lax.dot_general(q1, k, nt, preferred_element_type=F32),
                           lax.dot_general(q2, k, nt, preferred_element_type=F32)))
        for hh in heads:
            s1, s2 = scores[hh]
            p1 = jnp.exp2(s1 - jnp.max(s1, axis=-1, keepdims=True))
            p2 = jnp.exp2(s2 - jnp.max(s2, axis=-1, keepdims=True))
            l1 = jnp.sum(p1, axis=-1, keepdims=True)
            l2 = jnp.sum(p2, axis=-1, keepdims=True)
            w = p1 - (lam * l1 / l2) * p2
            o = jnp.dot(w.astype(BF16), v_ref[0:n_keys, col(hh)], preferred_element_type=F32) * (1.0 / l1)
            o = o * lax.rsqrt(jnp.mean(o * o, axis=-1, keepdims=True) + EPS) * g_ref[...] * out_scale
            o_ref[:, col(hh)] = o.astype(o_ref.dtype)

    n_all = k_ref.shape[0]
    if ctx_tile:
        t = pl.program_id(2)

        @pl.when(t == 0)
        def _():
            attend(n_ctx)

        @pl.when(t > 0)
        def _():
            attend(n_all)
    else:
        attend(n_all)


def _attention(qk, vq, lam, g_subln, lam_init, n_ctx, with_ctx):
    b, l, _ = qk.shape
    hp = ATT_HEADS_PER_STEP
    ng = ATT_HEADS // hp
    tq = TOKEN_TILE
    off = 0 if with_ctx else n_ctx // tq
    n_out = l if with_ctx else l - n_ctx
    kern = functools.partial(_attn_kernel, n_ctx=n_ctx, ctx_tile=with_ctx, out_scale=1.0 - lam_init)
    return pl.pallas_call(
        kern,
        out_shape=jax.ShapeDtypeStruct((b, n_out, ATT_HEADS * HEAD_W), BF16),
        grid=(b, ng, n_out // tq),
        in_specs=[pl.BlockSpec(memory_space=pltpu.SMEM),
                  pl.BlockSpec((None, tq, hp * HEAD_W), lambda i, h, t: (i, t + off, h)),
                  pl.BlockSpec((None, l, hp * HEAD_W), lambda i, h, t: (i, 0, ng + h)),
                  pl.BlockSpec((None, l, hp * HEAD_W), lambda i, h, t: (i, 0, h)),
                  pl.BlockSpec((1, HEAD_W), lambda i, h, t: (0, 0))],
        out_specs=pl.BlockSpec((None, tq, hp * HEAD_W), lambda i, h, t: (i, t, h)),
        compiler_params=_cparams(("parallel", "arbitrary", "arbitrary")),
        name="diff_attention",
    )(lam.reshape(1), qk, qk, vq, g_subln.reshape(1, HEAD_W))


def _gla_tables(chunk):
    c = chunk
    levels = []
    b = c // 2
    while b >= 1:
        levels.append(b)
        b //= 2
    t = np.arange(c)
    u = np.arange(c)[None, :]

    def build(forward):
        mats, qrows, pmask = [], [], []
        tt = t[:, None]
        mats.append((u <= tt) if forward else (u >= tt))
        for b in levels:
            pair = t // (2 * b)
            late = (t % (2 * b)) >= b
            m = pair * 2 * b + b - 1
            if forward:
                is_q = late
                lo = np.where(is_q, m + 1, t + 1)[:, None]
                hi = np.where(is_q, t, m)[:, None]
            else:
                is_q = ~late
                lo = np.where(is_q, t, m + 1)[:, None]
                hi = np.where(is_q, m, t - 1)[:, None]
            if b < GLA_MIN_BCAST_BLOCK:
                mats.append((u >= lo) & (u <= hi))
            qrows.append(np.repeat(is_q[:, None], LANES, 1).astype(np.float32))
            pmask.append(((pair[:, None] == pair[None, :]) & is_q[:, None] & (~is_q)[None, :]).astype(np.float32))
        return np.concatenate(mats, 0).astype(np.float32), np.stack(qrows), np.stack(pmask)

    mf, qf, pf = build(True)
    mb, qb, pb = build(False)
    return levels, np.stack([mf, mb]), np.stack([qf, qb]), np.stack([pf, pb])


def _gla_kernel(q_ref, kf_ref, kb_ref, v_ref, gf_ref, gb_ref, gate_ref, gn_ref, m_ref, qrow_ref, pm_ref,
                o_ref, of_ref, ob_ref, sf_ref, sb_ref, *, levels, n_ctx_chunks, n_chunks):
    c = GLA_CHUNK
    u_n = GLA_GROUP
    n_levels = len(levels)
    sf_ref[...] = jnp.zeros_like(sf_ref)
    sb_ref[...] = jnp.zeros_like(sb_ref)
    tn = (((0,), (0,)), ((), ()))
    nt = (((1,), (1,)), ((), ()))

    def decays(direction, r0, g_ref):
        rows = [pl.ds(r0 + u * c, c) for u in range(u_n)]
        g = jnp.concatenate([g_ref[rw, :] for rw in rows], axis=1)
        g_hi = g.astype(BF16)
        r1 = g - g_hi.astype(F32)
        g_mid = r1.astype(BF16)
        g_lo = (r1 - g_mid.astype(F32)).astype(BF16)
        m = m_ref[direction]
        sums = (jnp.dot(m, g_hi, preferred_element_type=F32) + jnp.dot(m, g_mid, preferred_element_type=F32)
                + jnp.dot(m, g_lo, preferred_element_type=F32))
        run = sums[0:c, :]

        def to_row(blk, row):
            return -jnp.abs(blk - jnp.broadcast_to(run[row:row + 1, :], blk.shape))

        expo, n_direct = [], 0
        for b in levels:
            if b >= GLA_MIN_BCAST_BLOCK:
                parts = [to_row(run[p * 2 * b:(p + 1) * 2 * b, :], p * 2 * b + (b - 1 if direction == 0 else b))
                         for p in range(c // (2 * b))]
                expo.append(parts[0] if len(parts) == 1 else jnp.concatenate(parts, axis=0))
            else:
                n_direct += 1
                expo.append(sums[n_direct * c:(n_direct + 1) * c, :])
        expo.append(run)
        expo.append(to_row(run, c - 1 if direction == 0 else 0))
        return [jnp.exp(e) for e in expo]

    def group_pair(r0s):
        k_refs, g_refs, s_refs, out_refs = (kf_ref, kb_ref), (gf_ref, gb_ref), (sf_ref, sb_ref), (of_ref, ob_ref)
        dec = [decays(d, r0s[d], g_refs[d]) for d in (0, 1)]
        work = []
        for d in (0, 1):
            for u in range(u_n):
                rw = pl.ds(r0s[d] + u * c, c)
                q = q_ref[rw, :].astype(F32)
                k = k_refs[d][rw, :].astype(F32)
                v = v_ref[rw, :]
                du = [e[:, u * LANES:(u + 1) * LANES] for e in dec[d]]
                level_prods = []
                for lv in range(n_levels):
                    is_q = qrow_ref[d, lv] > 0.5
                    x = (jnp.where(is_q, q, k) * du[lv]).astype(BF16)
                    level_prods.append(lax.dot_general(x, x, nt, preferred_element_type=F32))
                q_dec, k_dec = du[n_levels], du[n_levels + 1]
                work.append(dict(d=d, rw=rw, v=v, prods=level_prods,
                                 diag=jnp.sum(q * k, axis=-1, keepdims=True) * v.astype(F32),
                                 q_state=(q * q_dec).astype(BF16), k_state=(k * k_dec).astype(BF16),
                                 total=q_dec[c - 1:c, :] if d == 0 else q_dec[0:1, :]))
        for w in work:
            scores = sum(pm_ref[w["d"], lv] * w["prods"][lv] for lv in range(n_levels))
            w["intra"] = w["diag"] + jnp.dot(scores.astype(BF16), w["v"], preferred_element_type=F32)
            w["vk"] = lax.dot_general(w["v"], w["k_state"], tn, preferred_element_type=F32)
        for d in (0, 1):
            state_t = s_refs[d][...]
            mine = [w for w in work if w["d"] == d]
            for w in (mine if d == 0 else mine[::-1]):
                out_refs[d][w["rw"], :] = w["intra"] + lax.dot_general(
                    w["q_state"], state_t.astype(BF16), nt, preferred_element_type=F32)
                state_t = state_t * w["total"] + w["vk"]
            s_refs[d][...] = state_t

    n_ctx_groups = n_ctx_chunks // u_n

    def body(i, carry):
        top = jnp.where(i < n_ctx_groups, n_ctx_chunks - i * u_n, n_chunks - (i - n_ctx_groups) * u_n)
        group_pair((pl.multiple_of(i * (u_n * c), u_n * c), pl.multiple_of((top - u_n) * c, u_n * c)))
        return carry

    lax.fori_loop(0, n_chunks // u_n, body, 0)
    o = of_ref[...] + ob_ref[...]
    o = o * lax.rsqrt(jnp.mean(o * o, axis=-1, keepdims=True) + EPS) * gn_ref[...]
    o_ref[...] = (o * gate_ref[...].astype(F32)).astype(o_ref.dtype)


def _recurrence(vq, kk, gg, ri, gates, g_rec, n_ctx):
    b, l, _ = vq.shape
    c = GLA_CHUNK
    levels, m_np, qrow_np, pm_np = _gla_tables(c)
    nh = REC_HEADS
    assert (n_ctx // c) % GLA_GROUP == 0 and ((l - n_ctx) // c) % GLA_GROUP == 0
    kern = functools.partial(_gla_kernel, levels=tuple(levels), n_ctx_chunks=n_ctx // c, n_chunks=l // c)
    head = lambda off: pl.BlockSpec((None, l, LANES), lambda i, h: (i, 0, h + off))
    const = lambda a: pl.BlockSpec(a.shape, lambda i, h: (0,) * a.ndim)
    m_all = jnp.asarray(m_np, BF16)
    qrow = jnp.asarray(qrow_np, F32)
    pm = jnp.asarray(pm_np, F32)
    return pl.pallas_call(
        kern,
        out_shape=jax.ShapeDtypeStruct((b, l, nh * LANES), BF16),
        grid=(b, nh),
        in_specs=[head(nh), head(0), head(nh), head(0), head(0), head(nh), head(0),
                  pl.BlockSpec((1, LANES), lambda i, h: (0, 0)),
                  const(m_all), const(qrow), const(pm)],
        out_specs=head(0),
        scratch_shapes=[pltpu.VMEM((l, LANES), F32), pltpu.VMEM((l, LANES), F32),
                        pltpu.VMEM((LANES, LANES), F32), pltpu.VMEM((LANES, LANES), F32)],
        compiler_params=_cparams(("parallel", "arbitrary")),
        name="hgrn2_recurrence",
    )(vq, kk, kk, ri, gg, gg, gates, g_rec.reshape(1, LANES), m_all, qrow, pm)


def _route(logits, bias):
    scores = [_sigmoid(lg) for lg in logits]
    biased = [sc + bs for sc, bs in zip(scores, bias)]
    row = lambda a, e: a[e]
    epg = EXPERTS_PER_GROUP
    pairs = [(i, j) for i in range(epg) for j in range(i + 1, epg)]
    best_gs, grp = None, None
    for gi in range(N_GROUPS):
        gs = None
        for (i, j) in pairs:
            s = row(biased, gi * epg + i) + row(biased, gi * epg + j)
            gs = s if gs is None else jnp.maximum(gs, s)
        if best_gs is None:
            best_gs, grp = gs, jnp.zeros_like(gs, dtype=jnp.int32)
        else:
            better = gs > best_gs
            grp = jnp.where(better, gi, grp)
            best_gs = jnp.maximum(best_gs, gs)

    def in_group(a, j):
        out = row(a, j)
        for gi in range(1, N_GROUPS):
            out = jnp.where(grp == gi, row(a, gi * epg + j), out)
        return out

    u = [in_group(biased, j) for j in range(epg)]
    sc = [in_group(scores, j) for j in range(epg)]
    b1, i1 = u[0], jnp.zeros_like(grp)
    for j in range(1, epg):
        better = u[j] > b1
        i1 = jnp.where(better, j, i1)
        b1 = jnp.maximum(b1, u[j])
    i2 = jnp.where(i1 == 0, 1, 0)
    b2 = jnp.where(i1 == 0, u[1], u[0])
    for j in range(1, epg):
        better = (i1 != j) & (u[j] > b2)
        i2 = jnp.where(better, j, i2)
        b2 = jnp.where(better, u[j], b2)
    pick = lambda idx: sum(jnp.where(idx == j, sc[j], 0.0) for j in range(epg))
    w1, w2 = pick(i1), pick(i2)
    tot = w1 + w2
    w1, w2 = w1 / tot, w2 / tot
    la, lb2 = jnp.minimum(i1, i2), jnp.maximum(i1, i2)
    wa = jnp.where(i1 < i2, w1, w2)
    wb = jnp.where(i1 < i2, w2, w1)
    pair_idx = jnp.where(la == 0, lb2 - 1, jnp.where(la == 1, lb2 + 1, 5))
    cls = grp * 6 + pair_idx
    ea = grp * epg + la
    eb = grp * epg + lb2
    return [cls.astype(F32), ea.astype(F32), eb.astype(F32), wa, wb]


def _route_kernel(bias_ref, logits_ref, route_ref):
    rows = _route([logits_ref[e] for e in range(N_EXPERTS)], [bias_ref[e] for e in range(N_EXPERTS)])
    for k, r in enumerate(rows):
        route_ref[k] = r
    for k in range(len(rows), route_ref.shape[0]):
        route_ref[k] = jnp.zeros(route_ref.shape[1:], F32)


def _routing(logits_t, b_router):
    ne, n_tok = logits_t.shape
    w = n_tok // SUBLANES
    out = pl.pallas_call(
        _route_kernel,
        out_shape=jax.ShapeDtypeStruct((8, SUBLANES, w), F32),
        grid=(1,),
        in_specs=[pl.BlockSpec(memory_space=pltpu.SMEM),
                  pl.BlockSpec((ne, SUBLANES, w), lambda i: (0, 0, 0))],
        out_specs=pl.BlockSpec((8, SUBLANES, w), lambda i: (0, 0, 0)),
        compiler_params=_cparams(("arbitrary",)),
        name="route_top2",
    )(b_router.astype(F32), logits_t.reshape(ne, SUBLANES, w))
    return out.reshape(8, n_tok)


def _merge_kernel(*refs, two_sources):
    if two_sources:
        (ctx_ref, x_ref, a_ref, r_ref, ga_ref, gr_ref, gt_ref, sh_ref, sc_ref, g2_ref,
         wa_ref, wr_ref, wo_ref, wrt_ref, xn_ref, h2_ref, logit_ref) = refs
    else:
        (x_ref, a_ref, r_ref, ga_ref, gr_ref, gt_ref, sh_ref, sc_ref, g2_ref,
         wa_ref, wr_ref, wo_ref, wrt_ref, xn_ref, h2_ref, logit_ref) = refs

    def emit(xv):
        ua = jnp.dot(a_ref[...], wa_ref[...], preferred_element_type=F32)
        ur = jnp.dot(r_ref[...], wr_ref[...], preferred_element_type=F32)
        mixed = ga_ref[...].astype(F32) * ua + gr_ref[...].astype(F32) * ur
        y = jnp.dot(mixed.astype(BF16), wo_ref[...], preferred_element_type=F32)
        xn = xv + gt_ref[...] * y
        xn_ref[...] = xn
        h2 = _norm_mod(xn, g2_ref[...], sh_ref[...], sc_ref[...])
        h2_ref[...] = h2
        nt = (((1,), (1,)), ((), ()))
        h_hi = h2.astype(BF16)
        h_lo = (h2 - h_hi.astype(F32)).astype(BF16)
        w_hi = wrt_ref[0]
        w_lo = wrt_ref[1]
        logit_ref[...] = (lax.dot_general(w_hi, h_hi, nt, preferred_element_type=F32)
                          + lax.dot_general(w_lo, h_hi, nt, preferred_element_type=F32)
                          + lax.dot_general(w_hi, h_lo, nt, preferred_element_type=F32))

    if two_sources:
        t = pl.program_id(1)

        @pl.when(t == 0)
        def _():
            emit(ctx_ref[...])

        @pl.when(t > 0)
        def _():
            emit(x_ref[...])
    else:
        emit(x_ref[...])


def _merge(res_inputs, a, r, gates, mods, g2, wa, wr, wo, wrt, n_ctx, with_ctx):
    b, l_a, d = a.shape
    l = gates.shape[1]
    tm = TOKEN_TILE
    off = 0 if with_ctx else n_ctx // tm
    nt = l // tm - off
    n_tok = b * nt * tm
    full = lambda shape: pl.BlockSpec(shape, lambda i, t: (0,) * len(shape))
    tile = lambda o: pl.BlockSpec((None, tm, d), lambda i, t: (i, t + o, 0))
    two_sources = len(res_inputs) == 2 and with_ctx
    if two_sources:
        res_specs = [pl.BlockSpec((None, tm, d), lambda i, t: (i, 0, 0)),
                     pl.BlockSpec((None, tm, d), lambda i, t: (i, jnp.maximum(t - 1, 0), 0))]
    elif len(res_inputs) == 2:
        res_inputs = res_inputs[1:]
        res_specs = [tile(0)]
    else:
        res_specs = [tile(off)]
    mspec = _mod_spec if with_ctx else _mod_spec_lat
    a_off = off - (l - l_a) // tm
    kern = functools.partial(_merge_kernel, two_sources=two_sources)
    return pl.pallas_call(
        kern,
        out_shape=[jax.ShapeDtypeStruct((b, nt * tm, d), F32),
                   jax.ShapeDtypeStruct((n_tok, d), F32),
                   jax.ShapeDtypeStruct((N_EXPERTS, n_tok), F32)],
        grid=(b, nt),
        in_specs=res_specs + [
            tile(a_off), tile(off),
            pl.BlockSpec((None, tm, d), lambda i, t: (i, t + off, 1)),
            pl.BlockSpec((None, tm, d), lambda i, t: (i, t + off, 2)),
            mspec(2, d), mspec(3, d), mspec(4, d), full((1, d)),
            full((d, d)), full((d, d)), full((d, d)), full((2, N_EXPERTS, d))],
        out_specs=[pl.BlockSpec((None, tm, d), lambda i, t: (i, t, 0)),
                   pl.BlockSpec((tm, d), lambda i, t: (i * nt + t, 0)),
                   pl.BlockSpec((N_EXPERTS, tm), lambda i, t: (0, i * nt + t))],
        compiler_params=_cparams(("parallel", "arbitrary")),
        name="merge_route",
    )(*res_inputs, a, r, gates, gates, mods, mods, mods, g2.reshape(1, d), wa, wr, wo, wrt)


def _moe_kernel(tile_a_ref, tile_b_ref, n_act_ref, src_ref, dst_ref, h_hbm, wts_ref,
                wga_ref, wua_ref, wda_ref, wgb_ref, wub_ref, wdb_ref, out_hbm,
                xbuf, ybuf, sem_in, sem_out, *, n_tok):
    i = pl.program_id(0)
    tm = MOE_TILE
    n_act = n_act_ref[0]
    slot = i % 2

    def row_copies(tile, sl, gather, start):
        base = tile * tm

        def body(j, c):
            r0 = j * SUBLANES
            for u in range(SUBLANES):
                if gather:
                    cp = pltpu.make_async_copy(h_hbm.at[pl.ds(src_ref[base + r0 + u], 1), :],
                                               xbuf.at[sl, j, pl.ds(u, 1), :], sem_in.at[sl])
                else:
                    cp = pltpu.make_async_copy(ybuf.at[sl, j, pl.ds(u, 1), :],
                                               out_hbm.at[pl.ds(dst_ref[base + r0 + u], 1), :], sem_out.at[sl])
                if start:
                    cp.start()
                else:
                    cp.wait()
            return c

        lax.fori_loop(0, tm // SUBLANES, body, 0)

    @pl.when(i == 0)
    def _():
        ybuf[0] = jnp.zeros(ybuf.shape[1:], ybuf.dtype)

        def pad_fill(j, start):
            cp = pltpu.make_async_copy(ybuf.at[0, j], out_hbm.at[pl.ds(n_tok + j * SUBLANES, SUBLANES), :],
                                       sem_out.at[0])
            cp.start() if start else cp.wait()

        pl.loop(0, tm // SUBLANES)(functools.partial(pad_fill, start=True))
        pl.loop(0, tm // SUBLANES)(functools.partial(pad_fill, start=False))

        @pl.when(n_act > 0)
        def _():
            row_copies(0, 0, gather=True, start=True)

    @pl.when(i < n_act)
    def _():
        row_copies(i, slot, gather=True, start=False)

        @pl.when(i + 1 < n_act)
        def _():
            row_copies(i + 1, 1 - slot, gather=True, start=True)

        x = xbuf[slot].reshape(tm, xbuf.shape[-1]).astype(BF16)

        def expert(wg_ref, wu_ref, wd_ref):
            gate = jnp.dot(x, wg_ref[...], preferred_element_type=F32)
            up = jnp.dot(x, wu_ref[...], preferred_element_type=F32)
            hid = (gate * _sigmoid(gate) * up).astype(BF16)
            return jnp.dot(hid, wd_ref[...], preferred_element_type=F32)

        wts = wts_ref[...]
        y = wts[:, 0:1] * expert(wga_ref, wua_ref, wda_ref) + wts[:, 1:2] * expert(wgb_ref, wub_ref, wdb_ref)

        @pl.when(i >= 2)
        def _():
            row_copies(i - 2, slot, gather=False, start=False)

        ybuf[slot] = y.reshape(ybuf.shape[1:])
        row_copies(i, slot, gather=False, start=True)

        @pl.when(i == n_act - 1)
        def _():
            row_copies(i, slot, gather=False, start=False)

            @pl.when(i >= 1)
            def _():
                row_copies(i - 1, 1 - slot, gather=False, start=False)


def _pair_class_experts():
    ea, eb = [], []
    for g in range(N_GROUPS):
        for i in range(EXPERTS_PER_GROUP):
            for j in range(i + 1, EXPERTS_PER_GROUP):
                ea.append(g * EXPERTS_PER_GROUP + i)
                eb.append(g * EXPERTS_PER_GROUP + j)
    return np.asarray(ea, np.int32), np.asarray(eb, np.int32)


def _moe_plan(route, n_tok):
    tm = MOE_TILE
    cls = route[0].astype(jnp.int32)
    onehot = (cls[:, None] == jnp.arange(N_PAIR_CLASSES, dtype=jnp.int32)[None, :]).astype(jnp.int32)
    csum = jnp.cumsum(onehot, axis=0)
    counts = csum[-1]
    padded = ((counts + tm - 1) // tm) * tm
    pends = jnp.cumsum(padded)
    pstarts = pends - padded
    pos = jnp.sum(onehot * (csum - 1 + pstarts[None, :]), axis=1)
    n_tiles = n_tok // tm + N_PAIR_CLASSES
    rows = jnp.arange(n_tiles * tm, dtype=jnp.int32)
    tok = (n_tok + rows % tm).at[pos].set(jnp.arange(n_tok, dtype=jnp.int32))
    wts = jnp.zeros((n_tiles * tm, 2), F32).at[pos].set(jnp.stack([route[3], route[4]], axis=1))
    tile_rows = rows[::tm]
    n_act = (pends[-1] // tm).reshape(1)
    tile_cls = jnp.sum((jnp.minimum(tile_rows, pends[-1] - 1)[:, None] >= pends[None, :]).astype(jnp.int32), axis=1)
    tile_cls = jnp.minimum(tile_cls, N_PAIR_CLASSES - 1)
    ea_np, eb_np = _pair_class_experts()
    ea = jnp.asarray(ea_np)[tile_cls]
    eb = jnp.asarray(eb_np)[tile_cls]
    return ea, eb, n_act, jnp.minimum(tok, n_tok - 1), tok, wts, n_tiles


def _moe(h2, route, wg, wu, wd):
    n_tok, d = h2.shape
    de = wg.shape[2]
    tm = MOE_TILE
    ea, eb, n_act, src, dst, wts, n_tiles = _moe_plan(route, n_tok)
    wspec_a = lambda r, c: pl.BlockSpec((None, r, c), lambda i, ta, tb, na, sr, ds: (ta[i], 0, 0))
    wspec_b = lambda r, c: pl.BlockSpec((None, r, c), lambda i, ta, tb, na, sr, ds: (tb[i], 0, 0))
    grid_spec = pltpu.PrefetchScalarGridSpec(
        num_scalar_prefetch=5,
        grid=(n_tiles,),
        in_specs=[pl.BlockSpec(memory_space=pl.ANY),
                  pl.BlockSpec((tm, 2), lambda i, ta, tb, na, sr, ds: (i, 0)),
                  wspec_a(d, de), wspec_a(d, de), wspec_a(de, d),
                  wspec_b(d, de), wspec_b(d, de), wspec_b(de, d)],
        out_specs=pl.BlockSpec(memory_space=pl.ANY),
        scratch_shapes=[pltpu.VMEM((2, tm // SUBLANES, SUBLANES, d), F32),
                        pltpu.VMEM((2, tm // SUBLANES, SUBLANES, d), F32),
                        pltpu.SemaphoreType.DMA((2,)), pltpu.SemaphoreType.DMA((2,))],
    )
    return pl.pallas_call(
        functools.partial(_moe_kernel, n_tok=n_tok),
        out_shape=jax.ShapeDtypeStruct((n_tok + tm, d), F32),
        grid_spec=grid_spec,
        compiler_params=pltpu.CompilerParams(dimension_semantics=("arbitrary",), vmem_limit_bytes=VMEM_LIMIT,
                                             has_side_effects=True),
        name="moe_pairs",
    )(ea, eb, n_act, src, dst, h2, wts, wg, wu, wd, wg, wu, wd)


def _post_kernel(x_ref, y_ref, gt_ref, g_ref, sh_ref, sc_ref, xo_ref, h_ref):
    xn = x_ref[...] + gt_ref[...] * y_ref[...]
    xo_ref[...] = xn
    h_ref[...] = _norm_mod(xn, g_ref[...], sh_ref[...], sc_ref[...]).astype(h_ref.dtype)


def _final_kernel(x_ref, y_ref, gt_ref, g_ref, o_ref):
    xn = x_ref[...] + gt_ref[...] * y_ref[...]
    o_ref[...] = xn * lax.rsqrt(jnp.mean(xn * xn, axis=-1, keepdims=True) + EPS) * g_ref[...]


def _post(x_all, moe_out, mods, mods_next, g_next):
    b, l, d = x_all.shape
    tm = TOKEN_TILE
    nt = l // tm
    tile = pl.BlockSpec((None, tm, d), lambda i, t: (i, t, 0))
    return pl.pallas_call(
        _post_kernel,
        out_shape=[jax.ShapeDtypeStruct((b, l, d), F32), jax.ShapeDtypeStruct((b, l, d), BF16)],
        grid=(b, nt),
        in_specs=[tile, pl.BlockSpec((tm, d), lambda i, t: (i * nt + t, 0)),
                  _mod_spec(5, d), pl.BlockSpec((1, d), lambda i, t: (0, 0)), _mod_spec(0, d), _mod_spec(1, d)],
        out_specs=[tile, tile],
        compiler_params=_cparams(("parallel", "arbitrary")),
        name="moe_residual_prenorm",
    )(x_all, moe_out, mods, g_next.reshape(1, d), mods_next, mods_next)


def _final(x_lat, moe_out, mods, g_final):
    b, s, d = x_lat.shape
    tm = TOKEN_TILE
    nt = s // tm
    tile = pl.BlockSpec((None, tm, d), lambda i, t: (i, t, 0))
    return pl.pallas_call(
        _final_kernel,
        out_shape=jax.ShapeDtypeStruct((b, s, d), F32),
        grid=(b, nt),
        in_specs=[tile, pl.BlockSpec((tm, d), lambda i, t: (i * nt + t, 0)),
                  _mod_spec_lat(5, d), pl.BlockSpec((1, d), lambda i, t: (0, 0))],
        out_specs=tile,
        compiler_params=_cparams(("parallel", "arbitrary")),
        name="moe_residual_final_norm",
    )(x_lat, moe_out, mods, g_final.reshape(1, d))


def _rope_tables(n_ctx, n_lat):
    t = np.arange(n_lat)
    n_freq = ATT_HEAD_DIM // 4
    inv = (ROPE_BASE ** (-np.arange(n_freq, dtype=np.float32) / n_freq)).astype(np.float32)
    ang_r = (t // GRID_W).astype(np.float32)[:, None] * inv
    ang_c = (t % GRID_W).astype(np.float32)[:, None] * inv
    ang = np.concatenate([ang_r, ang_r, ang_c, ang_c], axis=1)
    ang = np.concatenate([ang, ang], axis=1)
    sign = np.where((np.arange(HEAD_W) % 32) < 16, -1.0, 1.0).astype(np.float32)
    cos = np.concatenate([np.ones((n_ctx, HEAD_W), np.float32), np.cos(ang)], axis=0)
    sin = np.concatenate([np.zeros((n_ctx, HEAD_W), np.float32), np.sin(ang) * sign], axis=0)
    return jnp.asarray(cos, F32), jnp.asarray(sin, F32)


def kernel(x, c, ctx, c_ctx, w_mod, b_mod, g_norm1, g_norm2, w_in, lambda_q1, lambda_k1, lambda_q2, lambda_k2,
           g_subln, lb_logits, g_rec_norm, w_br_attn, w_br_rec, w_out, w_router, b_router, w_gate, w_up, w_down,
           g_final):
    b, s, d = x.shape
    n_ctx = ctx.shape[1]
    depth = w_in.shape[0]
    assert n_ctx == TOKEN_TILE and s % TOKEN_TILE == 0 and s % GRID_W == 0
    assert w_in.shape[2] == N_SPLITS * d

    bp = -(-(b + 1) // 8) * 8
    c_all = jnp.concatenate([c, c_ctx[None], jnp.zeros((bp - b - 1, d), F32)], axis=0)
    mod_all = _modulation(c_all, w_mod, b_mod)
    mods = [jnp.stack([jnp.broadcast_to(mod_all[l, b], (b, 6 * d)), mod_all[l, :b]], axis=1)
            .reshape(2 * b, 1, 6 * d) for l in range(depth)]

    lbs = jnp.cumsum(jax.nn.softmax(lb_logits.astype(F32), axis=0), axis=0)
    lbs = lbs - lbs[:1]
    cos_t, sin_t = _rope_tables(n_ctx, s)
    wrt_f = jnp.transpose(w_router).astype(F32)
    wrt_hi = wrt_f.astype(BF16)
    wrt = jnp.stack([wrt_hi, (wrt_f - wrt_hi.astype(F32)).astype(BF16)])
    l_all = n_ctx + s
    full2 = lambda shape: pl.BlockSpec(shape, lambda j, i: (0, 0))

    x_all = None
    out = None
    h = _prenorm(ctx, x, g_norm1[0], mods[0])
    for l in range(depth):
        last = l == depth - 1
        lam_init = 0.8 - 0.6 * math.exp(-0.3 * l)
        lam = (jnp.exp(jnp.sum(lambda_q1[l].astype(F32) * lambda_k1[l].astype(F32)))
               - jnp.exp(jnp.sum(lambda_q2[l].astype(F32) * lambda_k2[l].astype(F32))) + lam_init)
        w_bf = w_in[l].astype(BF16)
        (qk,) = _project(_proj_rope_kernel, h, w_bf, 0, 2, [cos_t, sin_t],
                         [full2((l_all, HEAD_W)), full2((l_all, HEAD_W))], [BF16], "proj_rope")
        (vq,) = _project(_proj_plain_kernel, h, w_bf, 2, 2, [], [], [BF16], "proj_v_rq")
        kk, gg = _project(_proj_forget_kernel, h, w_bf, 4, 2, [lbs[l].reshape(1, d)],
                          [full2((1, d))], [BF16, F32], "proj_forget")
        (ri,) = _project(_proj_plain_kernel, h, w_bf, 6, 1, [], [], [BF16], "proj_ri")
        (gates,) = _project(_proj_gates_kernel, h, w_bf, 7, 3, [], [], [BF16], "proj_gates")

        a = _attention(qk, vq, lam, g_subln[l], lam_init, n_ctx, with_ctx=not last)
        r = _recurrence(vq, kk, gg, ri, gates, g_rec_norm[l], n_ctx)

        res = (ctx, x) if l == 0 else (x_all,)
        xn, h2, logits_t = _merge(res, a, r, gates, mods[l], g_norm2[l], w_br_attn[l].astype(BF16),
                                  w_br_rec[l].astype(BF16), w_out[l].astype(BF16), wrt, n_ctx, with_ctx=not last)
        route = _routing(logits_t, b_router)
        moe_out = _moe(h2, route, w_gate[l].astype(BF16), w_up[l].astype(BF16), w_down[l].astype(BF16))
        if last:
            out = _final(xn, moe_out, mods[l], g_final)
        else:
            x_all, h = _post(xn, moe_out, mods[l], mods[l + 1], g_norm1[l + 1])
    return out
```

```python
import functools
import math

import numpy as np
import jax
import jax.numpy as jnp
from jax import lax
from jax.experimental import pallas as pl
from jax.experimental.pallas import tpu as pltpu

EPS = 1e-6
GRID_W = 64
ROPE_BASE = 10000.0
ATT_HEADS = 8
ATT_HEAD_DIM = 64
HEAD_W = 2 * ATT_HEAD_DIM
REC_HEADS = 8
N_EXPERTS = 16
N_GROUPS = 4
EXPERTS_PER_GROUP = N_EXPERTS // N_GROUPS
N_PAIR_CLASSES = N_GROUPS * 6
N_SPLITS = 10

LANES = 128
SUBLANES = 8
TOKEN_TILE = 256
PROJ_ROWS = 768
ATT_HEADS_PER_STEP = 4
GLA_CHUNK = 64
GLA_GROUP = 4
GLA_MIN_BCAST_BLOCK = 4
MOE_TILE = 256
VMEM_LIMIT = 56 * 1024 * 1024

F32 = jnp.float32
BF16 = jnp.bfloat16


def _sigmoid(x):
    return 1.0 / (1.0 + jnp.exp(-x))


def _cparams(sem):
    return pltpu.CompilerParams(dimension_semantics=sem, vmem_limit_bytes=VMEM_LIMIT)


def _mod_kernel(c_ref, w_ref, b_ref, o_ref):
    c = c_ref[...]
    s = c * _sigmoid(c)
    o_ref[...] = jnp.dot(s, w_ref[...], precision=lax.Precision.HIGHEST,
                         preferred_element_type=F32) + b_ref[...]


def _modulation(c_all, w_mod, b_mod):
    depth, d, n = w_mod.shape
    bp = c_all.shape[0]
    tn = 1024
    return pl.pallas_call(
        _mod_kernel,
        out_shape=jax.ShapeDtypeStruct((depth, bp, n), F32),
        grid=(depth, n // tn),
        in_specs=[pl.BlockSpec((bp, d), lambda l, j: (0, 0)),
                  pl.BlockSpec((None, d, tn), lambda l, j: (l, 0, j)),
                  pl.BlockSpec((None, 1, tn), lambda l, j: (l, 0, j))],
        out_specs=pl.BlockSpec((None, bp, tn), lambda l, j: (l, 0, j)),
        compiler_params=_cparams(("arbitrary", "arbitrary")),
        name="adaln_mod",
    )(c_all, w_mod, b_mod.reshape(depth, 1, n))


def _norm_mod(x, g, sh, sc):
    y = x * lax.rsqrt(jnp.mean(x * x, axis=-1, keepdims=True) + EPS) * g
    return y * (1.0 + sc) + sh


def _mod_spec(piece, d):
    return pl.BlockSpec((None, 1, d), lambda b, t: (2 * b + jnp.minimum(t, 1), 0, piece))


def _mod_spec_lat(piece, d):
    return pl.BlockSpec((None, 1, d), lambda b, t: (2 * b + 1, 0, piece))


def _prenorm_kernel(ctx_ref, x_ref, g_ref, sh_ref, sc_ref, h_ref):
    t = pl.program_id(1)

    def emit(xv):
        h_ref[...] = _norm_mod(xv, g_ref[...], sh_ref[...], sc_ref[...]).astype(BF16)

    @pl.when(t == 0)
    def _():
        emit(ctx_ref[...])

    @pl.when(t > 0)
    def _():
        emit(x_ref[...])


def _prenorm(ctx, x, g, mods):
    b, c, d = ctx.shape
    s = x.shape[1]
    tm = TOKEN_TILE
    nt = (c + s) // tm
    return pl.pallas_call(
        _prenorm_kernel,
        out_shape=jax.ShapeDtypeStruct((b, c + s, d), BF16),
        grid=(b, nt),
        in_specs=[pl.BlockSpec((None, tm, d), lambda i, t: (i, 0, 0)),
                  pl.BlockSpec((None, tm, d), lambda i, t: (i, jnp.maximum(t - 1, 0), 0)),
                  pl.BlockSpec((1, d), lambda i, t: (0, 0)),
                  _mod_spec(0, d), _mod_spec(1, d)],
        out_specs=pl.BlockSpec((None, tm, d), lambda i, t: (i, t, 0)),
        compiler_params=_cparams(("parallel", "arbitrary")),
        name="prenorm",
    )(ctx, x, g.reshape(1, d), mods, mods)


def _swap16(xs, first16):
    return jnp.where(first16, pltpu.roll(xs, LANES - 16, axis=1), pltpu.roll(xs, 16, axis=1))


def _proj_chunks(h_ref, w_ref, epilogue):
    n = h_ref.shape[0] // PROJ_ROWS
    rows = [slice(r * PROJ_ROWS, (r + 1) * PROJ_ROWS) for r in range(n)]
    mm = lambda r: jnp.dot(h_ref[rows[r], :], w_ref[...], preferred_element_type=F32)
    acc = mm(0)
    for r in range(n):
        nxt = mm(r + 1) if r + 1 < n else None
        epilogue(rows[r], acc)
        acc = nxt


def _proj_rope_kernel(h_ref, w_ref, cos_ref, sin_ref, o_ref):
    j = pl.program_id(0)
    scale = jnp.where(j == 0, ATT_HEAD_DIM ** -0.5 * math.log2(math.e), 1.0).astype(F32)
    lane = lax.broadcasted_iota(jnp.int32, (PROJ_ROWS, LANES), 1)
    first16 = (lane % 32) < 16

    def epilogue(rows, acc):
        cos = cos_ref[rows, :] * scale
        sin = sin_ref[rows, :] * scale
        for k in range(acc.shape[1] // LANES):
            xs = acc[:, k * LANES:(k + 1) * LANES]
            o_ref[rows, k * LANES:(k + 1) * LANES] = (xs * cos + _swap16(xs, first16) * sin).astype(o_ref.dtype)

    _proj_chunks(h_ref, w_ref, epilogue)


def _proj_plain_kernel(h_ref, w_ref, o_ref):
    def epilogue(rows, acc):
        o_ref[rows, :] = acc.astype(o_ref.dtype)

    _proj_chunks(h_ref, w_ref, epilogue)


def _proj_forget_kernel(h_ref, w_ref, lb_ref, k_ref, g_ref):
    lb = lb_ref[...]
    log_lb = jnp.log(lb)
    log_1m_lb = jnp.log1p(-lb)

    def epilogue(rows, z):
        log_sig = jnp.minimum(z, 0.0) - jnp.log(1.0 + jnp.exp(-jnp.abs(z)))
        bterm = log_1m_lb + log_sig
        mx = jnp.maximum(log_lb, bterm)
        mn = jnp.minimum(log_lb, bterm)
        log_f = mx + jnp.log(1.0 + jnp.exp(mn - mx))
        g_ref[rows, :] = log_f
        k_ref[rows, :] = (1.0 - jnp.exp(log_f)).astype(k_ref.dtype)

    _proj_chunks(h_ref, w_ref, epilogue)


def _proj_gates_kernel(h_ref, w_ref, o_ref):
    j = pl.program_id(0)

    def epilogue(rows, z):
        sg = _sigmoid(z)
        o_ref[rows, :] = jnp.where(j == 0, z * sg, sg).astype(o_ref.dtype)

    _proj_chunks(h_ref, w_ref, epilogue)


def _project(kernel, h, w_bf, first_split, n_splits, extra_inputs, extra_specs, out_dtypes, name):
    b, l, d = h.shape
    wcol = w_bf.shape[1] // N_SPLITS
    out_shape = [jax.ShapeDtypeStruct((b, l, n_splits * wcol), dt) for dt in out_dtypes]
    out_specs = [pl.BlockSpec((None, l, wcol), lambda j, i: (i, 0, j)) for _ in out_dtypes]
    res = pl.pallas_call(
        kernel,
        out_shape=out_shape,
        grid=(n_splits, b),
        in_specs=[pl.BlockSpec((None, l, d), lambda j, i: (i, 0, 0)),
                  pl.BlockSpec((d, wcol), lambda j, i: (0, first_split + j))] + extra_specs,
        out_specs=out_specs,
        compiler_params=_cparams(("arbitrary", "arbitrary")),
        name=name,
    )(h, w_bf, *extra_inputs)
    return res


def _attn_kernel(lam_ref, q_ref, k_ref, v_ref, g_ref, o_ref, *, n_ctx, ctx_tile, out_scale):
    lam = lam_ref[0]
    nt = (((1,), (1,)), ((), ()))
    heads = range(ATT_HEADS_PER_STEP)
    col = lambda hh: slice(hh * HEAD_W, (hh + 1) * HEAD_W)

    def attend(n_keys):
        scores = []
        for hh in heads:
            q = q_ref[:, col(hh)]
            k = k_ref[0:n_keys, col(hh)]
            lane = lax.broadcasted_iota(jnp.int32, q.shape, 1)
            zero = jnp.zeros_like(q)
            q1 = jnp.where(lane < ATT_HEAD_DIM, q, zero)
            q2 = jnp.where(lane < ATT_HEAD_DIM, zero, q)
            scores.append((lax.dot_general(q1, k, nt, preferred_element_type=F32),
                           lax.dot_general(q2, k, nt, preferred_element_type=F32)))
        for hh in heads:
            v = v_ref[0:n_keys, col(hh)]
            v1 = jnp.concatenate([v, jnp.ones_like(v)], axis=1)
            outs = []
            for sm in scores[hh]:
                p = jnp.exp2(sm - jnp.max(sm, axis=-1, keepdims=True))
                outs.append(jnp.dot(p.astype(BF16), v1, preferred_element_type=F32))
            (o1, o2) = outs
            l1, l2 = o1[:, HEAD_W:HEAD_W + 1], o2[:, HEAD_W:HEAD_W + 1]
            o = o1[:, :HEAD_W] * (1.0 / l1) - o2[:, :HEAD_W] * (lam / l2)
            o = o * lax.rsqrt(jnp.mean(o * o, axis=-1, keepdims=True) + EPS) * g_ref[...] * out_scale
            o_ref[:, col(hh)] = o.astype(o_ref.dtype)

    n_all = k_ref.shape[0]
    if ctx_tile:
        t = pl.program_id(2)

        @pl.when(t == 0)
        def _():
            attend(n_ctx)

        @pl.when(t > 0)
        def _():
            attend(n_all)
    else:
        attend(n_all)


def _attention(qk, vq, lam, g_subln, lam_init, n_ctx, with_ctx):
    b, l, _ = qk.shape
    hp = ATT_HEADS_PER_STEP
    ng = ATT_HEADS // hp
    tq = TOKEN_TILE
    off = 0 if with_ctx else n_ctx // tq
    n_out = l if with_ctx else l - n_ctx
    kern = functools.partial(_attn_kernel, n_ctx=n_ctx, ctx_tile=with_ctx, out_scale=1.0 - lam_init)
    return pl.pallas_call(
        kern,
        out_shape=jax.ShapeDtypeStruct((b, n_out, ATT_HEADS * HEAD_W), BF16),
        grid=(b, ng, n_out // tq),
        in_specs=[pl.BlockSpec(memory_space=pltpu.SMEM),
                  pl.BlockSpec((None, tq, hp * HEAD_W), lambda i, h, t: (i, t + off, h)),
                  pl.BlockSpec((None, l, hp * HEAD_W), lambda i, h, t: (i, 0, ng + h)),
                  pl.BlockSpec((None, l, hp * HEAD_W), lambda i, h, t: (i, 0, h)),
                  pl.BlockSpec((1, HEAD_W), lambda i, h, t: (0, 0))],
        out_specs=pl.BlockSpec((None, tq, hp * HEAD_W), lambda i, h, t: (i, t, h)),
        compiler_params=_cparams(("parallel", "arbitrary", "arbitrary")),
        name="diff_attention",
    )(lam.reshape(1), qk, qk, vq, g_subln.reshape(1, HEAD_W))


def _gla_tables(chunk):
    c = chunk
    levels = []
    b = c // 2
    while b >= 1:
        levels.append(b)
        b //= 2
    t = np.arange(c)
    u = np.arange(c)[None, :]

    def build(forward):
        mats, qrows, pmask = [], [], []
        tt = t[:, None]
        mats.append((u <= tt) if forward else (u >= tt))
        for b in levels:
            pair = t // (2 * b)
            late = (t % (2 * b)) >= b
            m = pair * 2 * b + b - 1
            if forward:
                is_q = late
                lo = np.where(is_q, m + 1, t + 1)[:, None]
                hi = np.where(is_q, t, m)[:, None]
            else:
                is_q = ~late
                lo = np.where(is_q, t, m + 1)[:, None]
                hi = np.where(is_q, m, t - 1)[:, None]
            if b < GLA_MIN_BCAST_BLOCK:
                mats.append((u >= lo) & (u <= hi))
            qrows.append(np.repeat(is_q[:, None], LANES, 1).astype(np.float32))
            pmask.append(((pair[:, None] == pair[None, :]) & is_q[:, None] & (~is_q)[None, :]).astype(np.float32))
        return np.concatenate(mats, 0).astype(np.float32), np.stack(qrows), np.stack(pmask)

    mf, qf, pf = build(True)
    mb, qb, pb = build(False)
    return levels, np.stack([mf, mb]), np.stack([qf, qb]), np.stack([pf, pb])


def _gla_kernel(q_ref, kf_ref, kb_ref, v_ref, gf_ref, gb_ref, gate_ref, gn_ref, m_ref, qrow_ref, pm_ref,
                o_ref, of_ref, ob_ref, sf_ref, sb_ref, *, levels, n_ctx_chunks, n_chunks):
    c = GLA_CHUNK
    u_n = GLA_GROUP
    n_levels = len(levels)
    sf_ref[...] = jnp.zeros_like(sf_ref)
    sb_ref[...] = jnp.zeros_like(sb_ref)
    tn = (((0,), (0,)), ((), ()))
    nt = (((1,), (1,)), ((), ()))

    def decays(direction, r0, g_ref):
        rows = [pl.ds(r0 + u * c, c) for u in range(u_n)]
        g = jnp.concatenate([g_ref[rw, :] for rw in rows], axis=1)
        g_hi = g.astype(BF16)
        r1 = g - g_hi.astype(F32)
        g_mid = r1.astype(BF16)
        g_lo = (r1 - g_mid.astype(F32)).astype(BF16)
        m = m_ref[direction]
        sums = (jnp.dot(m, g_hi, preferred_element_type=F32) + jnp.dot(m, g_mid, preferred_element_type=F32)
                + jnp.dot(m, g_lo, preferred_element_type=F32))
        run = sums[0:c, :]

        def to_row(blk, row):
            return -jnp.abs(blk - jnp.broadcast_to(run[row:row + 1, :], blk.shape))

        expo, n_direct = [], 0
        for b in levels:
            if b >= GLA_MIN_BCAST_BLOCK:
                parts = [to_row(run[p * 2 * b:(p + 1) * 2 * b, :], p * 2 * b + (b - 1 if direction == 0 else b))
                         for p in range(c // (2 * b))]
                expo.append(parts[0] if len(parts) == 1 else jnp.concatenate(parts, axis=0))
            else:
                n_direct += 1
                expo.append(sums[n_direct * c:(n_direct + 1) * c, :])
        expo.append(run)
        expo.append(to_row(run, c - 1 if direction == 0 else 0))
        return [jnp.exp(e) for e in expo]

    def group_pair(r0s):
        k_refs, g_refs, s_refs, out_refs = (kf_ref, kb_ref), (gf_ref, gb_ref), (sf_ref, sb_ref), (of_ref, ob_ref)
        dec = [decays(d, r0s[d], g_refs[d]) for d in (0, 1)]
        work = []
        for d in (0, 1):
            for u in range(u_n):
                rw = pl.ds(r0s[d] + u * c, c)
                q = q_ref[rw, :].astype(F32)
                k = k_refs[d][rw, :].astype(F32)
                v = v_ref[rw, :]
                du = [e[:, u * LANES:(u + 1) * LANES] for e in dec[d]]
                level_prods = []
                for lv in range(n_levels):
                    is_q = qrow_ref[d, lv] > 0.5
                    x = (jnp.where(is_q, q, k) * du[lv]).astype(BF16)
                    level_prods.append(lax.dot_general(x, x, nt, preferred_element_type=F32))
                q_dec, k_dec = du[n_levels], du[n_levels + 1]
                work.append(dict(d=d, rw=rw, v=v, prods=level_prods,
                                 diag=jnp.sum(q * k, axis=-1, keepdims=True) * v.astype(F32),
                                 q_state=(q * q_dec).astype(BF16), k_state=(k * k_dec).astype(BF16),
                                 total=q_dec[c - 1:c, :] if d == 0 else q_dec[0:1, :]))
        for w in work:
            scores = sum(pm_ref[w["d"], lv] * w["prods"][lv] for lv in range(n_levels))
            w["intra"] = w["diag"] + jnp.dot(scores.astype(BF16), w["v"], preferred_element_type=F32)
            w["vk"] = lax.dot_general(w["v"], w["k_state"], tn, preferred_element_type=F32)
        for d in (0, 1):
            state_t = s_refs[d][...]
            mine = [w for w in work if w["d"] == d]
            for w in (mine if d == 0 else mine[::-1]):
                out_refs[d][w["rw"], :] = w["intra"] + lax.dot_general(
                    w["q_state"], state_t.astype(BF16), nt, preferred_element_type=F32)
                state_t = state_t * w["total"] + w["vk"]
            s_refs[d][...] = state_t

    n_ctx_groups = n_ctx_chunks // u_n

    def body(i, carry):
        top = jnp.where(i < n_ctx_groups, n_ctx_chunks - i * u_n, n_chunks - (i - n_ctx_groups) * u_n)
        group_pair((pl.multiple_of(i * (u_n * c), u_n * c), pl.multiple_of((top - u_n) * c, u_n * c)))
        return carry

    lax.fori_loop(0, n_chunks // u_n, body, 0)
    o = of_ref[...] + ob_ref[...]
    o = o * lax.rsqrt(jnp.mean(o * o, axis=-1, keepdims=True) + EPS) * gn_ref[...]
    o_ref[...] = (o * gate_ref[...].astype(F32)).astype(o_ref.dtype)


def _recurrence(vq, kk, gg, ri, gates, g_rec, n_ctx):
    b, l, _ = vq.shape
    c = GLA_CHUNK
    levels, m_np, qrow_np, pm_np = _gla_tables(c)
    nh = REC_HEADS
    assert (n_ctx // c) % GLA_GROUP == 0 and ((l - n_ctx) // c) % GLA_GROUP == 0
    kern = functools.partial(_gla_kernel, levels=tuple(levels), n_ctx_chunks=n_ctx // c, n_chunks=l // c)
    head = lambda off: pl.BlockSpec((None, l, LANES), lambda i, h: (i, 0, h + off))
    const = lambda a: pl.BlockSpec(a.shape, lambda i, h: (0,) * a.ndim)
    m_all = jnp.asarray(m_np, BF16)
    qrow = jnp.asarray(qrow_np, F32)
    pm = jnp.asarray(pm_np, F32)
    return pl.pallas_call(
        kern,
        out_shape=jax.ShapeDtypeStruct((b, l, nh * LANES), BF16),
        grid=(b, nh),
        in_specs=[head(nh), head(0), head(nh), head(0), head(0), head(nh), head(0),
                  pl.BlockSpec((1, LANES), lambda i, h: (0, 0)),
                  const(m_all), const(qrow), const(pm)],
        out_specs=head(0),
        scratch_shapes=[pltpu.VMEM((l, LANES), F32), pltpu.VMEM((l, LANES), F32),
                        pltpu.VMEM((LANES, LANES), F32), pltpu.VMEM((LANES, LANES), F32)],
        compiler_params=_cparams(("parallel", "arbitrary")),
        name="hgrn2_recurrence",
    )(vq, kk, kk, ri, gg, gg, gates, g_rec.reshape(1, LANES), m_all, qrow, pm)


def _route(logits, bias):
    scores = [_sigmoid(lg) for lg in logits]
    biased = [sc + bs for sc, bs in zip(scores, bias)]
    row = lambda a, e: a[e]
    epg = EXPERTS_PER_GROUP
    pairs = [(i, j) for i in range(epg) for j in range(i + 1, epg)]
    best_gs, grp = None, None
    for gi in range(N_GROUPS):
        gs = None
        for (i, j) in pairs:
            s = row(biased, gi * epg + i) + row(biased, gi * epg + j)
            gs = s if gs is None else jnp.maximum(gs, s)
        if best_gs is None:
            best_gs, grp = gs, jnp.zeros_like(gs, dtype=jnp.int32)
        else:
            better = gs > best_gs
            grp = jnp.where(better, gi, grp)
            best_gs = jnp.maximum(best_gs, gs)

    def in_group(a, j):
        out = row(a, j)
        for gi in range(1, N_GROUPS):
            out = jnp.where(grp == gi, row(a, gi * epg + j), out)
        return out

    u = [in_group(biased, j) for j in range(epg)]
    sc = [in_group(scores, j) for j in range(epg)]
    b1, i1 = u[0], jnp.zeros_like(grp)
    for j in range(1, epg):
        better = u[j] > b1
        i1 = jnp.where(better, j, i1)
        b1 = jnp.maximum(b1, u[j])
    i2 = jnp.where(i1 == 0, 1, 0)
    b2 = jnp.where(i1 == 0, u[1], u[0])
    for j in range(1, epg):
        better = (i1 != j) & (u[j] > b2)
        i2 = jnp.where(better, j, i2)
        b2 = jnp.where(better, u[j], b2)
    pick = lambda idx: sum(jnp.where(idx == j, sc[j], 0.0) for j in range(epg))
    w1, w2 = pick(i1), pick(i2)
    tot = w1 + w2
    w1, w2 = w1 / tot, w2 / tot
    la, lb2 = jnp.minimum(i1, i2), jnp.maximum(i1, i2)
    wa = jnp.where(i1 < i2, w1, w2)
    wb = jnp.where(i1 < i2, w2, w1)
    pair_idx = jnp.where(la == 0, lb2 - 1, jnp.where(la == 1, lb2 + 1, 5))
    cls = grp * 6 + pair_idx
    ea = grp * epg + la
    eb = grp * epg + lb2
    return [cls.astype(F32), ea.astype(F32), eb.astype(F32), wa, wb]


def _route_kernel(bias_ref, logits_ref, route_ref):
    rows = _route([logits_ref[e] for e in range(N_EXPERTS)], [bias_ref[e] for e in range(N_EXPERTS)])
    for k, r in enumerate(rows):
        route_ref[k] = r
    for k in range(len(rows), route_ref.shape[0]):
        route_ref[k] = jnp.zeros(route_ref.shape[1:], F32)


def _routing(logits_t, b_router):
    ne, n_tok = logits_t.shape
    w = n_tok // SUBLANES
    out = pl.pallas_call(
        _route_kernel,
        out_shape=jax.ShapeDtypeStruct((8, SUBLANES, w), F32),
        grid=(1,),
        in_specs=[pl.BlockSpec(memory_space=pltpu.SMEM),
                  pl.BlockSpec((ne, SUBLANES, w), lambda i: (0, 0, 0))],
        out_specs=pl.BlockSpec((8, SUBLANES, w), lambda i: (0, 0, 0)),
        compiler_params=_cparams(("arbitrary",)),
        name="route_top2",
    )(b_router.astype(F32), logits_t.reshape(ne, SUBLANES, w))
    return out.reshape(8, n_tok)


def _merge_kernel(*refs, two_sources):
    if two_sources:
        (ctx_ref, x_ref, a_ref, r_ref, ga_ref, gr_ref, gt_ref, sh_ref, sc_ref, g2_ref,
         wa_ref, wr_ref, wo_ref, wrt_ref, xn_ref, h2_ref, logit_ref) = refs
    else:
        (x_ref, a_ref, r_ref, ga_ref, gr_ref, gt_ref, sh_ref, sc_ref, g2_ref,
         wa_ref, wr_ref, wo_ref, wrt_ref, xn_ref, h2_ref, logit_ref) = refs

    def emit(xv):
        ua = jnp.dot(a_ref[...], wa_ref[...], preferred_element_type=F32)
        ur = jnp.dot(r_ref[...], wr_ref[...], preferred_element_type=F32)
        mixed = ga_ref[...].astype(F32) * ua + gr_ref[...].astype(F32) * ur
        y = jnp.dot(mixed.astype(BF16), wo_ref[...], preferred_element_type=F32)
        xn = xv + gt_ref[...] * y
        xn_ref[...] = xn
        h2 = _norm_mod(xn, g2_ref[...], sh_ref[...], sc_ref[...])
        h2_ref[...] = h2
        nt = (((1,), (1,)), ((), ()))
        h_hi = h2.astype(BF16)
        h_lo = (h2 - h_hi.astype(F32)).astype(BF16)
        w_hi = wrt_ref[0]
        w_lo = wrt_ref[1]
        logit_ref[...] = (lax.dot_general(w_hi, h_hi, nt, preferred_element_type=F32)
                          + lax.dot_general(w_lo, h_hi, nt, preferred_element_type=F32)
                          + lax.dot_general(w_hi, h_lo, nt, preferred_element_type=F32))

    if two_sources:
        t = pl.program_id(1)

        @pl.when(t == 0)
        def _():
            emit(ctx_ref[...])

        @pl.when(t > 0)
        def _():
            emit(x_ref[...])
    else:
        emit(x_ref[...])


def _merge(res_inputs, a, r, gates, mods, g2, wa, wr, wo, wrt, n_ctx, with_ctx):
    b, l_a, d = a.shape
    l = gates.shape[1]
    tm = TOKEN_TILE
    off = 0 if with_ctx else n_ctx // tm
    nt = l // tm - off
    n_tok = b * nt * tm
    full = lambda shape: pl.BlockSpec(shape, lambda i, t: (0,) * len(shape))
    tile = lambda o: pl.BlockSpec((None, tm, d), lambda i, t: (i, t + o, 0))
    two_sources = len(res_inputs) == 2 and with_ctx
    if two_sources:
        res_specs = [pl.BlockSpec((None, tm, d), lambda i, t: (i, 0, 0)),
                     pl.BlockSpec((None, tm, d), lambda i, t: (i, jnp.maximum(t - 1, 0), 0))]
    elif len(res_inputs) == 2:
        res_inputs = res_inputs[1:]
        res_specs = [tile(0)]
    else:
        res_specs = [tile(off)]
    mspec = _mod_spec if with_ctx else _mod_spec_lat
    a_off = off - (l - l_a) // tm
    kern = functools.partial(_merge_kernel, two_sources=two_sources)
    return pl.pallas_call(
        kern,
        out_shape=[jax.ShapeDtypeStruct((b, nt * tm, d), F32),
                   jax.ShapeDtypeStruct((n_tok, d), F32),
                   jax.ShapeDtypeStruct((N_EXPERTS, n_tok), F32)],
        grid=(b, nt),
        in_specs=res_specs + [
            tile(a_off), tile(off),
            pl.BlockSpec((None, tm, d), lambda i, t: (i, t + off, 1)),
            pl.BlockSpec((None, tm, d), lambda i, t: (i, t + off, 2)),
            mspec(2, d), mspec(3, d), mspec(4, d), full((1, d)),
            full((d, d)), full((d, d)), full((d, d)), full((2, N_EXPERTS, d))],
        out_specs=[pl.BlockSpec((None, tm, d), lambda i, t: (i, t, 0)),
                   pl.BlockSpec((tm, d), lambda i, t: (i * nt + t, 0)),
                   pl.BlockSpec((N_EXPERTS, tm), lambda i, t: (0, i * nt + t))],
        compiler_params=_cparams(("parallel", "arbitrary")),
        name="merge_route",
    )(*res_inputs, a, r, gates, gates, mods, mods, mods, g2.reshape(1, d), wa, wr, wo, wrt)


def _moe_kernel(tile_a_ref, tile_b_ref, n_act_ref, src_ref, dst_ref, h_hbm, wts_ref,
                wga_ref, wua_ref, wda_ref, wgb_ref, wub_ref, wdb_ref, out_hbm,
                xbuf, ybuf, sem_in, sem_out, *, n_tok):
    i = pl.program_id(0)
    tm = MOE_TILE
    n_act = n_act_ref[0]
    slot = i % 2

    def row_copies(tile, sl, gather, start):
        base = tile * tm

        def body(j, c):
            r0 = j * SUBLANES
            for u in range(SUBLANES):
                if gather:
                    cp = pltpu.make_async_copy(h_hbm.at[pl.ds(src_ref[base + r0 + u], 1), :],
                                               xbuf.at[sl, j, pl.ds(u, 1), :], sem_in.at[sl])
                else:
                    cp = pltpu.make_async_copy(ybuf.at[sl, j, pl.ds(u, 1), :],
                                               out_hbm.at[pl.ds(dst_ref[base + r0 + u], 1), :], sem_out.at[sl])
                if start:
                    cp.start()
                else:
                    cp.wait()
            return c

        lax.fori_loop(0, tm // SUBLANES, body, 0)

    @pl.when(i == 0)
    def _():
        ybuf[0] = jnp.zeros(ybuf.shape[1:], ybuf.dtype)

        def pad_fill(j, start):
            cp = pltpu.make_async_copy(ybuf.at[0, j], out_hbm.at[pl.ds(n_tok + j * SUBLANES, SUBLANES), :],
                                       sem_out.at[0])
            cp.start() if start else cp.wait()

        pl.loop(0, tm // SUBLANES)(functools.partial(pad_fill, start=True))
        pl.loop(0, tm // SUBLANES)(functools.partial(pad_fill, start=False))

        @pl.when(n_act > 0)
        def _():
            row_copies(0, 0, gather=True, start=True)

    @pl.when(i < n_act)
    def _():
        row_copies(i, slot, gather=True, start=False)

        @pl.when(i + 1 < n_act)
        def _():
            row_copies(i + 1, 1 - slot, gather=True, start=True)

        x = xbuf[slot].reshape(tm, xbuf.shape[-1]).astype(BF16)

        def expert(wg_ref, wu_ref, wd_ref):
            gate = jnp.dot(x, wg_ref[...], preferred_element_type=F32)
            up = jnp.dot(x, wu_ref[...], preferred_element_type=F32)
            hid = (gate * _sigmoid(gate) * up).astype(BF16)
            return jnp.dot(hid, wd_ref[...], preferred_element_type=F32)

        wts = wts_ref[...]
        y = wts[:, 0:1] * expert(wga_ref, wua_ref, wda_ref) + wts[:, 1:2] * expert(wgb_ref, wub_ref, wdb_ref)

        @pl.when(i >= 2)
        def _():
            row_copies(i - 2, slot, gather=False, start=False)

        ybuf[slot] = y.reshape(ybuf.shape[1:])
        row_copies(i, slot, gather=False, start=True)

        @pl.when(i == n_act - 1)
        def _():
            row_copies(i, slot, gather=False, start=False)

            @pl.when(i >= 1)
            def _():
                row_copies(i - 1, 1 - slot, gather=False, start=False)


def _pair_class_experts():
    ea, eb = [], []
    for g in range(N_GROUPS):
        for i in range(EXPERTS_PER_GROUP):
            for j in range(i + 1, EXPERTS_PER_GROUP):
                ea.append(g * EXPERTS_PER_GROUP + i)
                eb.append(g * EXPERTS_PER_GROUP + j)
    return np.asarray(ea, np.int32), np.asarray(eb, np.int32)


def _moe_plan(route, n_tok):
    tm = MOE_TILE
    cls = route[0].astype(jnp.int32)
    onehot = (cls[:, None] == jnp.arange(N_PAIR_CLASSES, dtype=jnp.int32)[None, :]).astype(jnp.int32)
    csum = jnp.cumsum(onehot, axis=0)
    counts = csum[-1]
    padded = ((counts + tm - 1) // tm) * tm
    pends = jnp.cumsum(padded)
    pstarts = pends - padded
    pos = jnp.sum(onehot * (csum - 1 + pstarts[None, :]), axis=1)
    n_tiles = n_tok // tm + N_PAIR_CLASSES
    rows = jnp.arange(n_tiles * tm, dtype=jnp.int32)
    pad = jnp.stack([(n_tok + rows % tm).astype(F32), jnp.zeros_like(rows, F32), jnp.zeros_like(rows, F32)], axis=1)
    plan = pad.at[pos].set(jnp.stack([jnp.arange(n_tok, dtype=jnp.int32).astype(F32), route[3], route[4]], axis=1))
    tok = plan[:, 0].astype(jnp.int32)
    wts = plan[:, 1:3]
    tile_rows = rows[::tm]
    n_act = (pends[-1] // tm).reshape(1)
    tile_cls = jnp.sum((jnp.minimum(tile_rows, pends[-1] - 1)[:, None] >= pends[None, :]).astype(jnp.int32), axis=1)
    tile_cls = jnp.minimum(tile_cls, N_PAIR_CLASSES - 1)
    ea_np, eb_np = _pair_class_experts()
    ea = jnp.asarray(ea_np)[tile_cls]
    eb = jnp.asarray(eb_np)[tile_cls]
    return ea, eb, n_act, jnp.minimum(tok, n_tok - 1), tok, wts, n_tiles


def _moe(h2, route, wg, wu, wd):
    n_tok, d = h2.shape
    de = wg.shape[2]
    tm = MOE_TILE
    ea, eb, n_act, src, dst, wts, n_tiles = _moe_plan(route, n_tok)
    wspec_a = lambda r, c: pl.BlockSpec((None, r, c), lambda i, ta, tb, na, sr, ds: (ta[i], 0, 0))
    wspec_b = lambda r, c: pl.BlockSpec((None, r, c), lambda i, ta, tb, na, sr, ds: (tb[i], 0, 0))
    grid_spec = pltpu.PrefetchScalarGridSpec(
        num_scalar_prefetch=5,
        grid=(n_tiles,),
        in_specs=[pl.BlockSpec(memory_space=pl.ANY),
                  pl.BlockSpec((tm, 2), lambda i, ta, tb, na, sr, ds: (i, 0)),
                  wspec_a(d, de), wspec_a(d, de), wspec_a(de, d),
                  wspec_b(d, de), wspec_b(d, de), wspec_b(de, d)],
        out_specs=pl.BlockSpec(memory_space=pl.ANY),
        scratch_shapes=[pltpu.VMEM((2, tm // SUBLANES, SUBLANES, d), F32),
                        pltpu.VMEM((2, tm // SUBLANES, SUBLANES, d), F32),
                        pltpu.SemaphoreType.DMA((2,)), pltpu.SemaphoreType.DMA((2,))],
    )
    return pl.pallas_call(
        functools.partial(_moe_kernel, n_tok=n_tok),
        out_shape=jax.ShapeDtypeStruct((n_tok + tm, d), F32),
        grid_spec=grid_spec,
        compiler_params=pltpu.CompilerParams(dimension_semantics=("arbitrary",), vmem_limit_bytes=VMEM_LIMIT,
                                             has_side_effects=True),
        name="moe_pairs",
    )(ea, eb, n_act, src, dst, h2, wts, wg, wu, wd, wg, wu, wd)


def _post_kernel(x_ref, y_ref, gt_ref, g_ref, sh_ref, sc_ref, xo_ref, h_ref):
    xn = x_ref[...] + gt_ref[...] * y_ref[...]
    xo_ref[...] = xn
    h_ref[...] = _norm_mod(xn, g_ref[...], sh_ref[...], sc_ref[...]).astype(h_ref.dtype)


def _final_kernel(x_ref, y_ref, gt_ref, g_ref, o_ref):
    xn = x_ref[...] + gt_ref[...] * y_ref[...]
    o_ref[...] = xn * lax.rsqrt(jnp.mean(xn * xn, axis=-1, keepdims=True) + EPS) * g_ref[...]


def _post(x_all, moe_out, mods, mods_next, g_next):
    b, l, d = x_all.shape
    tm = TOKEN_TILE
    nt = l // tm
    tile = pl.BlockSpec((None, tm, d), lambda i, t: (i, t, 0))
    return pl.pallas_call(
        _post_kernel,
        out_shape=[jax.ShapeDtypeStruct((b, l, d), F32), jax.ShapeDtypeStruct((b, l, d), BF16)],
        grid=(b, nt),
        in_specs=[tile, pl.BlockSpec((tm, d), lambda i, t: (i * nt + t, 0)),
                  _mod_spec(5, d), pl.BlockSpec((1, d), lambda i, t: (0, 0)), _mod_spec(0, d), _mod_spec(1, d)],
        out_specs=[tile, tile],
        compiler_params=_cparams(("parallel", "arbitrary")),
        name="moe_residual_prenorm",
    )(x_all, moe_out, mods, g_next.reshape(1, d), mods_next, mods_next)


def _final(x_lat, moe_out, mods, g_final):
    b, s, d = x_lat.shape
    tm = TOKEN_TILE
    nt = s // tm
    tile = pl.BlockSpec((None, tm, d), lambda i, t: (i, t, 0))
    return pl.pallas_call(
        _final_kernel,
        out_shape=jax.ShapeDtypeStruct((b, s, d), F32),
        grid=(b, nt),
        in_specs=[tile, pl.BlockSpec((tm, d), lambda i, t: (i * nt + t, 0)),
                  _mod_spec_lat(5, d), pl.BlockSpec((1, d), lambda i, t: (0, 0))],
        out_specs=tile,
        compiler_params=_cparams(("parallel", "arbitrary")),
        name="moe_residual_final_norm",
    )(x_lat, moe_out, mods, g_final.reshape(1, d))


def _rope_tables(n_ctx, n_lat):
    t = np.arange(n_lat)
    n_freq = ATT_HEAD_DIM // 4
    inv = (ROPE_BASE ** (-np.arange(n_freq, dtype=np.float32) / n_freq)).astype(np.float32)
    ang_r = (t // GRID_W).astype(np.float32)[:, None] * inv
    ang_c = (t % GRID_W).astype(np.float32)[:, None] * inv
    ang = np.concatenate([ang_r, ang_r, ang_c, ang_c], axis=1)
    ang = np.concatenate([ang, ang], axis=1)
    sign = np.where((np.arange(HEAD_W) % 32) < 16, -1.0, 1.0).astype(np.float32)
    cos = np.concatenate([np.ones((n_ctx, HEAD_W), np.float32), np.cos(ang)], axis=0)
    sin = np.concatenate([np.zeros((n_ctx, HEAD_W), np.float32), np.sin(ang) * sign], axis=0)
    return jnp.asarray(cos, F32), jnp.asarray(sin, F32)


def kernel(x, c, ctx, c_ctx, w_mod, b_mod, g_norm1, g_norm2, w_in, lambda_q1, lambda_k1, lambda_q2, lambda_k2,
           g_subln, lb_logits, g_rec_norm, w_br_attn, w_br_rec, w_out, w_router, b_router, w_gate, w_up, w_down,
           g_final):
    b, s, d = x.shape
    n_ctx = ctx.shape[1]
    depth = w_in.shape[0]
    assert n_ctx == TOKEN_TILE and s % TOKEN_TILE == 0 and s % GRID_W == 0
    assert w_in.shape[2] == N_SPLITS * d

    bp = -(-(b + 1) // 8) * 8
    c_all = jnp.concatenate([c, c_ctx[None], jnp.zeros((bp - b - 1, d), F32)], axis=0)
    mod_all = _modulation(c_all, w_mod, b_mod)
    mods = [jnp.stack([jnp.broadcast_to(mod_all[l, b], (b, 6 * d)), mod_all[l, :b]], axis=1)
            .reshape(2 * b, 1, 6 * d) for l in range(depth)]

    lbs = jnp.cumsum(jax.nn.softmax(lb_logits.astype(F32), axis=0), axis=0)
    lbs = lbs - lbs[:1]
    cos_t, sin_t = _rope_tables(n_ctx, s)
    wrt_f = jnp.transpose(w_router).astype(F32)
    wrt_hi = wrt_f.astype(BF16)
    wrt = jnp.stack([wrt_hi, (wrt_f - wrt_hi.astype(F32)).astype(BF16)])
    l_all = n_ctx + s
    full2 = lambda shape: pl.BlockSpec(shape, lambda j, i: (0, 0))

    x_all = None
    out = None
    h = _prenorm(ctx, x, g_norm1[0], mods[0])
    for l in range(depth):
        last = l == depth - 1
        lam_init = 0.8 - 0.6 * math.exp(-0.3 * l)
        lam = (jnp.exp(jnp.sum(lambda_q1[l].astype(F32) * lambda_k1[l].astype(F32)))
               - jnp.exp(jnp.sum(lambda_q2[l].astype(F32) * lambda_k2[l].astype(F32))) + lam_init)
        w_bf = w_in[l].astype(BF16)
        (qk,) = _project(_proj_rope_kernel, h, w_bf, 0, 2, [cos_t, sin_t],
                         [full2((l_all, HEAD_W)), full2((l_all, HEAD_W))], [BF16], "proj_rope")
        (vq,) = _project(_proj_plain_kernel, h, w_bf, 2, 2, [], [], [BF16], "proj_v_rq")
        kk, gg = _project(_proj_forget_kernel, h, w_bf, 4, 2, [lbs[l].reshape(1, d)],
                          [full2((1, d))], [BF16, F32], "proj_forget")
        (ri,) = _project(_proj_plain_kernel, h, w_bf, 6, 1, [], [], [BF16], "proj_ri")
        (gates,) = _project(_proj_gates_kernel, h, w_bf, 7, 3, [], [], [BF16], "proj_gates")

        a = _attention(qk, vq, lam, g_subln[l], lam_init, n_ctx, with_ctx=not last)
        r = _recurrence(vq, kk, gg, ri, gates, g_rec_norm[l], n_ctx)

        res = (ctx, x) if l == 0 else (x_all,)
        xn, h2, logits_t = _merge(res, a, r, gates, mods[l], g_norm2[l], w_br_attn[l].astype(BF16),
                                  w_br_rec[l].astype(BF16), w_out[l].astype(BF16), wrt, n_ctx, with_ctx=not last)
        route = _routing(logits_t, b_router)
        moe_out = _moe(h2, route, w_gate[l].astype(BF16), w_up[l].astype(BF16), w_down[l].astype(BF16))
        if last:
            out = _final(xn, moe_out, mods[l], g_final)
        else:
            x_all, h = _post(xn, moe_out, mods[l], mods[l + 1], g_norm1[l + 1])
    return out
```

```python
import functools
import math

import numpy as np
import jax
import jax.numpy as jnp
from jax import lax
from jax.experimental import pallas as pl
from jax.experimental.pallas import tpu as pltpu

EPS = 1e-6
GRID_W = 64
ROPE_BASE = 10000.0
ATT_HEADS = 8
ATT_HEAD_DIM = 64
HEAD_W = 2 * ATT_HEAD_DIM
REC_HEADS = 8
N_EXPERTS = 16
N_GROUPS = 4
EXPERTS_PER_GROUP = N_EXPERTS // N_GROUPS
N_PAIR_CLASSES = N_GROUPS * 6
N_SPLITS = 10

LANES = 128
SUBLANES = 8
TOKEN_TILE = 256
PROJ_ROWS = 768
ATT_HEADS_PER_STEP = 4
GLA_CHUNK = 64
GLA_GROUP = 4
GLA_MIN_BCAST_BLOCK = 4
MOE_TILE = 256
VMEM_LIMIT = 56 * 1024 * 1024

F32 = jnp.float32
BF16 = jnp.bfloat16


def _sigmoid(x):
    return 1.0 / (1.0 + jnp.exp(-x))


def _cparams(sem):
    return pltpu.CompilerParams(dimension_semantics=sem, vmem_limit_bytes=VMEM_LIMIT)


def _mod_kernel(c_ref, w_ref, b_ref, o_ref):
    c = c_ref[...]
    s = c * _sigmoid(c)
    o_ref[...] = jnp.dot(s, w_ref[...], precision=lax.Precision.HIGHEST,
                         preferred_element_type=F32) + b_ref[...]


def _modulation(c_all, w_mod, b_mod):
    depth, d, n = w_mod.shape
    bp = c_all.shape[0]
    tn = 1024
    return pl.pallas_call(
        _mod_kernel,
        out_shape=jax.ShapeDtypeStruct((depth, bp, n), F32),
        grid=(depth, n // tn),
        in_specs=[pl.BlockSpec((bp, d), lambda l, j: (0, 0)),
                  pl.BlockSpec((None, d, tn), lambda l, j: (l, 0, j)),
                  pl.BlockSpec((None, 1, tn), lambda l, j: (l, 0, j))],
        out_specs=pl.BlockSpec((None, bp, tn), lambda l, j: (l, 0, j)),
        compiler_params=_cparams(("arbitrary", "arbitrary")),
        name="adaln_mod",
    )(c_all, w_mod, b_mod.reshape(depth, 1, n))


def _norm_mod(x, g, sh, sc):
    y = x * lax.rsqrt(jnp.mean(x * x, axis=-1, keepdims=True) + EPS) * g
    return y * (1.0 + sc) + sh


def _mod_spec(piece, d):
    return pl.BlockSpec((None, 1, d), lambda b, t: (2 * b + jnp.minimum(t, 1), 0, piece))


def _mod_spec_lat(piece, d):
    return pl.BlockSpec((None, 1, d), lambda b, t: (2 * b + 1, 0, piece))


def _prenorm_kernel(ctx_ref, x_ref, g_ref, sh_ref, sc_ref, h_ref):
    t = pl.program_id(1)

    def emit(xv):
        h_ref[...] = _norm_mod(xv, g_ref[...], sh_ref[...], sc_ref[...]).astype(BF16)

    @pl.when(t == 0)
    def _():
        emit(ctx_ref[...])

    @pl.when(t > 0)
    def _():
        emit(x_ref[...])


def _prenorm(ctx, x, g, mods):
    b, c, d = ctx.shape
    s = x.shape[1]
    tm = TOKEN_TILE
    nt = (c + s) // tm
    return pl.pallas_call(
        _prenorm_kernel,
        out_shape=jax.ShapeDtypeStruct((b, c + s, d), BF16),
        grid=(b, nt),
        in_specs=[pl.BlockSpec((None, tm, d), lambda i, t: (i, 0, 0)),
                  pl.BlockSpec((None, tm, d), lambda i, t: (i, jnp.maximum(t - 1, 0), 0)),
                  pl.BlockSpec((1, d), lambda i, t: (0, 0)),
                  _mod_spec(0, d), _mod_spec(1, d)],
        out_specs=pl.BlockSpec((None, tm, d), lambda i, t: (i, t, 0)),
        compiler_params=_cparams(("parallel", "arbitrary")),
        name="prenorm",
    )(ctx, x, g.reshape(1, d), mods, mods)


def _swap16(xs, first16):
    return jnp.where(first16, pltpu.roll(xs, LANES - 16, axis=1), pltpu.roll(xs, 16, axis=1))


def _proj_chunks(h_ref, w_ref, epilogue):
    n = h_ref.shape[0] // PROJ_ROWS
    rows = [slice(r * PROJ_ROWS, (r + 1) * PROJ_ROWS) for r in range(n)]
    mm = lambda r: jnp.dot(h_ref[rows[r], :], w_ref[...], preferred_element_type=F32)
    acc = mm(0)
    for r in range(n):
        nxt = mm(r + 1) if r + 1 < n else None
        epilogue(rows[r], acc)
        acc = nxt


def _proj_rope_kernel(h_ref, w_ref, cos_ref, sin_ref, o_ref):
    j = pl.program_id(0)
    scale = jnp.where(j == 0, ATT_HEAD_DIM ** -0.5 * math.log2(math.e), 1.0).astype(F32)
    lane = lax.broadcasted_iota(jnp.int32, (PROJ_ROWS, LANES), 1)
    first16 = (lane % 32) < 16

    def epilogue(rows, acc):
        cos = cos_ref[rows, :] * scale
        sin = sin_ref[rows, :] * scale
        for k in range(acc.shape[1] // LANES):
            xs = acc[:, k * LANES:(k + 1) * LANES]
            o_ref[rows, k * LANES:(k + 1) * LANES] = (xs * cos + _swap16(xs, first16) * sin).astype(o_ref.dtype)

    _proj_chunks(h_ref, w_ref, epilogue)


def _proj_plain_kernel(h_ref, w_ref, o_ref):
    def epilogue(rows, acc):
        o_ref[rows, :] = acc.astype(o_ref.dtype)

    _proj_chunks(h_ref, w_ref, epilogue)


def _proj_forget_kernel(h_ref, w_ref, lb_ref, k_ref, g_ref):
    lb = lb_ref[...]
    log_lb = jnp.log(lb)
    log_1m_lb = jnp.log1p(-lb)

    def epilogue(rows, z):
        log_sig = jnp.minimum(z, 0.0) - jnp.log(1.0 + jnp.exp(-jnp.abs(z)))
        bterm = log_1m_lb + log_sig
        mx = jnp.maximum(log_lb, bterm)
        mn = jnp.minimum(log_lb, bterm)
        log_f = mx + jnp.log(1.0 + jnp.exp(mn - mx))
        g_ref[rows, :] = log_f
        k_ref[rows, :] = (1.0 - jnp.exp(log_f)).astype(k_ref.dtype)

    _proj_chunks(h_ref, w_ref, epilogue)


def _proj_forget_nobound_kernel(h_ref, w_ref, k_ref, g_ref):
    def epilogue(rows, z):
        log_f = jnp.minimum(z, 0.0) - jnp.log(1.0 + jnp.exp(-jnp.abs(z)))
        g_ref[rows, :] = log_f
        k_ref[rows, :] = (1.0 - jnp.exp(log_f)).astype(k_ref.dtype)

    _proj_chunks(h_ref, w_ref, epilogue)


def _proj_gates_kernel(h_ref, w_ref, o_ref):
    j = pl.program_id(0)

    def epilogue(rows, z):
        sg = _sigmoid(z)
        o_ref[rows, :] = jnp.where(j == 0, z * sg, sg).astype(o_ref.dtype)

    _proj_chunks(h_ref, w_ref, epilogue)


def _project(kernel, h, w_bf, first_split, n_splits, extra_inputs, extra_specs, out_dtypes, name):
    b, l, d = h.shape
    wcol = w_bf.shape[1] // N_SPLITS
    out_shape = [jax.ShapeDtypeStruct((b, l, n_splits * wcol), dt) for dt in out_dtypes]
    out_specs = [pl.BlockSpec((None, l, wcol), lambda j, i: (i, 0, j)) for _ in out_dtypes]
    res = pl.pallas_call(
        kernel,
        out_shape=out_shape,
        grid=(n_splits, b),
        in_specs=[pl.BlockSpec((None, l, d), lambda j, i: (i, 0, 0)),
                  pl.BlockSpec((d, wcol), lambda j, i: (0, first_split + j))] + extra_specs,
        out_specs=out_specs,
        compiler_params=_cparams(("arbitrary", "arbitrary")),
        name=name,
    )(h, w_bf, *extra_inputs)
    return res


def _attn_kernel(lam_ref, q_ref, k_ref, v_ref, g_ref, o_ref, *, n_ctx, ctx_tile, out_scale):
    lam = lam_ref[0]
    nt = (((1,), (1,)), ((), ()))
    heads = range(ATT_HEADS_PER_STEP)
    col = lambda hh: slice(hh * HEAD_W, (hh + 1) * HEAD_W)

    def attend(n_keys):
        scores = []
        for hh in heads:
            q = q_ref[:, col(hh)]
            k = k_ref[0:n_keys, col(hh)]
            lane = lax.broadcasted_iota(jnp.int32, q.shape, 1)
            zero = jnp.zeros_like(q)
            q1 = jnp.where(lane < ATT_HEAD_DIM, q, zero)
            q2 = jnp.where(lane < ATT_HEAD_DIM, zero, q)
            scores.append((lax.dot_general(q1, k, nt, preferred_element_type=F32),
                           lax.dot_general(q2, k, nt, preferred_element_type=F32)))
        for hh in heads:
            v = v_ref[0:n_keys, col(hh)]
            v1 = jnp.concatenate([v, jnp.ones_like(v)], axis=1)
            outs = []
            for sm in scores[hh]:
                p = jnp.exp2(sm - jnp.max(sm, axis=-1, keepdims=True))
                outs.append(jnp.dot(p.astype(BF16), v1, preferred_element_type=F32))
            (o1, o2) = outs
            l1, l2 = o1[:, HEAD_W:HEAD_W + 1], o2[:, HEAD_W:HEAD_W + 1]
            o = o1[:, :HEAD_W] * (1.0 / l1) - o2[:, :HEAD_W] * (lam / l2)
            o = o * lax.rsqrt(jnp.mean(o * o, axis=-1, keepdims=True) + EPS) * g_ref[...] * out_scale
            o_ref[:, col(hh)] = o.astype(o_ref.dtype)

    n_all = k_ref.shape[0]
    if ctx_tile:
        t = pl.program_id(2)

        @pl.when(t == 0)
        def _():
            attend(n_ctx)

        @pl.when(t > 0)
        def _():
            attend(n_all)
    else:
        attend(n_all)


def _attention(qk, vq, lam, g_subln, lam_init, n_ctx, with_ctx):
    b, l, _ = qk.shape
    hp = ATT_HEADS_PER_STEP
    ng = ATT_HEADS // hp
    tq = TOKEN_TILE
    off = 0 if with_ctx else n_ctx // tq
    n_out = l if with_ctx else l - n_ctx
    kern = functools.partial(_attn_kernel, n_ctx=n_ctx, ctx_tile=with_ctx, out_scale=1.0 - lam_init)
    return pl.pallas_call(
        kern,
        out_shape=jax.ShapeDtypeStruct((b, n_out, ATT_HEADS * HEAD_W), BF16),
        grid=(b, ng, n_out // tq),
        in_specs=[pl.BlockSpec(memory_space=pltpu.SMEM),
                  pl.BlockSpec((None, tq, hp * HEAD_W), lambda i, h, t: (i, t + off, h)),
                  pl.BlockSpec((None, l, hp * HEAD_W), lambda i, h, t: (i, 0, ng + h)),
                  pl.BlockSpec((None, l, hp * HEAD_W), lambda i, h, t: (i, 0, h)),
                  pl.BlockSpec((1, HEAD_W), lambda i, h, t: (0, 0))],
        out_specs=pl.BlockSpec((None, tq, hp * HEAD_W), lambda i, h, t: (i, t, h)),
        compiler_params=_cparams(("parallel", "arbitrary", "arbitrary")),
        name="diff_attention",
    )(lam.reshape(1), qk, qk, vq, g_subln.reshape(1, HEAD_W))


def _gla_tables(chunk):
    c = chunk
    levels = []
    b = c // 2
    while b >= 1:
        levels.append(b)
        b //= 2
    t = np.arange(c)
    u = np.arange(c)[None, :]

    def build(forward):
        mats, qrows, pmask = [], [], []
        tt = t[:, None]
        mats.append((u <= tt) if forward else (u >= tt))
        for b in levels:
            pair = t // (2 * b)
            late = (t % (2 * b)) >= b
            m = pair * 2 * b + b - 1
            if forward:
                is_q = late
                lo = np.where(is_q, m + 1, t + 1)[:, None]
                hi = np.where(is_q, t, m)[:, None]
            else:
                is_q = ~late
                lo = np.where(is_q, t, m + 1)[:, None]
                hi = np.where(is_q, m, t - 1)[:, None]
            if b < GLA_MIN_BCAST_BLOCK:
                mats.append((u >= lo) & (u <= hi))
            qrows.append(np.repeat(is_q[:, None], LANES, 1).astype(np.float32))
            pmask.append(((pair[:, None] == pair[None, :]) & is_q[:, None] & (~is_q)[None, :]).astype(np.float32))
        return np.concatenate(mats, 0).astype(np.float32), np.stack(qrows), np.stack(pmask)

    mf, qf, pf = build(True)
    mb, qb, pb = build(False)
    return levels, np.stack([mf, mb]), np.stack([qf, qb]), np.stack([pf, pb])


def _gla_kernel(q_ref, kf_ref, kb_ref, v_ref, gf_ref, gb_ref, gate_ref, gn_ref, m_ref, qrow_ref, pm_ref,
                o_ref, of_ref, ob_ref, sf_ref, sb_ref, *, levels, n_ctx_chunks, n_chunks, ctx_outputs):
    c = GLA_CHUNK
    u_n = GLA_GROUP
    n_levels = len(levels)
    sf_ref[...] = jnp.zeros_like(sf_ref)
    sb_ref[...] = jnp.zeros_like(sb_ref)
    tn = (((0,), (0,)), ((), ()))
    nt = (((1,), (1,)), ((), ()))

    def decays(direction, r0, g_ref):
        rows = [pl.ds(r0 + u * c, c) for u in range(u_n)]
        g = jnp.concatenate([g_ref[rw, :] for rw in rows], axis=1)
        g_hi = g.astype(BF16)
        r1 = g - g_hi.astype(F32)
        g_mid = r1.astype(BF16)
        g_lo = (r1 - g_mid.astype(F32)).astype(BF16)
        m = m_ref[direction]
        sums = (jnp.dot(m, g_hi, preferred_element_type=F32) + jnp.dot(m, g_mid, preferred_element_type=F32)
                + jnp.dot(m, g_lo, preferred_element_type=F32))
        run = sums[0:c, :]

        def to_row(blk, row):
            return -jnp.abs(blk - jnp.broadcast_to(run[row:row + 1, :], blk.shape))

        expo, n_direct = [], 0
        for b in levels:
            if b >= GLA_MIN_BCAST_BLOCK:
                parts = [to_row(run[p * 2 * b:(p + 1) * 2 * b, :], p * 2 * b + (b - 1 if direction == 0 else b))
                         for p in range(c // (2 * b))]
                expo.append(parts[0] if len(parts) == 1 else jnp.concatenate(parts, axis=0))
            else:
                n_direct += 1
                expo.append(sums[n_direct * c:(n_direct + 1) * c, :])
        expo.append(run)
        expo.append(to_row(run, c - 1 if direction == 0 else 0))
        return [jnp.exp(e) for e in expo]

    def group_pair(r0s, outputs=True):
        k_refs, g_refs, s_refs, out_refs = (kf_ref, kb_ref), (gf_ref, gb_ref), (sf_ref, sb_ref), (of_ref, ob_ref)
        dec = [decays(d, r0s[d], g_refs[d]) for d in (0, 1)]
        work = []
        for d in (0, 1):
            for u in range(u_n):
                rw = pl.ds(r0s[d] + u * c, c)
                k = k_refs[d][rw, :].astype(F32)
                v = v_ref[rw, :]
                du = [e[:, u * LANES:(u + 1) * LANES] for e in dec[d]]
                q_dec, k_dec = du[n_levels], du[n_levels + 1]
                w = dict(d=d, rw=rw, v=v, k_state=(k * k_dec).astype(BF16),
                         total=q_dec[c - 1:c, :] if d == 0 else q_dec[0:1, :])
                if outputs:
                    q = q_ref[rw, :].astype(F32)
                    level_prods = []
                    for lv in range(n_levels):
                        is_q = qrow_ref[d, lv] > 0.5
                        x = (jnp.where(is_q, q, k) * du[lv]).astype(BF16)
                        level_prods.append(lax.dot_general(x, x, nt, preferred_element_type=F32))
                    w.update(prods=level_prods, q_state=(q * q_dec).astype(BF16),
                             diag=jnp.sum(q * k, axis=-1, keepdims=True) * v.astype(F32))
                work.append(w)
        for w in work:
            if outputs:
                scores = sum(pm_ref[w["d"], lv] * w["prods"][lv] for lv in range(n_levels))
                w["intra"] = w["diag"] + jnp.dot(scores.astype(BF16), w["v"], preferred_element_type=F32)
            w["vk"] = lax.dot_general(w["v"], w["k_state"], tn, preferred_element_type=F32)
        for d in (0, 1):
            state_t = s_refs[d][...]
            mine = [w for w in work if w["d"] == d]
            for w in (mine if d == 0 else mine[::-1]):
                if outputs:
                    out_refs[d][w["rw"], :] = w["intra"] + lax.dot_general(
                        w["q_state"], state_t.astype(BF16), nt, preferred_element_type=F32)
                state_t = state_t * w["total"] + w["vk"]
            s_refs[d][...] = state_t

    n_ctx_groups = n_ctx_chunks // u_n
    n_groups = n_chunks // u_n

    def rows_of(i):
        top = n_ctx_chunks - i * u_n if i < n_ctx_groups else n_chunks - (i - n_ctx_groups) * u_n
        return i * u_n * c, (top - u_n) * c

    for i in range(n_groups):
        group_pair(rows_of(i), outputs=ctx_outputs or i >= n_ctx_groups)
    n_ctx_rows = n_ctx_chunks * c
    if ctx_outputs:
        out_rows = slice(0, n_chunks * c)
    else:
        out_rows = slice(n_ctx_rows, n_chunks * c)
        o_ref[0:n_ctx_rows, :] = jnp.zeros((n_ctx_rows, LANES), o_ref.dtype)
    o = of_ref[out_rows, :] + ob_ref[out_rows, :]
    o = o * lax.rsqrt(jnp.mean(o * o, axis=-1, keepdims=True) + EPS) * gn_ref[...]
    o_ref[out_rows, :] = (o * gate_ref[out_rows, :].astype(F32)).astype(o_ref.dtype)


def _recurrence(vq, kk, gg, ri, gates, g_rec, n_ctx, ctx_outputs):
    b, l, _ = vq.shape
    c = GLA_CHUNK
    levels, m_np, qrow_np, pm_np = _gla_tables(c)
    nh = REC_HEADS
    assert (n_ctx // c) % GLA_GROUP == 0 and ((l - n_ctx) // c) % GLA_GROUP == 0
    kern = functools.partial(_gla_kernel, levels=tuple(levels), n_ctx_chunks=n_ctx // c, n_chunks=l // c,
                             ctx_outputs=ctx_outputs)
    head = lambda off: pl.BlockSpec((None, l, LANES), lambda i, h: (i, 0, h + off))
    const = lambda a: pl.BlockSpec(a.shape, lambda i, h: (0,) * a.ndim)
    m_all = jnp.asarray(m_np, BF16)
    qrow = jnp.asarray(qrow_np, F32)
    pm = jnp.asarray(pm_np, F32)
    return pl.pallas_call(
        kern,
        out_shape=jax.ShapeDtypeStruct((b, l, nh * LANES), BF16),
        grid=(b, nh),
        in_specs=[head(nh), head(0), head(nh), head(0), head(0), head(nh), head(0),
                  pl.BlockSpec((1, LANES), lambda i, h: (0, 0)),
                  const(m_all), const(qrow), const(pm)],
        out_specs=head(0),
        scratch_shapes=[pltpu.VMEM((l, LANES), F32), pltpu.VMEM((l, LANES), F32),
                        pltpu.VMEM((LANES, LANES), F32), pltpu.VMEM((LANES, LANES), F32)],
        compiler_params=_cparams(("parallel", "arbitrary")),
        name="hgrn2_recurrence",
    )(vq, kk, kk, ri, gg, gg, gates, g_rec.reshape(1, LANES), m_all, qrow, pm)


def _route(logits, bias):
    scores = [_sigmoid(lg) for lg in logits]
    biased = [sc + bs for sc, bs in zip(scores, bias)]
    row = lambda a, e: a[e]
    epg = EXPERTS_PER_GROUP
    pairs = [(i, j) for i in range(epg) for j in range(i + 1, epg)]
    best_gs, grp = None, None
    for gi in range(N_GROUPS):
        gs = None
        for (i, j) in pairs:
            s = row(biased, gi * epg + i) + row(biased, gi * epg + j)
            gs = s if gs is None else jnp.maximum(gs, s)
        if best_gs is None:
            best_gs, grp = gs, jnp.zeros_like(gs, dtype=jnp.int32)
        else:
            better = gs > best_gs
            grp = jnp.where(better, gi, grp)
            best_gs = jnp.maximum(best_gs, gs)

    def in_group(a, j):
        out = row(a, j)
        for gi in range(1, N_GROUPS):
            out = jnp.where(grp == gi, row(a, gi * epg + j), out)
        return out

    u = [in_group(biased, j) for j in range(epg)]
    sc = [in_group(scores, j) for j in range(epg)]
    b1, i1 = u[0], jnp.zeros_like(grp)
    for j in range(1, epg):
        better = u[j] > b1
        i1 = jnp.where(better, j, i1)
        b1 = jnp.maximum(b1, u[j])
    i2 = jnp.where(i1 == 0, 1, 0)
    b2 = jnp.where(i1 == 0, u[1], u[0])
    for j in range(1, epg):
        better = (i1 != j) & (u[j] > b2)
        i2 = jnp.where(better, j, i2)
        b2 = jnp.where(better, u[j], b2)
    pick = lambda idx: sum(jnp.where(idx == j, sc[j], 0.0) for j in range(epg))
    w1, w2 = pick(i1), pick(i2)
    tot = w1 + w2
    w1, w2 = w1 / tot, w2 / tot
    la, lb2 = jnp.minimum(i1, i2), jnp.maximum(i1, i2)
    wa = jnp.where(i1 < i2, w1, w2)
    wb = jnp.where(i1 < i2, w2, w1)
    pair_idx = jnp.where(la == 0, lb2 - 1, jnp.where(la == 1, lb2 + 1, 5))
    cls = grp * 6 + pair_idx
    ea = grp * epg + la
    eb = grp * epg + lb2
    return [cls.astype(F32), ea.astype(F32), eb.astype(F32), wa, wb]


def _route_kernel(bias_ref, logits_ref, route_ref):
    rows = _route([logits_ref[e] for e in range(N_EXPERTS)], [bias_ref[e] for e in range(N_EXPERTS)])
    for k, r in enumerate(rows):
        route_ref[k] = r
    for k in range(len(rows), route_ref.shape[0]):
        route_ref[k] = jnp.zeros(route_ref.shape[1:], F32)


def _routing(logits_t, b_router):
    ne, n_tok = logits_t.shape
    w = n_tok // SUBLANES
    out = pl.pallas_call(
        _route_kernel,
        out_shape=jax.ShapeDtypeStruct((8, SUBLANES, w), F32),
        grid=(1,),
        in_specs=[pl.BlockSpec(memory_space=pltpu.SMEM),
                  pl.BlockSpec((ne, SUBLANES, w), lambda i: (0, 0, 0))],
        out_specs=pl.BlockSpec((8, SUBLANES, w), lambda i: (0, 0, 0)),
        compiler_params=_cparams(("arbitrary",)),
        name="route_top2",
    )(b_router.astype(F32), logits_t.reshape(ne, SUBLANES, w))
    return out.reshape(8, n_tok)


def _merge_kernel(*refs, two_sources):
    if two_sources:
        (ctx_ref, x_ref, a_ref, r_ref, ga_ref, gr_ref, gt_ref, sh_ref, sc_ref, g2_ref,
         wa_ref, wr_ref, wo_ref, wrt_ref, xn_ref, h2_ref, logit_ref) = refs
    else:
        (x_ref, a_ref, r_ref, ga_ref, gr_ref, gt_ref, sh_ref, sc_ref, g2_ref,
         wa_ref, wr_ref, wo_ref, wrt_ref, xn_ref, h2_ref, logit_ref) = refs

    def emit(xv):
        ua = jnp.dot(a_ref[...], wa_ref[...], preferred_element_type=F32)
        ur = jnp.dot(r_ref[...], wr_ref[...], preferred_element_type=F32)
        mixed = ga_ref[...].astype(F32) * ua + gr_ref[...].astype(F32) * ur
        y = jnp.dot(mixed.astype(BF16), wo_ref[...], preferred_element_type=F32)
        xn = xv + gt_ref[...] * y
        xn_ref[...] = xn
        h2 = _norm_mod(xn, g2_ref[...], sh_ref[...], sc_ref[...])
        h2_ref[...] = h2
        nt = (((1,), (1,)), ((), ()))
        h_hi = h2.astype(BF16)
        h_lo = (h2 - h_hi.astype(F32)).astype(BF16)
        w_hi = wrt_ref[0]
        w_lo = wrt_ref[1]
        logit_ref[...] = (lax.dot_general(w_hi, h_hi, nt, preferred_element_type=F32)
                          + lax.dot_general(w_lo, h_hi, nt, preferred_element_type=F32)
                          + lax.dot_general(w_hi, h_lo, nt, preferred_element_type=F32))

    if two_sources:
        t = pl.program_id(1)

        @pl.when(t == 0)
        def _():
            emit(ctx_ref[...])

        @pl.when(t > 0)
        def _():
            emit(x_ref[...])
    else:
        emit(x_ref[...])


def _merge(res_inputs, a, r, gates, mods, g2, wa, wr, wo, wrt, n_ctx, with_ctx):
    b, l_a, d = a.shape
    l = gates.shape[1]
    tm = TOKEN_TILE
    off = 0 if with_ctx else n_ctx // tm
    nt = l // tm - off
    n_tok = b * nt * tm
    full = lambda shape: pl.BlockSpec(shape, lambda i, t: (0,) * len(shape))
    tile = lambda o: pl.BlockSpec((None, tm, d), lambda i, t: (i, t + o, 0))
    two_sources = len(res_inputs) == 2 and with_ctx
    if two_sources:
        res_specs = [pl.BlockSpec((None, tm, d), lambda i, t: (i, 0, 0)),
                     pl.BlockSpec((None, tm, d), lambda i, t: (i, jnp.maximum(t - 1, 0), 0))]
    elif len(res_inputs) == 2:
        res_inputs = res_inputs[1:]
        res_specs = [tile(0)]
    else:
        res_specs = [tile(off)]
    mspec = _mod_spec if with_ctx else _mod_spec_lat
    a_off = off - (l - l_a) // tm
    kern = functools.partial(_merge_kernel, two_sources=two_sources)
    return pl.pallas_call(
        kern,
        out_shape=[jax.ShapeDtypeStruct((b, nt * tm, d), F32),
                   jax.ShapeDtypeStruct((n_tok, d), F32),
                   jax.ShapeDtypeStruct((N_EXPERTS, n_tok), F32)],
        grid=(b, nt),
        in_specs=res_specs + [
            tile(a_off), tile(off),
            pl.BlockSpec((None, tm, d), lambda i, t: (i, t + off, 1)),
            pl.BlockSpec((None, tm, d), lambda i, t: (i, t + off, 2)),
            mspec(2, d), mspec(3, d), mspec(4, d), full((1, d)),
            full((d, d)), full((d, d)), full((d, d)), full((2, N_EXPERTS, d))],
        out_specs=[pl.BlockSpec((None, tm, d), lambda i, t: (i, t, 0)),
                   pl.BlockSpec((tm, d), lambda i, t: (i * nt + t, 0)),
                   pl.BlockSpec((N_EXPERTS, tm), lambda i, t: (0, i * nt + t))],
        compiler_params=_cparams(("parallel", "arbitrary")),
        name="merge_route",
    )(*res_inputs, a, r, gates, gates, mods, mods, mods, g2.reshape(1, d), wa, wr, wo, wrt)


def _moe_kernel(tile_a_ref, tile_b_ref, n_act_ref, src_ref, dst_ref, h_hbm, wts_ref,
                wga_ref, wua_ref, wda_ref, wgb_ref, wub_ref, wdb_ref, out_hbm,
                xbuf, ybuf, sem_in, sem_out, *, n_tok):
    i = pl.program_id(0)
    tm = MOE_TILE
    n_act = n_act_ref[0]
    slot = i % 2

    def row_copies(tile, sl, gather, start):
        base = tile * tm

        def body(j, c):
            r0 = j * SUBLANES
            for u in range(SUBLANES):
                if gather:
                    cp = pltpu.make_async_copy(h_hbm.at[pl.ds(src_ref[base + r0 + u], 1), :],
                                               xbuf.at[sl, j, pl.ds(u, 1), :], sem_in.at[sl])
                else:
                    cp = pltpu.make_async_copy(ybuf.at[sl, j, pl.ds(u, 1), :],
                                               out_hbm.at[pl.ds(dst_ref[base + r0 + u], 1), :], sem_out.at[sl])
                if start:
                    cp.start()
                else:
                    cp.wait()
            return c

        lax.fori_loop(0, tm // SUBLANES, body, 0)

    @pl.when(i == 0)
    def _():
        ybuf[0] = jnp.zeros(ybuf.shape[1:], ybuf.dtype)

        def pad_fill(j, start):
            cp = pltpu.make_async_copy(ybuf.at[0, j], out_hbm.at[pl.ds(n_tok + j * SUBLANES, SUBLANES), :],
                                       sem_out.at[0])
            cp.start() if start else cp.wait()

        pl.loop(0, tm // SUBLANES)(functools.partial(pad_fill, start=True))
        pl.loop(0, tm // SUBLANES)(functools.partial(pad_fill, start=False))

        @pl.when(n_act > 0)
        def _():
            row_copies(0, 0, gather=True, start=True)

    @pl.when(i < n_act)
    def _():
        row_copies(i, slot, gather=True, start=False)

        @pl.when(i + 1 < n_act)
        def _():
            row_copies(i + 1, 1 - slot, gather=True, start=True)

        x = xbuf[slot].reshape(tm, xbuf.shape[-1]).astype(BF16)

        def expert(wg_ref, wu_ref, wd_ref):
            gate = jnp.dot(x, wg_ref[...], preferred_element_type=F32)
            up = jnp.dot(x, wu_ref[...], preferred_element_type=F32)
            hid = (gate * _sigmoid(gate) * up).astype(BF16)
            return jnp.dot(hid, wd_ref[...], preferred_element_type=F32)

        wts = wts_ref[...]
        y = wts[:, 0:1] * expert(wga_ref, wua_ref, wda_ref) + wts[:, 1:2] * expert(wgb_ref, wub_ref, wdb_ref)

        @pl.when(i >= 2)
        def _():
            row_copies(i - 2, slot, gather=False, start=False)

        ybuf[slot] = y.reshape(ybuf.shape[1:])
        row_copies(i, slot, gather=False, start=True)

        @pl.when(i == n_act - 1)
        def _():
            row_copies(i, slot, gather=False, start=False)

            @pl.when(i >= 1)
            def _():
                row_copies(i - 1, 1 - slot, gather=False, start=False)


def _pair_class_experts():
    ea, eb = [], []
    for g in range(N_GROUPS):
        for i in range(EXPERTS_PER_GROUP):
            for j in range(i + 1, EXPERTS_PER_GROUP):
                ea.append(g * EXPERTS_PER_GROUP + i)
                eb.append(g * EXPERTS_PER_GROUP + j)
    return np.asarray(ea, np.int32), np.asarray(eb, np.int32)


def _moe_plan(route, n_tok):
    tm = MOE_TILE
    cls = route[0].astype(jnp.int32)
    onehot = (cls[:, None] == jnp.arange(N_PAIR_CLASSES, dtype=jnp.int32)[None, :]).astype(jnp.int32)
    csum = jnp.cumsum(onehot, axis=0)
    counts = csum[-1]
    padded = ((counts + tm - 1) // tm) * tm
    pends = jnp.cumsum(padded)
    pstarts = pends - padded
    pos = jnp.sum(onehot * (csum - 1 + pstarts[None, :]), axis=1)
    n_tiles = n_tok // tm + N_PAIR_CLASSES
    rows = jnp.arange(n_tiles * tm, dtype=jnp.int32)
    pad = jnp.stack([(n_tok + rows % tm).astype(F32), jnp.zeros_like(rows, F32), jnp.zeros_like(rows, F32)], axis=1)
    plan = pad.at[pos].set(jnp.stack([jnp.arange(n_tok, dtype=jnp.int32).astype(F32), route[3], route[4]], axis=1))
    tok = plan[:, 0].astype(jnp.int32)
    wts = plan[:, 1:3]
    tile_rows = rows[::tm]
    n_act = (pends[-1] // tm).reshape(1)
    tile_cls = jnp.sum((jnp.minimum(tile_rows, pends[-1] - 1)[:, None] >= pends[None, :]).astype(jnp.int32), axis=1)
    tile_cls = jnp.minimum(tile_cls, N_PAIR_CLASSES - 1)
    ea_np, eb_np = _pair_class_experts()
    ea = jnp.asarray(ea_np)[tile_cls]
    eb = jnp.asarray(eb_np)[tile_cls]
    return ea, eb, n_act, jnp.minimum(tok, n_tok - 1), tok, wts, n_tiles


def _moe(h2, route, wg, wu, wd):
    n_tok, d = h2.shape
    de = wg.shape[2]
    tm = MOE_TILE
    ea, eb, n_act, src, dst, wts, n_tiles = _moe_plan(route, n_tok)
    wspec_a = lambda r, c: pl.BlockSpec((None, r, c), lambda i, ta, tb, na, sr, ds: (ta[i], 0, 0))
    wspec_b = lambda r, c: pl.BlockSpec((None, r, c), lambda i, ta, tb, na, sr, ds: (tb[i], 0, 0))
    grid_spec = pltpu.PrefetchScalarGridSpec(
        num_scalar_prefetch=5,
        grid=(n_tiles,),
        in_specs=[pl.BlockSpec(memory_space=pl.ANY),
                  pl.BlockSpec((tm, 2), lambda i, ta, tb, na, sr, ds: (i, 0)),
                  wspec_a(d, de), wspec_a(d, de), wspec_a(de, d),
                  wspec_b(d, de), wspec_b(d, de), wspec_b(de, d)],
        out_specs=pl.BlockSpec(memory_space=pl.ANY),
        scratch_shapes=[pltpu.VMEM((2, tm // SUBLANES, SUBLANES, d), F32),
                        pltpu.VMEM((2, tm // SUBLANES, SUBLANES, d), F32),
                        pltpu.SemaphoreType.DMA((2,)), pltpu.SemaphoreType.DMA((2,))],
    )
    return pl.pallas_call(
        functools.partial(_moe_kernel, n_tok=n_tok),
        out_shape=jax.ShapeDtypeStruct((n_tok + tm, d), F32),
        grid_spec=grid_spec,
        compiler_params=pltpu.CompilerParams(dimension_semantics=("arbitrary",), vmem_limit_bytes=VMEM_LIMIT,
                                             has_side_effects=True),
        name="moe_pairs",
    )(ea, eb, n_act, src, dst, h2, wts, wg, wu, wd, wg, wu, wd)


def _post_kernel(x_ref, y_ref, gt_ref, g_ref, sh_ref, sc_ref, xo_ref, h_ref):
    xn = x_ref[...] + gt_ref[...] * y_ref[...]
    xo_ref[...] = xn
    h_ref[...] = _norm_mod(xn, g_ref[...], sh_ref[...], sc_ref[...]).astype(h_ref.dtype)


def _final_kernel(x_ref, y_ref, gt_ref, g_ref, o_ref):
    xn = x_ref[...] + gt_ref[...] * y_ref[...]
    o_ref[...] = xn * lax.rsqrt(jnp.mean(xn * xn, axis=-1, keepdims=True) + EPS) * g_ref[...]


def _post(x_all, moe_out, mods, mods_next, g_next):
    b, l, d = x_all.shape
    tm = TOKEN_TILE
    nt = l // tm
    tile = pl.BlockSpec((None, tm, d), lambda i, t: (i, t, 0))
    return pl.pallas_call(
        _post_kernel,
        out_shape=[jax.ShapeDtypeStruct((b, l, d), F32), jax.ShapeDtypeStruct((b, l, d), BF16)],
        grid=(b, nt),
        in_specs=[tile, pl.BlockSpec((tm, d), lambda i, t: (i * nt + t, 0)),
                  _mod_spec(5, d), pl.BlockSpec((1, d), lambda i, t: (0, 0)), _mod_spec(0, d), _mod_spec(1, d)],
        out_specs=[tile, tile],
        compiler_params=_cparams(("parallel", "arbitrary")),
        name="moe_residual_prenorm",
    )(x_all, moe_out, mods, g_next.reshape(1, d), mods_next, mods_next)


def _final(x_lat, moe_out, mods, g_final):
    b, s, d = x_lat.shape
    tm = TOKEN_TILE
    nt = s // tm
    tile = pl.BlockSpec((None, tm, d), lambda i, t: (i, t, 0))
    return pl.pallas_call(
        _final_kernel,
        out_shape=jax.ShapeDtypeStruct((b, s, d), F32),
        grid=(b, nt),
        in_specs=[tile, pl.BlockSpec((tm, d), lambda i, t: (i * nt + t, 0)),
                  _mod_spec_lat(5, d), pl.BlockSpec((1, d), lambda i, t: (0, 0))],
        out_specs=tile,
        compiler_params=_cparams(("parallel", "arbitrary")),
        name="moe_residual_final_norm",
    )(x_lat, moe_out, mods, g_final.reshape(1, d))


def _rope_tables(n_ctx, n_lat):
    t = np.arange(n_lat)
    n_freq = ATT_HEAD_DIM // 4
    inv = (ROPE_BASE ** (-np.arange(n_freq, dtype=np.float32) / n_freq)).astype(np.float32)
    ang_r = (t // GRID_W).astype(np.float32)[:, None] * inv
    ang_c = (t % GRID_W).astype(np.float32)[:, None] * inv
    ang = np.concatenate([ang_r, ang_r, ang_c, ang_c], axis=1)
    ang = np.concatenate([ang, ang], axis=1)
    sign = np.where((np.arange(HEAD_W) % 32) < 16, -1.0, 1.0).astype(np.float32)
    cos = np.concatenate([np.ones((n_ctx, HEAD_W), np.float32), np.cos(ang)], axis=0)
    sin = np.concatenate([np.zeros((n_ctx, HEAD_W), np.float32), np.sin(ang) * sign], axis=0)
    return jnp.asarray(cos, F32), jnp.asarray(sin, F32)


def kernel(x, c, ctx, c_ctx, w_mod, b_mod, g_norm1, g_norm2, w_in, lambda_q1, lambda_k1, lambda_q2, lambda_k2,
           g_subln, lb_logits, g_rec_norm, w_br_attn, w_br_rec, w_out, w_router, b_router, w_gate, w_up, w_down,
           g_final):
    b, s, d = x.shape
    n_ctx = ctx.shape[1]
    depth = w_in.shape[0]
    assert n_ctx == TOKEN_TILE and s % TOKEN_TILE == 0 and s % GRID_W == 0
    assert w_in.shape[2] == N_SPLITS * d

    bp = -(-(b + 1) // 8) * 8
    c_all = jnp.concatenate([c, c_ctx[None], jnp.zeros((bp - b - 1, d), F32)], axis=0)
    mod_all = _modulation(c_all, w_mod, b_mod)
    mods = [jnp.stack([jnp.broadcast_to(mod_all[l, b], (b, 6 * d)), mod_all[l, :b]], axis=1)
            .reshape(2 * b, 1, 6 * d) for l in range(depth)]

    lbs = jnp.cumsum(jax.nn.softmax(lb_logits.astype(F32), axis=0), axis=0)
    lbs = lbs - lbs[:1]
    cos_t, sin_t = _rope_tables(n_ctx, s)
    wrt_f = jnp.transpose(w_router).astype(F32)
    wrt_hi = wrt_f.astype(BF16)
    wrt = jnp.stack([wrt_hi, (wrt_f - wrt_hi.astype(F32)).astype(BF16)])
    l_all = n_ctx + s
    full2 = lambda shape: pl.BlockSpec(shape, lambda j, i: (0, 0))

    x_all = None
    out = None
    h = _prenorm(ctx, x, g_norm1[0], mods[0])
    for l in range(depth):
        last = l == depth - 1
        lam_init = 0.8 - 0.6 * math.exp(-0.3 * l)
        lam = (jnp.exp(jnp.sum(lambda_q1[l].astype(F32) * lambda_k1[l].astype(F32)))
               - jnp.exp(jnp.sum(lambda_q2[l].astype(F32) * lambda_k2[l].astype(F32))) + lam_init)
        w_bf = w_in[l].astype(BF16)
        (qk,) = _project(_proj_rope_kernel, h, w_bf, 0, 2, [cos_t, sin_t],
                         [full2((l_all, HEAD_W)), full2((l_all, HEAD_W))], [BF16], "proj_rope")
        (vq,) = _project(_proj_plain_kernel, h, w_bf, 2, 2, [], [], [BF16], "proj_v_rq")
        if l == 0:
            kk, gg = _project(_proj_forget_nobound_kernel, h, w_bf, 4, 2, [], [], [BF16, F32], "proj_forget0")
        else:
            kk, gg = _project(_proj_forget_kernel, h, w_bf, 4, 2, [lbs[l].reshape(1, d)],
                              [full2((1, d))], [BF16, F32], "proj_forget")
        (ri,) = _project(_proj_plain_kernel, h, w_bf, 6, 1, [], [], [BF16], "proj_ri")
        (gates,) = _project(_proj_gates_kernel, h, w_bf, 7, 3, [], [], [BF16], "proj_gates")

        a = _attention(qk, vq, lam, g_subln[l], lam_init, n_ctx, with_ctx=not last)
        r = _recurrence(vq, kk, gg, ri, gates, g_rec_norm[l], n_ctx, ctx_outputs=not last)

        res = (ctx, x) if l == 0 else (x_all,)
        xn, h2, logits_t = _merge(res, a, r, gates, mods[l], g_norm2[l], w_br_attn[l].astype(BF16),
                                  w_br_rec[l].astype(BF16), w_out[l].astype(BF16), wrt, n_ctx, with_ctx=not last)
        route = _routing(logits_t, b_router)
        moe_out = _moe(h2, route, w_gate[l].astype(BF16), w_up[l].astype(BF16), w_down[l].astype(BF16))
        if last:
            out = _final(xn, moe_out, mods[l], g_final)
        else:
            x_all, h = _post(xn, moe_out, mods[l], mods[l + 1], g_norm1[l + 1])
    return out
```

```python
import functools
import math

import numpy as np
import jax
import jax.numpy as jnp
from jax import lax
from jax.experimental import pallas as pl
from jax.experimental.pallas import tpu as pltpu

EPS = 1e-6
GRID_W = 64
ROPE_BASE = 10000.0
ATT_HEADS = 8
ATT_HEAD_DIM = 64
HEAD_W = 2 * ATT_HEAD_DIM
REC_HEADS = 8
N_EXPERTS = 16
N_GROUPS = 4
EXPERTS_PER_GROUP = N_EXPERTS // N_GROUPS
N_PAIR_CLASSES = N_GROUPS * 6
N_SPLITS = 10

LANES = 128
SUBLANES = 8
TOKEN_TILE = 256
PROJ_ROWS = 768
ATT_HEADS_PER_STEP = 4
GLA_CHUNK = 64
GLA_GROUP = 4
GLA_MIN_BCAST_BLOCK = 4
CAST_BLOCK_BYTES = 8 * 1024 * 1024
MOE_TILE = 256
VMEM_LIMIT = 56 * 1024 * 1024

F32 = jnp.float32
BF16 = jnp.bfloat16


def _sigmoid(x):
    return 1.0 / (1.0 + jnp.exp(-x))


def _cparams(sem):
    return pltpu.CompilerParams(dimension_semantics=sem, vmem_limit_bytes=VMEM_LIMIT)


def _mod_kernel(c_ref, w_ref, b_ref, o_ref):
    c = c_ref[...]
    s = c * _sigmoid(c)
    o_ref[...] = jnp.dot(s, w_ref[...], precision=lax.Precision.HIGHEST,
                         preferred_element_type=F32) + b_ref[...]


def _modulation(c_all, w_mod, b_mod):
    depth, d, n = w_mod.shape
    bp = c_all.shape[0]
    tn = 1024
    return pl.pallas_call(
        _mod_kernel,
        out_shape=jax.ShapeDtypeStruct((depth, bp, n), F32),
        grid=(depth, n // tn),
        in_specs=[pl.BlockSpec((bp, d), lambda l, j: (0, 0)),
                  pl.BlockSpec((None, d, tn), lambda l, j: (l, 0, j)),
                  pl.BlockSpec((None, 1, tn), lambda l, j: (l, 0, j))],
        out_specs=pl.BlockSpec((None, bp, tn), lambda l, j: (l, 0, j)),
        compiler_params=_cparams(("arbitrary", "arbitrary")),
        name="adaln_mod",
    )(c_all, w_mod, b_mod.reshape(depth, 1, n))


def _cast_kernel(x_ref, o_ref):
    o_ref[...] = x_ref[...].astype(o_ref.dtype)


def _to_bf16(w):
    shape = w.shape
    cols = shape[-1]
    rows = w.size // cols
    tr = rows
    while tr * cols * 4 > CAST_BLOCK_BYTES and tr % 16 == 0:
        tr //= 2
    out = pl.pallas_call(
        _cast_kernel,
        out_shape=jax.ShapeDtypeStruct((rows, cols), BF16),
        grid=(rows // tr,),
        in_specs=[pl.BlockSpec((tr, cols), lambda i: (i, 0))],
        out_specs=pl.BlockSpec((tr, cols), lambda i: (i, 0)),
        compiler_params=_cparams(("arbitrary",)),
        name="weights_to_bf16",
    )(w.reshape(rows, cols))
    return out.reshape(shape)


def _norm_mod(x, g, sh, sc):
    y = x * lax.rsqrt(jnp.mean(x * x, axis=-1, keepdims=True) + EPS) * g
    return y * (1.0 + sc) + sh


def _mod_spec(piece, d):
    return pl.BlockSpec((None, 1, d), lambda b, t: (2 * b + jnp.minimum(t, 1), 0, piece))


def _mod_spec_lat(piece, d):
    return pl.BlockSpec((None, 1, d), lambda b, t: (2 * b + 1, 0, piece))


def _prenorm_kernel(ctx_ref, x_ref, g_ref, sh_ref, sc_ref, h_ref):
    t = pl.program_id(1)

    def emit(xv):
        h_ref[...] = _norm_mod(xv, g_ref[...], sh_ref[...], sc_ref[...]).astype(BF16)

    @pl.when(t == 0)
    def _():
        emit(ctx_ref[...])

    @pl.when(t > 0)
    def _():
        emit(x_ref[...])


def _prenorm(ctx, x, g, mods):
    b, c, d = ctx.shape
    s = x.shape[1]
    tm = TOKEN_TILE
    nt = (c + s) // tm
    return pl.pallas_call(
        _prenorm_kernel,
        out_shape=jax.ShapeDtypeStruct((b, c + s, d), BF16),
        grid=(b, nt),
        in_specs=[pl.BlockSpec((None, tm, d), lambda i, t: (i, 0, 0)),
                  pl.BlockSpec((None, tm, d), lambda i, t: (i, jnp.maximum(t - 1, 0), 0)),
                  pl.BlockSpec((1, d), lambda i, t: (0, 0)),
                  _mod_spec(0, d), _mod_spec(1, d)],
        out_specs=pl.BlockSpec((None, tm, d), lambda i, t: (i, t, 0)),
        compiler_params=_cparams(("parallel", "arbitrary")),
        name="prenorm",
    )(ctx, x, g.reshape(1, d), mods, mods)


def _swap16(xs, first16):
    return jnp.where(first16, pltpu.roll(xs, LANES - 16, axis=1), pltpu.roll(xs, 16, axis=1))


def _proj_chunks(h_ref, w_ref, epilogue):
    n = h_ref.shape[0] // PROJ_ROWS
    rows = [slice(r * PROJ_ROWS, (r + 1) * PROJ_ROWS) for r in range(n)]
    mm = lambda r: jnp.dot(h_ref[rows[r], :], w_ref[...], preferred_element_type=F32)
    acc = mm(0)
    for r in range(n):
        nxt = mm(r + 1) if r + 1 < n else None
        epilogue(rows[r], acc)
        acc = nxt


def _proj_rope_kernel(h_ref, w_ref, cos_ref, sin_ref, o_ref):
    j = pl.program_id(0)
    scale = jnp.where(j == 0, ATT_HEAD_DIM ** -0.5 * math.log2(math.e), 1.0).astype(F32)
    lane = lax.broadcasted_iota(jnp.int32, (PROJ_ROWS, LANES), 1)
    first16 = (lane % 32) < 16

    def epilogue(rows, acc):
        cos = cos_ref[rows, :] * scale
        sin = sin_ref[rows, :] * scale
        for k in range(acc.shape[1] // LANES):
            xs = acc[:, k * LANES:(k + 1) * LANES]
            o_ref[rows, k * LANES:(k + 1) * LANES] = (xs * cos + _swap16(xs, first16) * sin).astype(o_ref.dtype)

    _proj_chunks(h_ref, w_ref, epilogue)


def _proj_plain_kernel(h_ref, w_ref, o_ref):
    def epilogue(rows, acc):
        o_ref[rows, :] = acc.astype(o_ref.dtype)

    _proj_chunks(h_ref, w_ref, epilogue)


def _proj_forget_kernel(h_ref, w_ref, lb_ref, k_ref, g_ref):
    lb = lb_ref[...]
    log_lb = jnp.log(lb)
    log_1m_lb = jnp.log1p(-lb)

    def epilogue(rows, z):
        log_sig = jnp.minimum(z, 0.0) - jnp.log(1.0 + jnp.exp(-jnp.abs(z)))
        bterm = log_1m_lb + log_sig
        mx = jnp.maximum(log_lb, bterm)
        mn = jnp.minimum(log_lb, bterm)
        log_f = mx + jnp.log(1.0 + jnp.exp(mn - mx))
        g_ref[rows, :] = log_f
        k_ref[rows, :] = (1.0 - jnp.exp(log_f)).astype(k_ref.dtype)

    _proj_chunks(h_ref, w_ref, epilogue)


def _proj_forget_nobound_kernel(h_ref, w_ref, k_ref, g_ref):
    def epilogue(rows, z):
        log_f = jnp.minimum(z, 0.0) - jnp.log(1.0 + jnp.exp(-jnp.abs(z)))
        g_ref[rows, :] = log_f
        k_ref[rows, :] = (1.0 - jnp.exp(log_f)).astype(k_ref.dtype)

    _proj_chunks(h_ref, w_ref, epilogue)


def _proj_gates_kernel(h_ref, w_ref, o_ref):
    j = pl.program_id(0)

    def epilogue(rows, z):
        sg = _sigmoid(z)
        o_ref[rows, :] = jnp.where(j == 0, z * sg, sg).astype(o_ref.dtype)

    _proj_chunks(h_ref, w_ref, epilogue)


def _project(kernel, h, w_bf, layer, first_split, n_splits, extra_inputs, extra_specs, out_dtypes, name):
    b, l, d = h.shape
    wcol = w_bf.shape[2] // N_SPLITS
    out_shape = [jax.ShapeDtypeStruct((b, l, n_splits * wcol), dt) for dt in out_dtypes]
    out_specs = [pl.BlockSpec((None, l, wcol), lambda j, i: (i, 0, j)) for _ in out_dtypes]
    res = pl.pallas_call(
        kernel,
        out_shape=out_shape,
        grid=(n_splits, b),
        in_specs=[pl.BlockSpec((None, l, d), lambda j, i: (i, 0, 0)),
                  pl.BlockSpec((None, d, wcol), lambda j, i: (layer, 0, first_split + j))] + extra_specs,
        out_specs=out_specs,
        compiler_params=_cparams(("arbitrary", "arbitrary")),
        name=name,
    )(h, w_bf, *extra_inputs)
    return res


def _attn_kernel(lam_ref, q_ref, k_ref, v_ref, g_ref, o_ref, *, n_ctx, ctx_tile, out_scale):
    lam = lam_ref[0]
    nt = (((1,), (1,)), ((), ()))
    heads = range(ATT_HEADS_PER_STEP)
    col = lambda hh: slice(hh * HEAD_W, (hh + 1) * HEAD_W)

    def attend(n_keys):
        scores = []
        for hh in heads:
            q = q_ref[:, col(hh)]
            k = k_ref[0:n_keys, col(hh)]
            lane = lax.broadcasted_iota(jnp.int32, q.shape, 1)
            zero = jnp.zeros_like(q)
            q1 = jnp.where(lane < ATT_HEAD_DIM, q, zero)
            q2 = jnp.where(lane < ATT_HEAD_DIM, zero, q)
            scores.append((lax.dot_general(q1, k, nt, preferred_element_type=F32),
                           lax.dot_general(q2, k, nt, preferred_element_type=F32)))
        for hh in heads:
            v = v_ref[0:n_keys, col(hh)]
            v1 = jnp.concatenate([v, jnp.ones_like(v)], axis=1)
            outs = []
            for sm in scores[hh]:
                p = jnp.exp2(sm - jnp.max(sm, axis=-1, keepdims=True))
                outs.append(jnp.dot(p.astype(BF16), v1, preferred_element_type=F32))
            (o1, o2) = outs
            l1, l2 = o1[:, HEAD_W:HEAD_W + 1], o2[:, HEAD_W:HEAD_W + 1]
            o = o1[:, :HEAD_W] * (1.0 / l1) - o2[:, :HEAD_W] * (lam / l2)
            o = o * lax.rsqrt(jnp.mean(o * o, axis=-1, keepdims=True) + EPS) * g_ref[...] * out_scale
            o_ref[:, col(hh)] = o.astype(o_ref.dtype)

    n_all = k_ref.shape[0]
    if ctx_tile:
        t = pl.program_id(2)

        @pl.when(t == 0)
        def _():
            attend(n_ctx)

        @pl.when(t > 0)
        def _():
            attend(n_all)
    else:
        attend(n_all)


def _attention(qk, vq, lam, g_subln, lam_init, n_ctx, with_ctx):
    b, l, _ = qk.shape
    hp = ATT_HEADS_PER_STEP
    ng = ATT_HEADS // hp
    tq = TOKEN_TILE
    off = 0 if with_ctx else n_ctx // tq
    n_out = l if with_ctx else l - n_ctx
    kern = functools.partial(_attn_kernel, n_ctx=n_ctx, ctx_tile=with_ctx, out_scale=1.0 - lam_init)
    return pl.pallas_call(
        kern,
        out_shape=jax.ShapeDtypeStruct((b, n_out, ATT_HEADS * HEAD_W), BF16),
        grid=(b, ng, n_out // tq),
        in_specs=[pl.BlockSpec(memory_space=pltpu.SMEM),
                  pl.BlockSpec((None, tq, hp * HEAD_W), lambda i, h, t: (i, t + off, h)),
                  pl.BlockSpec((None, l, hp * HEAD_W), lambda i, h, t: (i, 0, ng + h)),
                  pl.BlockSpec((None, l, hp * HEAD_W), lambda i, h, t: (i, 0, h)),
                  pl.BlockSpec((1, HEAD_W), lambda i, h, t: (0, 0))],
        out_specs=pl.BlockSpec((None, tq, hp * HEAD_W), lambda i, h, t: (i, t, h)),
        compiler_params=_cparams(("parallel", "arbitrary", "arbitrary")),
        name="diff_attention",
    )(lam.reshape(1), qk, qk, vq, g_subln.reshape(1, HEAD_W))


def _gla_tables(chunk):
    c = chunk
    levels = []
    b = c // 2
    while b >= 1:
        levels.append(b)
        b //= 2
    t = np.arange(c)
    u = np.arange(c)[None, :]

    def build(forward):
        mats, qrows, pmask = [], [], []
        tt = t[:, None]
        mats.append((u <= tt) if forward else (u >= tt))
        for b in levels:
            pair = t // (2 * b)
            late = (t % (2 * b)) >= b
            m = pair * 2 * b + b - 1
            if forward:
                is_q = late
                lo = np.where(is_q, m + 1, t + 1)[:, None]
                hi = np.where(is_q, t, m)[:, None]
            else:
                is_q = ~late
                lo = np.where(is_q, t, m + 1)[:, None]
                hi = np.where(is_q, m, t - 1)[:, None]
            if b < GLA_MIN_BCAST_BLOCK:
                mats.append((u >= lo) & (u <= hi))
            qrows.append(np.repeat(is_q[:, None], LANES, 1).astype(np.float32))
            pmask.append(((pair[:, None] == pair[None, :]) & is_q[:, None] & (~is_q)[None, :]).astype(np.float32))
        return np.concatenate(mats, 0).astype(np.float32), np.stack(qrows), np.stack(pmask)

    mf, qf, pf = build(True)
    mb, qb, pb = build(False)
    return levels, np.stack([mf, mb]), np.stack([qf, qb]), np.stack([pf, pb])


def _gla_kernel(q_ref, kf_ref, kb_ref, v_ref, gf_ref, gb_ref, gate_ref, gn_ref, m_ref, qrow_ref, pm_ref,
                o_ref, of_ref, ob_ref, sf_ref, sb_ref, *, levels, n_ctx_chunks, n_chunks, ctx_outputs):
    c = GLA_CHUNK
    u_n = GLA_GROUP
    n_levels = len(levels)
    sf_ref[...] = jnp.zeros_like(sf_ref)
    sb_ref[...] = jnp.zeros_like(sb_ref)
    tn = (((0,), (0,)), ((), ()))
    nt = (((1,), (1,)), ((), ()))

    def decays(direction, r0, g_ref):
        rows = [pl.ds(r0 + u * c, c) for u in range(u_n)]
        g = jnp.concatenate([g_ref[rw, :] for rw in rows], axis=1)
        g_hi = g.astype(BF16)
        r1 = g - g_hi.astype(F32)
        g_mid = r1.astype(BF16)
        g_lo = (r1 - g_mid.astype(F32)).astype(BF16)
        m = m_ref[direction]
        sums = (jnp.dot(m, g_hi, preferred_element_type=F32) + jnp.dot(m, g_mid, preferred_element_type=F32)
                + jnp.dot(m, g_lo, preferred_element_type=F32))
        run = sums[0:c, :]

        def to_row(blk, row):
            return -jnp.abs(blk - jnp.broadcast_to(run[row:row + 1, :], blk.shape))

        expo, n_direct = [], 0
        for b in levels:
            if b >= GLA_MIN_BCAST_BLOCK:
                parts = [to_row(run[p * 2 * b:(p + 1) * 2 * b, :], p * 2 * b + (b - 1 if direction == 0 else b))
                         for p in range(c // (2 * b))]
                expo.append(parts[0] if len(parts) == 1 else jnp.concatenate(parts, axis=0))
            else:
                n_direct += 1
                expo.append(sums[n_direct * c:(n_direct + 1) * c, :])
        expo.append(run)
        expo.append(to_row(run, c - 1 if direction == 0 else 0))
        return [jnp.exp(e) for e in expo]

    def group_pair(r0s, outputs=True):
        k_refs, g_refs, s_refs, out_refs = (kf_ref, kb_ref), (gf_ref, gb_ref), (sf_ref, sb_ref), (of_ref, ob_ref)
        dec = [decays(d, r0s[d], g_refs[d]) for d in (0, 1)]
        work = []
        for d in (0, 1):
            for u in range(u_n):
                rw = pl.ds(r0s[d] + u * c, c)
                k = k_refs[d][rw, :].astype(F32)
                v = v_ref[rw, :]
                du = [e[:, u * LANES:(u + 1) * LANES] for e in dec[d]]
                q_dec, k_dec = du[n_levels], du[n_levels + 1]
                w = dict(d=d, rw=rw, v=v, k_state=(k * k_dec).astype(BF16),
                         total=q_dec[c - 1:c, :] if d == 0 else q_dec[0:1, :])
                if outputs:
                    q = q_ref[rw, :].astype(F32)
                    level_prods = []
                    for lv in range(n_levels):
                        is_q = qrow_ref[d, lv] > 0.5
                        x = (jnp.where(is_q, q, k) * du[lv]).astype(BF16)
                        level_prods.append(lax.dot_general(x, x, nt, preferred_element_type=F32))
                    w.update(prods=level_prods, q_state=(q * q_dec).astype(BF16),
                             diag=jnp.sum(q * k, axis=-1, keepdims=True) * v.astype(F32))
                work.append(w)
        for w in work:
            if outputs:
                scores = sum(pm_ref[w["d"], lv] * w["prods"][lv] for lv in range(n_levels))
                w["intra"] = w["diag"] + jnp.dot(scores.astype(BF16), w["v"], preferred_element_type=F32)
            w["vk"] = lax.dot_general(w["v"], w["k_state"], tn, preferred_element_type=F32)
        for d in (0, 1):
            state_t = s_refs[d][...]
            mine = [w for w in work if w["d"] == d]
            for w in (mine if d == 0 else mine[::-1]):
                if outputs:
                    out_refs[d][w["rw"], :] = w["intra"] + lax.dot_general(
                        w["q_state"], state_t.astype(BF16), nt, preferred_element_type=F32)
                state_t = state_t * w["total"] + w["vk"]
            s_refs[d][...] = state_t

    n_ctx_groups = n_ctx_chunks // u_n
    n_groups = n_chunks // u_n

    def rows_of(i):
        top = n_ctx_chunks - i * u_n if i < n_ctx_groups else n_chunks - (i - n_ctx_groups) * u_n
        return i * u_n * c, (top - u_n) * c

    for i in range(n_groups):
        group_pair(rows_of(i), outputs=ctx_outputs or i >= n_ctx_groups)
    n_ctx_rows = n_ctx_chunks * c
    if ctx_outputs:
        out_rows = slice(0, n_chunks * c)
    else:
        out_rows = slice(n_ctx_rows, n_chunks * c)
        o_ref[0:n_ctx_rows, :] = jnp.zeros((n_ctx_rows, LANES), o_ref.dtype)
    o = of_ref[out_rows, :] + ob_ref[out_rows, :]
    o = o * lax.rsqrt(jnp.mean(o * o, axis=-1, keepdims=True) + EPS) * gn_ref[...]
    o_ref[out_rows, :] = (o * gate_ref[out_rows, :].astype(F32)).astype(o_ref.dtype)


def _recurrence(vq, kk, gg, ri, gates, g_rec, n_ctx, ctx_outputs):
    b, l, _ = vq.shape
    c = GLA_CHUNK
    levels, m_np, qrow_np, pm_np = _gla_tables(c)
    nh = REC_HEADS
    assert (n_ctx // c) % GLA_GROUP == 0 and ((l - n_ctx) // c) % GLA_GROUP == 0
    kern = functools.partial(_gla_kernel, levels=tuple(levels), n_ctx_chunks=n_ctx // c, n_chunks=l // c,
                             ctx_outputs=ctx_outputs)
    head = lambda off: pl.BlockSpec((None, l, LANES), lambda i, h: (i, 0, h + off))
    const = lambda a: pl.BlockSpec(a.shape, lambda i, h: (0,) * a.ndim)
    m_all = jnp.asarray(m_np, BF16)
    qrow = jnp.asarray(qrow_np, F32)
    pm = jnp.asarray(pm_np, F32)
    return pl.pallas_call(
        kern,
        out_shape=jax.ShapeDtypeStruct((b, l, nh * LANES), BF16),
        grid=(b, nh),
        in_specs=[head(nh), head(0), head(nh), head(0), head(0), head(nh), head(0),
                  pl.BlockSpec((1, LANES), lambda i, h: (0, 0)),
                  const(m_all), const(qrow), const(pm)],
        out_specs=head(0),
        scratch_shapes=[pltpu.VMEM((l, LANES), F32), pltpu.VMEM((l, LANES), F32),
                        pltpu.VMEM((LANES, LANES), F32), pltpu.VMEM((LANES, LANES), F32)],
        compiler_params=_cparams(("parallel", "arbitrary")),
        name="hgrn2_recurrence",
    )(vq, kk, kk, ri, gg, gg, gates, g_rec.reshape(1, LANES), m_all, qrow, pm)


def _route(logits, bias):
    scores = [_sigmoid(lg) for lg in logits]
    biased = [sc + bs for sc, bs in zip(scores, bias)]
    row = lambda a, e: a[e]
    epg = EXPERTS_PER_GROUP
    pairs = [(i, j) for i in range(epg) for j in range(i + 1, epg)]
    best_gs, grp = None, None
    for gi in range(N_GROUPS):
        gs = None
        for (i, j) in pairs:
            s = row(biased, gi * epg + i) + row(biased, gi * epg + j)
            gs = s if gs is None else jnp.maximum(gs, s)
        if best_gs is None:
            best_gs, grp = gs, jnp.zeros_like(gs, dtype=jnp.int32)
        else:
            better = gs > best_gs
            grp = jnp.where(better, gi, grp)
            best_gs = jnp.maximum(best_gs, gs)

    def in_group(a, j):
        out = row(a, j)
        for gi in range(1, N_GROUPS):
            out = jnp.where(grp == gi, row(a, gi * epg + j), out)
        return out

    u = [in_group(biased, j) for j in range(epg)]
    sc = [in_group(scores, j) for j in range(epg)]
    b1, i1 = u[0], jnp.zeros_like(grp)
    for j in range(1, epg):
        better = u[j] > b1
        i1 = jnp.where(better, j, i1)
        b1 = jnp.maximum(b1, u[j])
    i2 = jnp.where(i1 == 0, 1, 0)
    b2 = jnp.where(i1 == 0, u[1], u[0])
    for j in range(1, epg):
        better = (i1 != j) & (u[j] > b2)
        i2 = jnp.where(better, j, i2)
        b2 = jnp.where(better, u[j], b2)
    pick = lambda idx: sum(jnp.where(idx == j, sc[j], 0.0) for j in range(epg))
    w1, w2 = pick(i1), pick(i2)
    tot = w1 + w2
    w1, w2 = w1 / tot, w2 / tot
    la, lb2 = jnp.minimum(i1, i2), jnp.maximum(i1, i2)
    wa = jnp.where(i1 < i2, w1, w2)
    wb = jnp.where(i1 < i2, w2, w1)
    pair_idx = jnp.where(la == 0, lb2 - 1, jnp.where(la == 1, lb2 + 1, 5))
    cls = grp * 6 + pair_idx
    ea = grp * epg + la
    eb = grp * epg + lb2
    return [cls.astype(F32), ea.astype(F32), eb.astype(F32), wa, wb]


def _route_kernel(bias_ref, logits_ref, route_ref):
    rows = _route([logits_ref[e] for e in range(N_EXPERTS)], [bias_ref[e] for e in range(N_EXPERTS)])
    for k, r in enumerate(rows):
        route_ref[k] = r
    for k in range(len(rows), route_ref.shape[0]):
        route_ref[k] = jnp.zeros(route_ref.shape[1:], F32)


def _routing(logits_t, b_router):
    ne, n_tok = logits_t.shape
    w = n_tok // SUBLANES
    out = pl.pallas_call(
        _route_kernel,
        out_shape=jax.ShapeDtypeStruct((8, SUBLANES, w), F32),
        grid=(1,),
        in_specs=[pl.BlockSpec(memory_space=pltpu.SMEM),
                  pl.BlockSpec((ne, SUBLANES, w), lambda i: (0, 0, 0))],
        out_specs=pl.BlockSpec((8, SUBLANES, w), lambda i: (0, 0, 0)),
        compiler_params=_cparams(("arbitrary",)),
        name="route_top2",
    )(b_router.astype(F32), logits_t.reshape(ne, SUBLANES, w))
    return out.reshape(8, n_tok)


def _merge_kernel(*refs, two_sources):
    if two_sources:
        (ctx_ref, x_ref, a_ref, r_ref, ga_ref, gr_ref, gt_ref, sh_ref, sc_ref, g2_ref,
         wa_ref, wr_ref, wo_ref, wrt_ref, xn_ref, h2_ref, logit_ref) = refs
    else:
        (x_ref, a_ref, r_ref, ga_ref, gr_ref, gt_ref, sh_ref, sc_ref, g2_ref,
         wa_ref, wr_ref, wo_ref, wrt_ref, xn_ref, h2_ref, logit_ref) = refs

    def emit(xv):
        ua = jnp.dot(a_ref[...], wa_ref[...], preferred_element_type=F32)
        ur = jnp.dot(r_ref[...], wr_ref[...], preferred_element_type=F32)
        mixed = ga_ref[...].astype(F32) * ua + gr_ref[...].astype(F32) * ur
        y = jnp.dot(mixed.astype(BF16), wo_ref[...], preferred_element_type=F32)
        xn = xv + gt_ref[...] * y
        xn_ref[...] = xn
        h2 = _norm_mod(xn, g2_ref[...], sh_ref[...], sc_ref[...])
        h2_ref[...] = h2
        nt = (((1,), (1,)), ((), ()))
        h_hi = h2.astype(BF16)
        h_lo = (h2 - h_hi.astype(F32)).astype(BF16)
        w_hi = wrt_ref[0]
        w_lo = wrt_ref[1]
        logit_ref[...] = (lax.dot_general(w_hi, h_hi, nt, preferred_element_type=F32)
                          + lax.dot_general(w_lo, h_hi, nt, preferred_element_type=F32)
                          + lax.dot_general(w_hi, h_lo, nt, preferred_element_type=F32))

    if two_sources:
        t = pl.program_id(1)

        @pl.when(t == 0)
        def _():
            emit(ctx_ref[...])

        @pl.when(t > 0)
        def _():
            emit(x_ref[...])
    else:
        emit(x_ref[...])


def _merge(res_inputs, a, r, gates, mods, g2, wa, wr, wo, layer, wrt, n_ctx, with_ctx):
    b, l_a, d = a.shape
    l = gates.shape[1]
    tm = TOKEN_TILE
    off = 0 if with_ctx else n_ctx // tm
    nt = l // tm - off
    n_tok = b * nt * tm
    full = lambda shape: pl.BlockSpec(shape, lambda i, t: (0,) * len(shape))
    tile = lambda o: pl.BlockSpec((None, tm, d), lambda i, t: (i, t + o, 0))
    two_sources = len(res_inputs) == 2 and with_ctx
    if two_sources:
        res_specs = [pl.BlockSpec((None, tm, d), lambda i, t: (i, 0, 0)),
                     pl.BlockSpec((None, tm, d), lambda i, t: (i, jnp.maximum(t - 1, 0), 0))]
    elif len(res_inputs) == 2:
        res_inputs = res_inputs[1:]
        res_specs = [tile(0)]
    else:
        res_specs = [tile(off)]
    mspec = _mod_spec if with_ctx else _mod_spec_lat
    wspec = pl.BlockSpec((None, d, d), lambda i, t: (layer, 0, 0))
    a_off = off - (l - l_a) // tm
    kern = functools.partial(_merge_kernel, two_sources=two_sources)
    return pl.pallas_call(
        kern,
        out_shape=[jax.ShapeDtypeStruct((b, nt * tm, d), F32),
                   jax.ShapeDtypeStruct((n_tok, d), F32),
                   jax.ShapeDtypeStruct((N_EXPERTS, n_tok), F32)],
        grid=(b, nt),
        in_specs=res_specs + [
            tile(a_off), tile(off),
            pl.BlockSpec((None, tm, d), lambda i, t: (i, t + off, 1)),
            pl.BlockSpec((None, tm, d), lambda i, t: (i, t + off, 2)),
            mspec(2, d), mspec(3, d), mspec(4, d), full((1, d)),
            wspec, wspec, wspec, full((2, N_EXPERTS, d))],
        out_specs=[pl.BlockSpec((None, tm, d), lambda i, t: (i, t, 0)),
                   pl.BlockSpec((tm, d), lambda i, t: (i * nt + t, 0)),
                   pl.BlockSpec((N_EXPERTS, tm), lambda i, t: (0, i * nt + t))],
        compiler_params=_cparams(("parallel", "arbitrary")),
        name="merge_route",
    )(*res_inputs, a, r, gates, gates, mods, mods, mods, g2.reshape(1, d), wa, wr, wo, wrt)


def _moe_kernel(tile_a_ref, tile_b_ref, n_act_ref, src_ref, dst_ref, h_hbm, wts_ref,
                wga_ref, wua_ref, wda_ref, wgb_ref, wub_ref, wdb_ref, out_hbm,
                xbuf, ybuf, sem_in, sem_out, *, n_tok):
    i = pl.program_id(0)
    tm = MOE_TILE
    n_act = n_act_ref[0]
    slot = i % 2

    def row_copies(tile, sl, gather, start):
        base = tile * tm

        def body(j, c):
            r0 = j * SUBLANES
            for u in range(SUBLANES):
                if gather:
                    cp = pltpu.make_async_copy(h_hbm.at[pl.ds(src_ref[base + r0 + u], 1), :],
                                               xbuf.at[sl, j, pl.ds(u, 1), :], sem_in.at[sl])
                else:
                    cp = pltpu.make_async_copy(ybuf.at[sl, j, pl.ds(u, 1), :],
                                               out_hbm.at[pl.ds(dst_ref[base + r0 + u], 1), :], sem_out.at[sl])
                if start:
                    cp.start()
                else:
                    cp.wait()
            return c

        lax.fori_loop(0, tm // SUBLANES, body, 0)

    @pl.when(i == 0)
    def _():
        ybuf[0] = jnp.zeros(ybuf.shape[1:], ybuf.dtype)

        def pad_fill(j, start):
            cp = pltpu.make_async_copy(ybuf.at[0, j], out_hbm.at[pl.ds(n_tok + j * SUBLANES, SUBLANES), :],
                                       sem_out.at[0])
            cp.start() if start else cp.wait()

        pl.loop(0, tm // SUBLANES)(functools.partial(pad_fill, start=True))
        pl.loop(0, tm // SUBLANES)(functools.partial(pad_fill, start=False))

        @pl.when(n_act > 0)
        def _():
            row_copies(0, 0, gather=True, start=True)

    @pl.when(i < n_act)
    def _():
        row_copies(i, slot, gather=True, start=False)

        @pl.when(i + 1 < n_act)
        def _():
            row_copies(i + 1, 1 - slot, gather=True, start=True)

        x = xbuf[slot].reshape(tm, xbuf.shape[-1]).astype(BF16)

        def expert(wg_ref, wu_ref, wd_ref):
            gate = jnp.dot(x, wg_ref[...], preferred_element_type=F32)
            up = jnp.dot(x, wu_ref[...], preferred_element_type=F32)
            hid = (gate * _sigmoid(gate) * up).astype(BF16)
            return jnp.dot(hid, wd_ref[...], preferred_element_type=F32)

        wts = wts_ref[...]
        y = wts[:, 0:1] * expert(wga_ref, wua_ref, wda_ref) + wts[:, 1:2] * expert(wgb_ref, wub_ref, wdb_ref)

        @pl.when(i >= 2)
        def _():
            row_copies(i - 2, slot, gather=False, start=False)

        ybuf[slot] = y.reshape(ybuf.shape[1:])
        row_copies(i, slot, gather=False, start=True)

        @pl.when(i == n_act - 1)
        def _():
            row_copies(i, slot, gather=False, start=False)

            @pl.when(i >= 1)
            def _():
                row_copies(i - 1, 1 - slot, gather=False, start=False)


def _pair_class_experts():
    ea, eb = [], []
    for g in range(N_GROUPS):
        for i in range(EXPERTS_PER_GROUP):
            for j in range(i + 1, EXPERTS_PER_GROUP):
                ea.append(g * EXPERTS_PER_GROUP + i)
                eb.append(g * EXPERTS_PER_GROUP + j)
    return np.asarray(ea, np.int32), np.asarray(eb, np.int32)


def _moe_plan(route, n_tok):
    tm = MOE_TILE
    cls = route[0].astype(jnp.int32)
    onehot = (cls[:, None] == jnp.arange(N_PAIR_CLASSES, dtype=jnp.int32)[None, :]).astype(jnp.int32)
    csum = jnp.cumsum(onehot, axis=0)
    counts = csum[-1]
    padded = ((counts + tm - 1) // tm) * tm
    pends = jnp.cumsum(padded)
    pstarts = pends - padded
    pos = jnp.sum(onehot * (csum - 1 + pstarts[None, :]), axis=1)
    n_tiles = n_tok // tm + N_PAIR_CLASSES
    rows = jnp.arange(n_tiles * tm, dtype=jnp.int32)
    pad = jnp.stack([(n_tok + rows % tm).astype(F32), jnp.zeros_like(rows, F32), jnp.zeros_like(rows, F32)], axis=1)
    plan = pad.at[pos].set(jnp.stack([jnp.arange(n_tok, dtype=jnp.int32).astype(F32), route[3], route[4]], axis=1))
    tok = plan[:, 0].astype(jnp.int32)
    wts = plan[:, 1:3]
    tile_rows = rows[::tm]
    n_act = (pends[-1] // tm).reshape(1)
    tile_cls = jnp.sum((jnp.minimum(tile_rows, pends[-1] - 1)[:, None] >= pends[None, :]).astype(jnp.int32), axis=1)
    tile_cls = jnp.minimum(tile_cls, N_PAIR_CLASSES - 1)
    ea_np, eb_np = _pair_class_experts()
    ea = jnp.asarray(ea_np)[tile_cls]
    eb = jnp.asarray(eb_np)[tile_cls]
    return ea, eb, n_act, jnp.minimum(tok, n_tok - 1), tok, wts, n_tiles


def _moe(h2, route, wg, wu, wd, layer):
    n_tok, d = h2.shape
    de = wg.shape[2]
    e0 = layer * N_EXPERTS
    tm = MOE_TILE
    ea, eb, n_act, src, dst, wts, n_tiles = _moe_plan(route, n_tok)
    wspec_a = lambda r, c: pl.BlockSpec((None, r, c), lambda i, ta, tb, na, sr, ds: (e0 + ta[i], 0, 0))
    wspec_b = lambda r, c: pl.BlockSpec((None, r, c), lambda i, ta, tb, na, sr, ds: (e0 + tb[i], 0, 0))
    grid_spec = pltpu.PrefetchScalarGridSpec(
        num_scalar_prefetch=5,
        grid=(n_tiles,),
        in_specs=[pl.BlockSpec(memory_space=pl.ANY),
                  pl.BlockSpec((tm, 2), lambda i, ta, tb, na, sr, ds: (i, 0)),
                  wspec_a(d, de), wspec_a(d, de), wspec_a(de, d),
                  wspec_b(d, de), wspec_b(d, de), wspec_b(de, d)],
        out_specs=pl.BlockSpec(memory_space=pl.ANY),
        scratch_shapes=[pltpu.VMEM((2, tm // SUBLANES, SUBLANES, d), F32),
                        pltpu.VMEM((2, tm // SUBLANES, SUBLANES, d), F32),
                        pltpu.SemaphoreType.DMA((2,)), pltpu.SemaphoreType.DMA((2,))],
    )
    return pl.pallas_call(
        functools.partial(_moe_kernel, n_tok=n_tok),
        out_shape=jax.ShapeDtypeStruct((n_tok + tm, d), F32),
        grid_spec=grid_spec,
        compiler_params=pltpu.CompilerParams(dimension_semantics=("arbitrary",), vmem_limit_bytes=VMEM_LIMIT,
                                             has_side_effects=True),
        name="moe_pairs",
    )(ea, eb, n_act, src, dst, h2, wts, wg, wu, wd, wg, wu, wd)


def _post_kernel(x_ref, y_ref, gt_ref, g_ref, sh_ref, sc_ref, xo_ref, h_ref):
    xn = x_ref[...] + gt_ref[...] * y_ref[...]
    xo_ref[...] = xn
    h_ref[...] = _norm_mod(xn, g_ref[...], sh_ref[...], sc_ref[...]).astype(h_ref.dtype)


def _final_kernel(x_ref, y_ref, gt_ref, g_ref, o_ref):
    xn = x_ref[...] + gt_ref[...] * y_ref[...]
    o_ref[...] = xn * lax.rsqrt(jnp.mean(xn * xn, axis=-1, keepdims=True) + EPS) * g_ref[...]


def _post(x_all, moe_out, mods, mods_next, g_next):
    b, l, d = x_all.shape
    tm = TOKEN_TILE
    nt = l // tm
    tile = pl.BlockSpec((None, tm, d), lambda i, t: (i, t, 0))
    return pl.pallas_call(
        _post_kernel,
        out_shape=[jax.ShapeDtypeStruct((b, l, d), F32), jax.ShapeDtypeStruct((b, l, d), BF16)],
        grid=(b, nt),
        in_specs=[tile, pl.BlockSpec((tm, d), lambda i, t: (i * nt + t, 0)),
                  _mod_spec(5, d), pl.BlockSpec((1, d), lambda i, t: (0, 0)), _mod_spec(0, d), _mod_spec(1, d)],
        out_specs=[tile, tile],
        compiler_params=_cparams(("parallel", "arbitrary")),
        name="moe_residual_prenorm",
    )(x_all, moe_out, mods, g_next.reshape(1, d), mods_next, mods_next)


def _final(x_lat, moe_out, mods, g_final):
    b, s, d = x_lat.shape
    tm = TOKEN_TILE
    nt = s // tm
    tile = pl.BlockSpec((None, tm, d), lambda i, t: (i, t, 0))
    return pl.pallas_call(
        _final_kernel,
        out_shape=jax.ShapeDtypeStruct((b, s, d), F32),
        grid=(b, nt),
        in_specs=[tile, pl.BlockSpec((tm, d), lambda i, t: (i * nt + t, 0)),
                  _mod_spec_lat(5, d), pl.BlockSpec((1, d), lambda i, t: (0, 0))],
        out_specs=tile,
        compiler_params=_cparams(("parallel", "arbitrary")),
        name="moe_residual_final_norm",
    )(x_lat, moe_out, mods, g_final.reshape(1, d))


def _rope_tables(n_ctx, n_lat):
    t = np.arange(n_lat)
    n_freq = ATT_HEAD_DIM // 4
    inv = (ROPE_BASE ** (-np.arange(n_freq, dtype=np.float32) / n_freq)).astype(np.float32)
    ang_r = (t // GRID_W).astype(np.float32)[:, None] * inv
    ang_c = (t % GRID_W).astype(np.float32)[:, None] * inv
    ang = np.concatenate([ang_r, ang_r, ang_c, ang_c], axis=1)
    ang = np.concatenate([ang, ang], axis=1)
    sign = np.where((np.arange(HEAD_W) % 32) < 16, -1.0, 1.0).astype(np.float32)
    cos = np.concatenate([np.ones((n_ctx, HEAD_W), np.float32), np.cos(ang)], axis=0)
    sin = np.concatenate([np.zeros((n_ctx, HEAD_W), np.float32), np.sin(ang) * sign], axis=0)
    return jnp.asarray(cos, F32), jnp.asarray(sin, F32)


def kernel(x, c, ctx, c_ctx, w_mod, b_mod, g_norm1, g_norm2, w_in, lambda_q1, lambda_k1, lambda_q2, lambda_k2,
           g_subln, lb_logits, g_rec_norm, w_br_attn, w_br_rec, w_out, w_router, b_router, w_gate, w_up, w_down,
           g_final):
    b, s, d = x.shape
    n_ctx = ctx.shape[1]
    depth = w_in.shape[0]
    assert n_ctx == TOKEN_TILE and s % TOKEN_TILE == 0 and s % GRID_W == 0
    assert w_in.shape[2] == N_SPLITS * d

    bp = -(-(b + 1) // 8) * 8
    c_all = jnp.concatenate([c, c_ctx[None], jnp.zeros((bp - b - 1, d), F32)], axis=0)
    mod_all = _modulation(c_all, w_mod, b_mod)
    mods = [jnp.stack([jnp.broadcast_to(mod_all[l, b], (b, 6 * d)), mod_all[l, :b]], axis=1)
            .reshape(2 * b, 1, 6 * d) for l in range(depth)]

    lbs = jnp.cumsum(jax.nn.softmax(lb_logits.astype(F32), axis=0), axis=0)
    lbs = lbs - lbs[:1]
    cos_t, sin_t = _rope_tables(n_ctx, s)
    wrt_f = jnp.transpose(w_router).astype(F32)
    wrt_hi = wrt_f.astype(BF16)
    wrt = jnp.stack([wrt_hi, (wrt_f - wrt_hi.astype(F32)).astype(BF16)])
    l_all = n_ctx + s
    full2 = lambda shape: pl.BlockSpec(shape, lambda j, i: (0, 0))

    w_in_bf, w_br_a_bf, w_br_r_bf, w_out_bf = (_to_bf16(w) for w in (w_in, w_br_attn, w_br_rec, w_out))
    w_gate_bf, w_up_bf, w_down_bf = (_to_bf16(w).reshape((-1,) + w.shape[2:]) for w in (w_gate, w_up, w_down))
    x_all = None
    out = None
    h = _prenorm(ctx, x, g_norm1[0], mods[0])
    for l in range(depth):
        last = l == depth - 1
        lam_init = 0.8 - 0.6 * math.exp(-0.3 * l)
        lam = (jnp.exp(jnp.sum(lambda_q1[l].astype(F32) * lambda_k1[l].astype(F32)))
               - jnp.exp(jnp.sum(lambda_q2[l].astype(F32) * lambda_k2[l].astype(F32))) + lam_init)
        (qk,) = _project(_proj_rope_kernel, h, w_in_bf, l, 0, 2, [cos_t, sin_t],
                         [full2((l_all, HEAD_W)), full2((l_all, HEAD_W))], [BF16], "proj_rope")
        (vq,) = _project(_proj_plain_kernel, h, w_in_bf, l, 2, 2, [], [], [BF16], "proj_v_rq")
        if l == 0:
            kk, gg = _project(_proj_forget_nobound_kernel, h, w_in_bf, l, 4, 2, [], [], [BF16, F32], "proj_forget0")
        else:
            kk, gg = _project(_proj_forget_kernel, h, w_in_bf, l, 4, 2, [lbs[l].reshape(1, d)],
                              [full2((1, d))], [BF16, F32], "proj_forget")
        (ri,) = _project(_proj_plain_kernel, h, w_in_bf, l, 6, 1, [], [], [BF16], "proj_ri")
        (gates,) = _project(_proj_gates_kernel, h, w_in_bf, l, 7, 3, [], [], [BF16], "proj_gates")

        a = _attention(qk, vq, lam, g_subln[l], lam_init, n_ctx, with_ctx=not last)
        r = _recurrence(vq, kk, gg, ri, gates, g_rec_norm[l], n_ctx, ctx_outputs=not last)

        res = (ctx, x) if l == 0 else (x_all,)
        xn, h2, logits_t = _merge(res, a, r, gates, mods[l], g_norm2[l], w_br_a_bf, w_br_r_bf, w_out_bf, l,
                                  wrt, n_ctx, with_ctx=not last)
        route = _routing(logits_t, b_router)
        moe_out = _moe(h2, route, w_gate_bf, w_up_bf, w_down_bf, l)
        if last:
            out = _final(xn, moe_out, mods[l], g_final)
        else:
            x_all, h = _post(xn, moe_out, mods[l], mods[l + 1], g_norm1[l + 1])
    return out
```

```python
import functools
import math

import numpy as np
import jax
import jax.numpy as jnp
from jax import lax
from jax.experimental import pallas as pl
from jax.experimental.pallas import tpu as pltpu

EPS = 1e-6
GRID_W = 64
ROPE_BASE = 10000.0
ATT_HEADS = 8
ATT_HEAD_DIM = 64
HEAD_W = 2 * ATT_HEAD_DIM
REC_HEADS = 8
N_EXPERTS = 16
N_GROUPS = 4
EXPERTS_PER_GROUP = N_EXPERTS // N_GROUPS
N_PAIR_CLASSES = N_GROUPS * 6
N_SPLITS = 10

LANES = 128
SUBLANES = 8
TOKEN_TILE = 256
PROJ_ROWS = 768
ATT_HEADS_PER_STEP = 4
GLA_CHUNK = 64
GLA_GROUP = 4
GLA_MIN_BCAST_BLOCK = 4
CAST_BLOCK_BYTES = 8 * 1024 * 1024
MOE_TILE = 256
VMEM_LIMIT = 56 * 1024 * 1024

F32 = jnp.float32
BF16 = jnp.bfloat16
LOG2_E = math.log2(math.e)


def _sigmoid(x):
    return 1.0 / (1.0 + jnp.exp(-x))


def _cparams(sem):
    return pltpu.CompilerParams(dimension_semantics=sem, vmem_limit_bytes=VMEM_LIMIT)


def _mod_kernel(c_ref, w_ref, b_ref, o_ref):
    c = c_ref[...]
    s = c * _sigmoid(c)
    o_ref[...] = jnp.dot(s, w_ref[...], precision=lax.Precision.HIGHEST,
                         preferred_element_type=F32) + b_ref[...]


def _modulation(c_all, w_mod, b_mod):
    depth, d, n = w_mod.shape
    bp = c_all.shape[0]
    tn = 1024
    return pl.pallas_call(
        _mod_kernel,
        out_shape=jax.ShapeDtypeStruct((depth, bp, n), F32),
        grid=(depth, n // tn),
        in_specs=[pl.BlockSpec((bp, d), lambda l, j: (0, 0)),
                  pl.BlockSpec((None, d, tn), lambda l, j: (l, 0, j)),
                  pl.BlockSpec((None, 1, tn), lambda l, j: (l, 0, j))],
        out_specs=pl.BlockSpec((None, bp, tn), lambda l, j: (l, 0, j)),
        compiler_params=_cparams(("arbitrary", "arbitrary")),
        name="adaln_mod",
    )(c_all, w_mod, b_mod.reshape(depth, 1, n))


def _cast_kernel(x_ref, o_ref):
    o_ref[...] = x_ref[...].astype(o_ref.dtype)


def _to_bf16(w):
    shape = w.shape
    cols = shape[-1]
    rows = w.size // cols
    tr = rows
    while tr * cols * 4 > CAST_BLOCK_BYTES and tr % 16 == 0:
        tr //= 2
    out = pl.pallas_call(
        _cast_kernel,
        out_shape=jax.ShapeDtypeStruct((rows, cols), BF16),
        grid=(rows // tr,),
        in_specs=[pl.BlockSpec((tr, cols), lambda i: (i, 0))],
        out_specs=pl.BlockSpec((tr, cols), lambda i: (i, 0)),
        compiler_params=_cparams(("arbitrary",)),
        name="weights_to_bf16",
    )(w.reshape(rows, cols))
    return out.reshape(shape)


def _norm_mod(x, g, sh, sc):
    y = x * lax.rsqrt(jnp.mean(x * x, axis=-1, keepdims=True) + EPS) * g
    return y * (1.0 + sc) + sh


def _mod_spec(piece, d):
    return pl.BlockSpec((None, 1, d), lambda b, t: (2 * b + jnp.minimum(t, 1), 0, piece))


def _mod_spec_lat(piece, d):
    return pl.BlockSpec((None, 1, d), lambda b, t: (2 * b + 1, 0, piece))


def _prenorm_kernel(ctx_ref, x_ref, g_ref, sh_ref, sc_ref, h_ref):
    t = pl.program_id(1)

    def emit(xv):
        h_ref[...] = _norm_mod(xv, g_ref[...], sh_ref[...], sc_ref[...]).astype(BF16)

    @pl.when(t == 0)
    def _():
        emit(ctx_ref[...])

    @pl.when(t > 0)
    def _():
        emit(x_ref[...])


def _prenorm(ctx, x, g, mods):
    b, c, d = ctx.shape
    s = x.shape[1]
    tm = TOKEN_TILE
    nt = (c + s) // tm
    return pl.pallas_call(
        _prenorm_kernel,
        out_shape=jax.ShapeDtypeStruct((b, c + s, d), BF16),
        grid=(b, nt),
        in_specs=[pl.BlockSpec((None, tm, d), lambda i, t: (i, 0, 0)),
                  pl.BlockSpec((None, tm, d), lambda i, t: (i, jnp.maximum(t - 1, 0), 0)),
                  pl.BlockSpec((1, d), lambda i, t: (0, 0)),
                  _mod_spec(0, d), _mod_spec(1, d)],
        out_specs=pl.BlockSpec((None, tm, d), lambda i, t: (i, t, 0)),
        compiler_params=_cparams(("parallel", "arbitrary")),
        name="prenorm",
    )(ctx, x, g.reshape(1, d), mods, mods)


def _swap16(xs, first16):
    return jnp.where(first16, pltpu.roll(xs, LANES - 16, axis=1), pltpu.roll(xs, 16, axis=1))


def _proj_chunks(h_ref, w_ref, epilogue):
    n = h_ref.shape[0] // PROJ_ROWS
    rows = [slice(r * PROJ_ROWS, (r + 1) * PROJ_ROWS) for r in range(n)]
    mm = lambda r: jnp.dot(h_ref[rows[r], :], w_ref[...], preferred_element_type=F32)
    acc = mm(0)
    for r in range(n):
        nxt = mm(r + 1) if r + 1 < n else None
        epilogue(rows[r], acc)
        acc = nxt


def _proj_rope_kernel(h_ref, w_ref, cos_ref, sin_ref, o_ref):
    j = pl.program_id(0)
    scale = jnp.where(j == 0, ATT_HEAD_DIM ** -0.5 * math.log2(math.e), 1.0).astype(F32)
    lane = lax.broadcasted_iota(jnp.int32, (PROJ_ROWS, LANES), 1)
    first16 = (lane % 32) < 16

    def epilogue(rows, acc):
        cos = cos_ref[rows, :] * scale
        sin = sin_ref[rows, :] * scale
        for k in range(acc.shape[1] // LANES):
            xs = acc[:, k * LANES:(k + 1) * LANES]
            o_ref[rows, k * LANES:(k + 1) * LANES] = (xs * cos + _swap16(xs, first16) * sin).astype(o_ref.dtype)

    _proj_chunks(h_ref, w_ref, epilogue)


def _proj_plain_kernel(h_ref, w_ref, o_ref):
    def epilogue(rows, acc):
        o_ref[rows, :] = acc.astype(o_ref.dtype)

    _proj_chunks(h_ref, w_ref, epilogue)


def _proj_forget_kernel(h_ref, w_ref, lb_ref, k_ref, g_ref):
    lb = lb_ref[...]

    def epilogue(rows, z):
        e = jnp.exp(-jnp.abs(z))
        sig = jnp.where(z >= 0.0, 1.0, e) / (1.0 + e)
        f = lb + (1.0 - lb) * sig
        g_ref[rows, :] = jnp.where(f > 0.0, jnp.log(f), z)
        k_ref[rows, :] = (1.0 - f).astype(k_ref.dtype)

    _proj_chunks(h_ref, w_ref, epilogue)


def _proj_forget_nobound_kernel(h_ref, w_ref, k_ref, g_ref):
    def epilogue(rows, z):
        log_f = jnp.minimum(z, 0.0) - jnp.log(1.0 + jnp.exp(-jnp.abs(z)))
        g_ref[rows, :] = log_f
        k_ref[rows, :] = (1.0 - jnp.exp(log_f)).astype(k_ref.dtype)

    _proj_chunks(h_ref, w_ref, epilogue)


def _proj_gates_kernel(h_ref, w_ref, o_ref):
    j = pl.program_id(0)

    def epilogue(rows, z):
        sg = _sigmoid(z)
        o_ref[rows, :] = jnp.where(j == 0, z * sg, sg).astype(o_ref.dtype)

    _proj_chunks(h_ref, w_ref, epilogue)


def _project(kernel, h, w_bf, layer, first_split, n_splits, extra_inputs, extra_specs, out_dtypes, name):
    b, l, d = h.shape
    wcol = w_bf.shape[2] // N_SPLITS
    out_shape = [jax.ShapeDtypeStruct((b, l, n_splits * wcol), dt) for dt in out_dtypes]
    out_specs = [pl.BlockSpec((None, l, wcol), lambda j, i: (i, 0, j)) for _ in out_dtypes]
    res = pl.pallas_call(
        kernel,
        out_shape=out_shape,
        grid=(n_splits, b),
        in_specs=[pl.BlockSpec((None, l, d), lambda j, i: (i, 0, 0)),
                  pl.BlockSpec((None, d, wcol), lambda j, i: (layer, 0, first_split + j))] + extra_specs,
        out_specs=out_specs,
        compiler_params=_cparams(("arbitrary", "arbitrary")),
        name=name,
    )(h, w_bf, *extra_inputs)
    return res


def _attn_kernel(lam_ref, q_ref, k_ref, v_ref, g_ref, o_ref, *, n_ctx, ctx_tile, out_scale):
    lam = lam_ref[0]
    nt = (((1,), (1,)), ((), ()))
    heads = range(ATT_HEADS_PER_STEP)
    col = lambda hh: slice(hh * HEAD_W, (hh + 1) * HEAD_W)

    def attend(n_keys):
        scores = []
        for hh in heads:
            q = q_ref[:, col(hh)]
            k = k_ref[0:n_keys, col(hh)]
            lane = lax.broadcasted_iota(jnp.int32, q.shape, 1)
            zero = jnp.zeros_like(q)
            q1 = jnp.where(lane < ATT_HEAD_DIM, q, zero)
            q2 = jnp.where(lane < ATT_HEAD_DIM, zero, q)
            scores.append((lax.dot_general(q1, k, nt, preferred_element_type=F32),
                           lax.dot_general(q2, k, nt, preferred_element_type=F32)))
        for hh in heads:
            v = v_ref[0:n_keys, col(hh)]
            v1 = jnp.concatenate([v, jnp.ones_like(v)], axis=1)
            outs = []
            for sm in scores[hh]:
                p = jnp.exp2(sm - jnp.max(sm, axis=-1, keepdims=True))
                outs.append(jnp.dot(p.astype(BF16), v1, preferred_element_type=F32))
            (o1, o2) = outs
            l1, l2 = o1[:, HEAD_W:HEAD_W + 1], o2[:, HEAD_W:HEAD_W + 1]
            o = o1[:, :HEAD_W] * (1.0 / l1) - o2[:, :HEAD_W] * (lam / l2)
            o = o * lax.rsqrt(jnp.mean(o * o, axis=-1, keepdims=True) + EPS) * g_ref[...] * out_scale
            o_ref[:, col(hh)] = o.astype(o_ref.dtype)

    n_all = k_ref.shape[0]
    if ctx_tile:
        t = pl.program_id(2)

        @pl.when(t == 0)
        def _():
            attend(n_ctx)

        @pl.when(t > 0)
        def _():
            attend(n_all)
    else:
        attend(n_all)


def _attention(qk, vq, lam, g_subln, lam_init, n_ctx, with_ctx):
    b, l, _ = qk.shape
    hp = ATT_HEADS_PER_STEP
    ng = ATT_HEADS // hp
    tq = TOKEN_TILE
    off = 0 if with_ctx else n_ctx // tq
    n_out = l if with_ctx else l - n_ctx
    kern = functools.partial(_attn_kernel, n_ctx=n_ctx, ctx_tile=with_ctx, out_scale=1.0 - lam_init)
    return pl.pallas_call(
        kern,
        out_shape=jax.ShapeDtypeStruct((b, n_out, ATT_HEADS * HEAD_W), BF16),
        grid=(b, ng, n_out // tq),
        in_specs=[pl.BlockSpec(memory_space=pltpu.SMEM),
                  pl.BlockSpec((None, tq, hp * HEAD_W), lambda i, h, t: (i, t + off, h)),
                  pl.BlockSpec((None, l, hp * HEAD_W), lambda i, h, t: (i, 0, ng + h)),
                  pl.BlockSpec((None, l, hp * HEAD_W), lambda i, h, t: (i, 0, h)),
                  pl.BlockSpec((1, HEAD_W), lambda i, h, t: (0, 0))],
        out_specs=pl.BlockSpec((None, tq, hp * HEAD_W), lambda i, h, t: (i, t, h)),
        compiler_params=_cparams(("parallel", "arbitrary", "arbitrary")),
        name="diff_attention",
    )(lam.reshape(1), qk, qk, vq, g_subln.reshape(1, HEAD_W))


def _gla_tables(chunk):
    c = chunk
    levels = []
    b = c // 2
    while b >= 1:
        levels.append(b)
        b //= 2
    t = np.arange(c)
    u = np.arange(c)[None, :]

    def build(forward):
        mats, qrows, pmask = [], [], []
        tt = t[:, None]
        mats.append((u <= tt) if forward else (u >= tt))
        for b in levels:
            pair = t // (2 * b)
            late = (t % (2 * b)) >= b
            m = pair * 2 * b + b - 1
            if forward:
                is_q = late
                lo = np.where(is_q, m + 1, t + 1)[:, None]
                hi = np.where(is_q, t, m)[:, None]
            else:
                is_q = ~late
                lo = np.where(is_q, t, m + 1)[:, None]
                hi = np.where(is_q, m, t - 1)[:, None]
            if b < GLA_MIN_BCAST_BLOCK:
                mats.append((u >= lo) & (u <= hi))
            qrows.append(np.repeat(is_q[:, None], LANES, 1).astype(np.float32))
            pmask.append(((pair[:, None] == pair[None, :]) & is_q[:, None] & (~is_q)[None, :]).astype(np.float32))
        return np.concatenate(mats, 0).astype(np.float32), np.stack(qrows), np.stack(pmask)

    mf, qf, pf = build(True)
    mb, qb, pb = build(False)
    return levels, np.stack([mf, mb]), np.stack([qf, qb]), np.stack([pf, pb])


def _gla_kernel(q_ref, kf_ref, kb_ref, v_ref, gf_ref, gb_ref, gate_ref, gn_ref, m_ref, qrow_ref, pm_ref,
                o_ref, of_ref, ob_ref, sf_ref, sb_ref, *, levels, n_ctx_chunks, n_chunks, ctx_outputs):
    c = GLA_CHUNK
    u_n = GLA_GROUP
    n_levels = len(levels)
    sf_ref[...] = jnp.zeros_like(sf_ref)
    sb_ref[...] = jnp.zeros_like(sb_ref)
    tn = (((0,), (0,)), ((), ()))
    nt = (((1,), (1,)), ((), ()))

    def decays(direction, r0, g_ref):
        rows = [pl.ds(r0 + u * c, c) for u in range(u_n)]
        g = jnp.concatenate([g_ref[rw, :] for rw in rows], axis=1)
        g_hi = g.astype(BF16)
        r1 = g - g_hi.astype(F32)
        g_mid = r1.astype(BF16)
        g_lo = (r1 - g_mid.astype(F32)).astype(BF16)
        m = m_ref[direction]
        sums = (jnp.dot(m, g_hi, preferred_element_type=F32) + jnp.dot(m, g_mid, preferred_element_type=F32)
                + jnp.dot(m, g_lo, preferred_element_type=F32))
        run = sums[0:c, :]

        def to_row(blk, row):
            return jnp.exp2(jnp.abs(blk - jnp.broadcast_to(run[row:row + 1, :], blk.shape)) * (-LOG2_E))

        dec, n_direct = [], 0
        for b in levels:
            if b >= GLA_MIN_BCAST_BLOCK:
                parts = [to_row(run[p * 2 * b:(p + 1) * 2 * b, :], p * 2 * b + (b - 1 if direction == 0 else b))
                         for p in range(c // (2 * b))]
                dec.append(parts[0] if len(parts) == 1 else jnp.concatenate(parts, axis=0))
            else:
                n_direct += 1
                dec.append(jnp.exp(sums[n_direct * c:(n_direct + 1) * c, :]))
        dec.append(jnp.exp(run))
        dec.append(to_row(run, c - 1 if direction == 0 else 0))
        return dec

    def group_pair(r0s, outputs=True):
        k_refs, g_refs, s_refs, out_refs = (kf_ref, kb_ref), (gf_ref, gb_ref), (sf_ref, sb_ref), (of_ref, ob_ref)
        dec = [decays(d, r0s[d], g_refs[d]) for d in (0, 1)]
        work = []
        for d in (0, 1):
            for u in range(u_n):
                rw = pl.ds(r0s[d] + u * c, c)
                k = k_refs[d][rw, :].astype(F32)
                v = v_ref[rw, :]
                du = [e[:, u * LANES:(u + 1) * LANES] for e in dec[d]]
                q_dec, k_dec = du[n_levels], du[n_levels + 1]
                w = dict(d=d, rw=rw, v=v, k_state=(k * k_dec).astype(BF16),
                         total=q_dec[c - 1:c, :] if d == 0 else q_dec[0:1, :])
                if outputs:
                    q = q_ref[rw, :].astype(F32)
                    level_prods = []
                    for lv, b in enumerate(levels):
                        if b >= SUBLANES:
                            qk = jnp.concatenate([(q if ((r // b) % 2 == 1) == (d == 0) else k)[r:r + b, :]
                                                  for r in range(0, c, b)], axis=0)
                        else:
                            qk = jnp.where(qrow_ref[d, lv] > 0.5, q, k)
                        x = (qk * du[lv]).astype(BF16)
                        level_prods.append(lax.dot_general(x, x, nt, preferred_element_type=F32))
                    w.update(prods=level_prods, q_state=(q * q_dec).astype(BF16),
                             diag=jnp.sum(q * k, axis=-1, keepdims=True) * v.astype(F32))
                work.append(w)
        for w in work:
            if outputs:
                scores = jnp.zeros((c, c), F32)
                for lv in range(n_levels):
                    scores = jnp.where(pm_ref[w["d"], lv] > 0.5, w["prods"][lv], scores)
                w["intra"] = w["diag"] + jnp.dot(scores.astype(BF16), w["v"], preferred_element_type=F32)
            w["vk"] = lax.dot_general(w["v"], w["k_state"], tn, preferred_element_type=F32)
        for d in (0, 1):
            state_t = s_refs[d][...]
            mine = [w for w in work if w["d"] == d]
            for w in (mine if d == 0 else mine[::-1]):
                if outputs:
                    out_refs[d][w["rw"], :] = w["intra"] + lax.dot_general(
                        w["q_state"], state_t.astype(BF16), nt, preferred_element_type=F32)
                state_t = state_t * w["total"] + w["vk"]
            s_refs[d][...] = state_t

    n_ctx_groups = n_ctx_chunks // u_n
    n_groups = n_chunks // u_n

    def rows_of(i):
        top = n_ctx_chunks - i * u_n if i < n_ctx_groups else n_chunks - (i - n_ctx_groups) * u_n
        return i * u_n * c, (top - u_n) * c

    for i in range(n_groups):
        group_pair(rows_of(i), outputs=ctx_outputs or i >= n_ctx_groups)
    n_ctx_rows = n_ctx_chunks * c
    if ctx_outputs:
        out_rows = slice(0, n_chunks * c)
    else:
        out_rows = slice(n_ctx_rows, n_chunks * c)
        o_ref[0:n_ctx_rows, :] = jnp.zeros((n_ctx_rows, LANES), o_ref.dtype)
    o = of_ref[out_rows, :] + ob_ref[out_rows, :]
    o = o * lax.rsqrt(jnp.mean(o * o, axis=-1, keepdims=True) + EPS) * gn_ref[...]
    o_ref[out_rows, :] = (o * gate_ref[out_rows, :].astype(F32)).astype(o_ref.dtype)


def _recurrence(vq, kk, gg, ri, gates, g_rec, n_ctx, ctx_outputs):
    b, l, _ = vq.shape
    c = GLA_CHUNK
    levels, m_np, qrow_np, pm_np = _gla_tables(c)
    nh = REC_HEADS
    assert (n_ctx // c) % GLA_GROUP == 0 and ((l - n_ctx) // c) % GLA_GROUP == 0
    kern = functools.partial(_gla_kernel, levels=tuple(levels), n_ctx_chunks=n_ctx // c, n_chunks=l // c,
                             ctx_outputs=ctx_outputs)
    head = lambda off: pl.BlockSpec((None, l, LANES), lambda i, h: (i, 0, h + off))
    const = lambda a: pl.BlockSpec(a.shape, lambda i, h: (0,) * a.ndim)
    m_all = jnp.asarray(m_np, BF16)
    qrow = jnp.asarray(qrow_np, F32)
    pm = jnp.asarray(pm_np, F32)
    return pl.pallas_call(
        kern,
        out_shape=jax.ShapeDtypeStruct((b, l, nh * LANES), BF16),
        grid=(b, nh),
        in_specs=[head(nh), head(0), head(nh), head(0), head(0), head(nh), head(0),
                  pl.BlockSpec((1, LANES), lambda i, h: (0, 0)),
                  const(m_all), const(qrow), const(pm)],
        out_specs=head(0),
        scratch_shapes=[pltpu.VMEM((l, LANES), F32), pltpu.VMEM((l, LANES), F32),
                        pltpu.VMEM((LANES, LANES), F32), pltpu.VMEM((LANES, LANES), F32)],
        compiler_params=_cparams(("parallel", "arbitrary")),
        name="hgrn2_recurrence",
    )(vq, kk, kk, ri, gg, gg, gates, g_rec.reshape(1, LANES), m_all, qrow, pm)


def _route(logits, bias):
    scores = [_sigmoid(lg) for lg in logits]
    biased = [sc + bs for sc, bs in zip(scores, bias)]
    row = lambda a, e: a[e]
    epg = EXPERTS_PER_GROUP
    pairs = [(i, j) for i in range(epg) for j in range(i + 1, epg)]
    best_gs, grp = None, None
    for gi in range(N_GROUPS):
        gs = None
        for (i, j) in pairs:
            s = row(biased, gi * epg + i) + row(biased, gi * epg + j)
            gs = s if gs is None else jnp.maximum(gs, s)
        if best_gs is None:
            best_gs, grp = gs, jnp.zeros_like(gs, dtype=jnp.int32)
        else:
            better = gs > best_gs
            grp = jnp.where(better, gi, grp)
            best_gs = jnp.maximum(best_gs, gs)

    def in_group(a, j):
        out = row(a, j)
        for gi in range(1, N_GROUPS):
            out = jnp.where(grp == gi, row(a, gi * epg + j), out)
        return out

    u = [in_group(biased, j) for j in range(epg)]
    sc = [in_group(scores, j) for j in range(epg)]
    b1, i1 = u[0], jnp.zeros_like(grp)
    for j in range(1, epg):
        better = u[j] > b1
        i1 = jnp.where(better, j, i1)
        b1 = jnp.maximum(b1, u[j])
    i2 = jnp.where(i1 == 0, 1, 0)
    b2 = jnp.where(i1 == 0, u[1], u[0])
    for j in range(1, epg):
        better = (i1 != j) & (u[j] > b2)
        i2 = jnp.where(better, j, i2)
        b2 = jnp.where(better, u[j], b2)
    pick = lambda idx: sum(jnp.where(idx == j, sc[j], 0.0) for j in range(epg))
    w1, w2 = pick(i1), pick(i2)
    tot = w1 + w2
    w1, w2 = w1 / tot, w2 / tot
    la, lb2 = jnp.minimum(i1, i2), jnp.maximum(i1, i2)
    wa = jnp.where(i1 < i2, w1, w2)
    wb = jnp.where(i1 < i2, w2, w1)
    pair_idx = jnp.where(la == 0, lb2 - 1, jnp.where(la == 1, lb2 + 1, 5))
    cls = grp * 6 + pair_idx
    ea = grp * epg + la
    eb = grp * epg + lb2
    return [cls.astype(F32), ea.astype(F32), eb.astype(F32), wa, wb]


def _route_kernel(bias_ref, logits_ref, route_ref):
    rows = _route([logits_ref[e] for e in range(N_EXPERTS)], [bias_ref[e] for e in range(N_EXPERTS)])
    for k, r in enumerate(rows):
        route_ref[k] = r
    for k in range(len(rows), route_ref.shape[0]):
        route_ref[k] = jnp.zeros(route_ref.shape[1:], F32)


def _routing(logits_t, b_router):
    ne, n_tok = logits_t.shape
    w = n_tok // SUBLANES
    out = pl.pallas_call(
        _route_kernel,
        out_shape=jax.ShapeDtypeStruct((8, SUBLANES, w), F32),
        grid=(1,),
        in_specs=[pl.BlockSpec(memory_space=pltpu.SMEM),
                  pl.BlockSpec((ne, SUBLANES, w), lambda i: (0, 0, 0))],
        out_specs=pl.BlockSpec((8, SUBLANES, w), lambda i: (0, 0, 0)),
        compiler_params=_cparams(("arbitrary",)),
        name="route_top2",
    )(b_router.astype(F32), logits_t.reshape(ne, SUBLANES, w))
    return out.reshape(8, n_tok)


def _merge_kernel(*refs, two_sources):
    if two_sources:
        (ctx_ref, x_ref, a_ref, r_ref, ga_ref, gr_ref, gt_ref, sh_ref, sc_ref, g2_ref,
         wa_ref, wr_ref, wo_ref, wrt_ref, xn_ref, h2_ref, logit_ref) = refs
    else:
        (x_ref, a_ref, r_ref, ga_ref, gr_ref, gt_ref, sh_ref, sc_ref, g2_ref,
         wa_ref, wr_ref, wo_ref, wrt_ref, xn_ref, h2_ref, logit_ref) = refs

    def emit(xv):
        ua = jnp.dot(a_ref[...], wa_ref[...], preferred_element_type=F32)
        ur = jnp.dot(r_ref[...], wr_ref[...], preferred_element_type=F32)
        mixed = ga_ref[...].astype(F32) * ua + gr_ref[...].astype(F32) * ur
        y = jnp.dot(mixed.astype(BF16), wo_ref[...], preferred_element_type=F32)
        xn = xv + gt_ref[...] * y
        xn_ref[...] = xn
        h2 = _norm_mod(xn, g2_ref[...], sh_ref[...], sc_ref[...])
        h2_ref[...] = h2
        nt = (((1,), (1,)), ((), ()))
        h_hi = h2.astype(BF16)
        h_lo = (h2 - h_hi.astype(F32)).astype(BF16)
        w_hi = wrt_ref[0]
        w_lo = wrt_ref[1]
        logit_ref[...] = (lax.dot_general(w_hi, h_hi, nt, preferred_element_type=F32)
                          + lax.dot_general(w_lo, h_hi, nt, preferred_element_type=F32)
                          + lax.dot_general(w_hi, h_lo, nt, preferred_element_type=F32))

    if two_sources:
        t = pl.program_id(1)

        @pl.when(t == 0)
        def _():
            emit(ctx_ref[...])

        @pl.when(t > 0)
        def _():
            emit(x_ref[...])
    else:
        emit(x_ref[...])


def _merge(res_inputs, a, r, gates, mods, g2, wa, wr, wo, layer, wrt, n_ctx, with_ctx):
    b, l_a, d = a.shape
    l = gates.shape[1]
    tm = TOKEN_TILE
    off = 0 if with_ctx else n_ctx // tm
    nt = l // tm - off
    n_tok = b * nt * tm
    full = lambda shape: pl.BlockSpec(shape, lambda i, t: (0,) * len(shape))
    tile = lambda o: pl.BlockSpec((None, tm, d), lambda i, t: (i, t + o, 0))
    two_sources = len(res_inputs) == 2 and with_ctx
    if two_sources:
        res_specs = [pl.BlockSpec((None, tm, d), lambda i, t: (i, 0, 0)),
                     pl.BlockSpec((None, tm, d), lambda i, t: (i, jnp.maximum(t - 1, 0), 0))]
    elif len(res_inputs) == 2:
        res_inputs = res_inputs[1:]
        res_specs = [tile(0)]
    else:
        res_specs = [tile(off)]
    mspec = _mod_spec if with_ctx else _mod_spec_lat
    wspec = pl.BlockSpec((None, d, d), lambda i, t: (layer, 0, 0))
    a_off = off - (l - l_a) // tm
    kern = functools.partial(_merge_kernel, two_sources=two_sources)
    return pl.pallas_call(
        kern,
        out_shape=[jax.ShapeDtypeStruct((b, nt * tm, d), F32),
                   jax.ShapeDtypeStruct((n_tok, d), F32),
                   jax.ShapeDtypeStruct((N_EXPERTS, n_tok), F32)],
        grid=(b, nt),
        in_specs=res_specs + [
            tile(a_off), tile(off),
            pl.BlockSpec((None, tm, d), lambda i, t: (i, t + off, 1)),
            pl.BlockSpec((None, tm, d), lambda i, t: (i, t + off, 2)),
            mspec(2, d), mspec(3, d), mspec(4, d), full((1, d)),
            wspec, wspec, wspec, full((2, N_EXPERTS, d))],
        out_specs=[pl.BlockSpec((None, tm, d), lambda i, t: (i, t, 0)),
                   pl.BlockSpec((tm, d), lambda i, t: (i * nt + t, 0)),
                   pl.BlockSpec((N_EXPERTS, tm), lambda i, t: (0, i * nt + t))],
        compiler_params=_cparams(("parallel", "arbitrary")),
        name="merge_route",
    )(*res_inputs, a, r, gates, gates, mods, mods, mods, g2.reshape(1, d), wa, wr, wo, wrt)


def _moe_kernel(tile_a_ref, tile_b_ref, n_act_ref, src_ref, dst_ref, h_hbm, wts_ref,
                wga_ref, wua_ref, wda_ref, wgb_ref, wub_ref, wdb_ref, out_hbm,
                xbuf, ybuf, sem_in, sem_out, *, n_tok):
    i = pl.program_id(0)
    tm = MOE_TILE
    n_act = n_act_ref[0]
    slot = i % 2

    def row_copies(tile, sl, gather, start):
        base = tile * tm

        def body(j, c):
            r0 = j * SUBLANES
            for u in range(SUBLANES):
                if gather:
                    cp = pltpu.make_async_copy(h_hbm.at[pl.ds(src_ref[base + r0 + u], 1), :],
                                               xbuf.at[sl, j, pl.ds(u, 1), :], sem_in.at[sl])
                else:
                    cp = pltpu.make_async_copy(ybuf.at[sl, j, pl.ds(u, 1), :],
                                               out_hbm.at[pl.ds(dst_ref[base + r0 + u], 1), :], sem_out.at[sl])
                if start:
                    cp.start()
                else:
                    cp.wait()
            return c

        lax.fori_loop(0, tm // SUBLANES, body, 0)

    @pl.when(i == 0)
    def _():
        ybuf[0] = jnp.zeros(ybuf.shape[1:], ybuf.dtype)

        def pad_fill(j, start):
            cp = pltpu.make_async_copy(ybuf.at[0, j], out_hbm.at[pl.ds(n_tok + j * SUBLANES, SUBLANES), :],
                                       sem_out.at[0])
            cp.start() if start else cp.wait()

        pl.loop(0, tm // SUBLANES)(functools.partial(pad_fill, start=True))
        pl.loop(0, tm // SUBLANES)(functools.partial(pad_fill, start=False))

        @pl.when(n_act > 0)
        def _():
            row_copies(0, 0, gather=True, start=True)

    @pl.when(i < n_act)
    def _():
        row_copies(i, slot, gather=True, start=False)

        @pl.when(i + 1 < n_act)
        def _():
            row_copies(i + 1, 1 - slot, gather=True, start=True)

        x = xbuf[slot].reshape(tm, xbuf.shape[-1]).astype(BF16)

        def expert(wg_ref, wu_ref, wd_ref):
            gate = jnp.dot(x, wg_ref[...], preferred_element_type=F32)
            up = jnp.dot(x, wu_ref[...], preferred_element_type=F32)
            hid = (gate * _sigmoid(gate) * up).astype(BF16)
            return jnp.dot(hid, wd_ref[...], preferred_element_type=F32)

        wts = wts_ref[...]
        y = wts[:, 0:1] * expert(wga_ref, wua_ref, wda_ref) + wts[:, 1:2] * expert(wgb_ref, wub_ref, wdb_ref)

        @pl.when(i >= 2)
        def _():
            row_copies(i - 2, slot, gather=False, start=False)

        ybuf[slot] = y.reshape(ybuf.shape[1:])
        row_copies(i, slot, gather=False, start=True)

        @pl.when(i == n_act - 1)
        def _():
            row_copies(i, slot, gather=False, start=False)

            @pl.when(i >= 1)
            def _():
                row_copies(i - 1, 1 - slot, gather=False, start=False)


def _pair_class_experts():
    ea, eb = [], []
    for g in range(N_GROUPS):
        for i in range(EXPERTS_PER_GROUP):
            for j in range(i + 1, EXPERTS_PER_GROUP):
                ea.append(g * EXPERTS_PER_GROUP + i)
                eb.append(g * EXPERTS_PER_GROUP + j)
    return np.asarray(ea, np.int32), np.asarray(eb, np.int32)


def _moe_plan(route, n_tok):
    tm = MOE_TILE
    cls = route[0].astype(jnp.int32)
    onehot = (cls[:, None] == jnp.arange(N_PAIR_CLASSES, dtype=jnp.int32)[None, :]).astype(jnp.int32)
    csum = jnp.cumsum(onehot, axis=0)
    counts = csum[-1]
    padded = ((counts + tm - 1) // tm) * tm
    pends = jnp.cumsum(padded)
    pstarts = pends - padded
    pos = jnp.sum(onehot * (csum - 1 + pstarts[None, :]), axis=1)
    n_tiles = n_tok // tm + N_PAIR_CLASSES
    rows = jnp.arange(n_tiles * tm, dtype=jnp.int32)
    pad = jnp.stack([(n_tok + rows % tm).astype(F32), jnp.zeros_like(rows, F32), jnp.zeros_like(rows, F32)], axis=1)
    plan = pad.at[pos].set(jnp.stack([jnp.arange(n_tok, dtype=jnp.int32).astype(F32), route[3], route[4]], axis=1))
    tok = plan[:, 0].astype(jnp.int32)
    wts = plan[:, 1:3]
    tile_rows = rows[::tm]
    n_act = (pends[-1] // tm).reshape(1)
    tile_cls = jnp.sum((jnp.minimum(tile_rows, pends[-1] - 1)[:, None] >= pends[None, :]).astype(jnp.int32), axis=1)
    tile_cls = jnp.minimum(tile_cls, N_PAIR_CLASSES - 1)
    ea_np, eb_np = _pair_class_experts()
    ea = jnp.asarray(ea_np)[tile_cls]
    eb = jnp.asarray(eb_np)[tile_cls]
    return ea, eb, n_act, jnp.minimum(tok, n_tok - 1), tok, wts, n_tiles


def _moe(h2, route, wg, wu, wd, layer):
    n_tok, d = h2.shape
    de = wg.shape[2]
    e0 = layer * N_EXPERTS
    tm = MOE_TILE
    ea, eb, n_act, src, dst, wts, n_tiles = _moe_plan(route, n_tok)
    wspec_a = lambda r, c: pl.BlockSpec((None, r, c), lambda i, ta, tb, na, sr, ds: (e0 + ta[i], 0, 0))
    wspec_b = lambda r, c: pl.BlockSpec((None, r, c), lambda i, ta, tb, na, sr, ds: (e0 + tb[i], 0, 0))
    grid_spec = pltpu.PrefetchScalarGridSpec(
        num_scalar_prefetch=5,
        grid=(n_tiles,),
        in_specs=[pl.BlockSpec(memory_space=pl.ANY),
                  pl.BlockSpec((tm, 2), lambda i, ta, tb, na, sr, ds: (i, 0)),
                  wspec_a(d, de), wspec_a(d, de), wspec_a(de, d),
                  wspec_b(d, de), wspec_b(d, de), wspec_b(de, d)],
        out_specs=pl.BlockSpec(memory_space=pl.ANY),
        scratch_shapes=[pltpu.VMEM((2, tm // SUBLANES, SUBLANES, d), F32),
                        pltpu.VMEM((2, tm // SUBLANES, SUBLANES, d), F32),
                        pltpu.SemaphoreType.DMA((2,)), pltpu.SemaphoreType.DMA((2,))],
    )
    return pl.pallas_call(
        functools.partial(_moe_kernel, n_tok=n_tok),
        out_shape=jax.ShapeDtypeStruct((n_tok + tm, d), F32),
        grid_spec=grid_spec,
        compiler_params=pltpu.CompilerParams(dimension_semantics=("arbitrary",), vmem_limit_bytes=VMEM_LIMIT,
                                             has_side_effects=True),
        name="moe_pairs",
    )(ea, eb, n_act, src, dst, h2, wts, wg, wu, wd, wg, wu, wd)


def _post_kernel(x_ref, y_ref, gt_ref, g_ref, sh_ref, sc_ref, xo_ref, h_ref):
    xn = x_ref[...] + gt_ref[...] * y_ref[...]
    xo_ref[...] = xn
    h_ref[...] = _norm_mod(xn, g_ref[...], sh_ref[...], sc_ref[...]).astype(h_ref.dtype)


def _final_kernel(x_ref, y_ref, gt_ref, g_ref, o_ref):
    xn = x_ref[...] + gt_ref[...] * y_ref[...]
    o_ref[...] = xn * lax.rsqrt(jnp.mean(xn * xn, axis=-1, keepdims=True) + EPS) * g_ref[...]


def _post(x_all, moe_out, mods, mods_next, g_next):
    b, l, d = x_all.shape
    tm = TOKEN_TILE
    nt = l // tm
    tile = pl.BlockSpec((None, tm, d), lambda i, t: (i, t, 0))
    return pl.pallas_call(
        _post_kernel,
        out_shape=[jax.ShapeDtypeStruct((b, l, d), F32), jax.ShapeDtypeStruct((b, l, d), BF16)],
        grid=(b, nt),
        in_specs=[tile, pl.BlockSpec((tm, d), lambda i, t: (i * nt + t, 0)),
                  _mod_spec(5, d), pl.BlockSpec((1, d), lambda i, t: (0, 0)), _mod_spec(0, d), _mod_spec(1, d)],
        out_specs=[tile, tile],
        compiler_params=_cparams(("parallel", "arbitrary")),
        name="moe_residual_prenorm",
    )(x_all, moe_out, mods, g_next.reshape(1, d), mods_next, mods_next)


def _final(x_lat, moe_out, mods, g_final):
    b, s, d = x_lat.shape
    tm = TOKEN_TILE
    nt = s // tm
    tile = pl.BlockSpec((None, tm, d), lambda i, t: (i, t, 0))
    return pl.pallas_call(
        _final_kernel,
        out_shape=jax.ShapeDtypeStruct((b, s, d), F32),
        grid=(b, nt),
        in_specs=[tile, pl.BlockSpec((tm, d), lambda i, t: (i * nt + t, 0)),
                  _mod_spec_lat(5, d), pl.BlockSpec((1, d), lambda i, t: (0, 0))],
        out_specs=tile,
        compiler_params=_cparams(("parallel", "arbitrary")),
        name="moe_residual_final_norm",
    )(x_lat, moe_out, mods, g_final.reshape(1, d))


def _rope_tables(n_ctx, n_lat):
    t = np.arange(n_lat)
    n_freq = ATT_HEAD_DIM // 4
    inv = (ROPE_BASE ** (-np.arange(n_freq, dtype=np.float32) / n_freq)).astype(np.float32)
    ang_r = (t // GRID_W).astype(np.float32)[:, None] * inv
    ang_c = (t % GRID_W).astype(np.float32)[:, None] * inv
    ang = np.concatenate([ang_r, ang_r, ang_c, ang_c], axis=1)
    ang = np.concatenate([ang, ang], axis=1)
    sign = np.where((np.arange(HEAD_W) % 32) < 16, -1.0, 1.0).astype(np.float32)
    cos = np.concatenate([np.ones((n_ctx, HEAD_W), np.float32), np.cos(ang)], axis=0)
    sin = np.concatenate([np.zeros((n_ctx, HEAD_W), np.float32), np.sin(ang) * sign], axis=0)
    return jnp.asarray(cos, F32), jnp.asarray(sin, F32)


def kernel(x, c, ctx, c_ctx, w_mod, b_mod, g_norm1, g_norm2, w_in, lambda_q1, lambda_k1, lambda_q2, lambda_k2,
           g_subln, lb_logits, g_rec_norm, w_br_attn, w_br_rec, w_out, w_router, b_router, w_gate, w_up, w_down,
           g_final):
    b, s, d = x.shape
    n_ctx = ctx.shape[1]
    depth = w_in.shape[0]
    assert n_ctx == TOKEN_TILE and s % TOKEN_TILE == 0 and s % GRID_W == 0
    assert w_in.shape[2] == N_SPLITS * d

    bp = -(-(b + 1) // 8) * 8
    c_all = jnp.concatenate([c, c_ctx[None], jnp.zeros((bp - b - 1, d), F32)], axis=0)
    mod_all = _modulation(c_all, w_mod, b_mod)
    mods = [jnp.stack([jnp.broadcast_to(mod_all[l, b], (b, 6 * d)), mod_all[l, :b]], axis=1)
            .reshape(2 * b, 1, 6 * d) for l in range(depth)]

    lbs = jnp.cumsum(jax.nn.softmax(lb_logits.astype(F32), axis=0), axis=0)
    lbs = lbs - lbs[:1]
    cos_t, sin_t = _rope_tables(n_ctx, s)
    wrt_f = jnp.transpose(w_router).astype(F32)
    wrt_hi = wrt_f.astype(BF16)
    wrt = jnp.stack([wrt_hi, (wrt_f - wrt_hi.astype(F32)).astype(BF16)])
    l_all = n_ctx + s
    full2 = lambda shape: pl.BlockSpec(shape, lambda j, i: (0, 0))

    w_in_bf, w_br_a_bf, w_br_r_bf, w_out_bf = (_to_bf16(w) for w in (w_in, w_br_attn, w_br_rec, w_out))
    w_gate_bf, w_up_bf, w_down_bf = (_to_bf16(w).reshape((-1,) + w.shape[2:]) for w in (w_gate, w_up, w_down))
    x_all = None
    out = None
    h = _prenorm(ctx, x, g_norm1[0], mods[0])
    for l in range(depth):
        last = l == depth - 1
        lam_init = 0.8 - 0.6 * math.exp(-0.3 * l)
        lam = (jnp.exp(jnp.sum(lambda_q1[l].astype(F32) * lambda_k1[l].astype(F32)))
               - jnp.exp(jnp.sum(lambda_q2[l].astype(F32) * lambda_k2[l].astype(F32))) + lam_init)
        (qk,) = _project(_proj_rope_kernel, h, w_in_bf, l, 0, 2, [cos_t, sin_t],
                         [full2((l_all, HEAD_W)), full2((l_all, HEAD_W))], [BF16], "proj_rope")
        (vq,) = _project(_proj_plain_kernel, h, w_in_bf, l, 2, 2, [], [], [BF16], "proj_v_rq")
        if l == 0:
            kk, gg = _project(_proj_forget_nobound_kernel, h, w_in_bf, l, 4, 2, [], [], [BF16, F32], "proj_forget0")
        else:
            kk, gg = _project(_proj_forget_kernel, h, w_in_bf, l, 4, 2, [lbs[l].reshape(1, d)],
                              [full2((1, d))], [BF16, F32], "proj_forget")
        (ri,) = _project(_proj_plain_kernel, h, w_in_bf, l, 6, 1, [], [], [BF16], "proj_ri")
        (gates,) = _project(_proj_gates_kernel, h, w_in_bf, l, 7, 3, [], [], [BF16], "proj_gates")

        a = _attention(qk, vq, lam, g_subln[l], lam_init, n_ctx, with_ctx=not last)
        r = _recurrence(vq, kk, gg, ri, gates, g_rec_norm[l], n_ctx, ctx_outputs=not last)

        res = (ctx, x) if l == 0 else (x_all,)
        xn, h2, logits_t = _merge(res, a, r, gates, mods[l], g_norm2[l], w_br_a_bf, w_br_r_bf, w_out_bf, l,
                                  wrt, n_ctx, with_ctx=not last)
        route = _routing(logits_t, b_router)
        moe_out = _moe(h2, route, w_gate_bf, w_up_bf, w_down_bf, l)
        if last:
            out = _final(xn, moe_out, mods[l], g_final)
        else:
            x_all, h = _post(xn, moe_out, mods[l], mods[l + 1], g_norm1[l + 1])
    return out
```

```python
import functools
import math

import numpy as np
import jax
import jax.numpy as jnp
from jax import lax
from jax.experimental import pallas as pl
from jax.experimental.pallas import tpu as pltpu

EPS = 1e-6
GRID_W = 64
ROPE_BASE = 10000.0
ATT_HEADS = 8
ATT_HEAD_DIM = 64
HEAD_W = 2 * ATT_HEAD_DIM
REC_HEADS = 8
N_EXPERTS = 16
N_GROUPS = 4
EXPERTS_PER_GROUP = N_EXPERTS // N_GROUPS
N_PAIR_CLASSES = N_GROUPS * 6
N_SPLITS = 10

LANES = 128
SUBLANES = 8
TOKEN_TILE = 256
PROJ_ROWS = 768
ATT_HEADS_PER_STEP = 4
MERGE_TILES_PER_STEP = (4, 3, 2, 1)
GLA_CHUNK = 64
GLA_GROUP = 4
GLA_MIN_BCAST_BLOCK = 4
CAST_BLOCK_BYTES = 8 * 1024 * 1024
MOE_TILE = 256
VMEM_LIMIT = 56 * 1024 * 1024

F32 = jnp.float32
BF16 = jnp.bfloat16
LOG2_E = math.log2(math.e)


def _sigmoid(x):
    return 1.0 / (1.0 + jnp.exp(-x))


def _cparams(sem):
    return pltpu.CompilerParams(dimension_semantics=sem, vmem_limit_bytes=VMEM_LIMIT)


def _mod_kernel(c_ref, w_ref, b_ref, o_ref):
    c = c_ref[...]
    s = c * _sigmoid(c)
    o_ref[...] = jnp.dot(s, w_ref[...], precision=lax.Precision.HIGHEST,
                         preferred_element_type=F32) + b_ref[...]


def _modulation(c_all, w_mod, b_mod):
    depth, d, n = w_mod.shape
    bp = c_all.shape[0]
    tn = 1024
    return pl.pallas_call(
        _mod_kernel,
        out_shape=jax.ShapeDtypeStruct((depth, bp, n), F32),
        grid=(depth, n // tn),
        in_specs=[pl.BlockSpec((bp, d), lambda l, j: (0, 0)),
                  pl.BlockSpec((None, d, tn), lambda l, j: (l, 0, j)),
                  pl.BlockSpec((None, 1, tn), lambda l, j: (l, 0, j))],
        out_specs=pl.BlockSpec((None, bp, tn), lambda l, j: (l, 0, j)),
        compiler_params=_cparams(("arbitrary", "arbitrary")),
        name="adaln_mod",
    )(c_all, w_mod, b_mod.reshape(depth, 1, n))


def _cast_kernel(x_ref, o_ref):
    o_ref[...] = x_ref[...].astype(o_ref.dtype)


def _to_bf16(w):
    shape = w.shape
    cols = shape[-1]
    rows = w.size // cols
    tr = rows
    while tr * cols * 4 > CAST_BLOCK_BYTES and tr % 16 == 0:
        tr //= 2
    out = pl.pallas_call(
        _cast_kernel,
        out_shape=jax.ShapeDtypeStruct((rows, cols), BF16),
        grid=(rows // tr,),
        in_specs=[pl.BlockSpec((tr, cols), lambda i: (i, 0))],
        out_specs=pl.BlockSpec((tr, cols), lambda i: (i, 0)),
        compiler_params=_cparams(("arbitrary",)),
        name="weights_to_bf16",
    )(w.reshape(rows, cols))
    return out.reshape(shape)


def _norm_mod(x, g, sh, sc):
    y = x * lax.rsqrt(jnp.mean(x * x, axis=-1, keepdims=True) + EPS) * g
    return y * (1.0 + sc) + sh


def _mod_spec(piece, d):
    return pl.BlockSpec((None, 1, d), lambda b, t: (2 * b + jnp.minimum(t, 1), 0, piece))


def _mod_spec_lat(piece, d):
    return pl.BlockSpec((None, 1, d), lambda b, t: (2 * b + 1, 0, piece))


def _prenorm_kernel(ctx_ref, x_ref, g_ref, sh_ref, sc_ref, h_ref):
    t = pl.program_id(1)

    def emit(xv):
        h_ref[...] = _norm_mod(xv, g_ref[...], sh_ref[...], sc_ref[...]).astype(BF16)

    @pl.when(t == 0)
    def _():
        emit(ctx_ref[...])

    @pl.when(t > 0)
    def _():
        emit(x_ref[...])


def _prenorm(ctx, x, g, mods):
    b, c, d = ctx.shape
    s = x.shape[1]
    tm = TOKEN_TILE
    nt = (c + s) // tm
    return pl.pallas_call(
        _prenorm_kernel,
        out_shape=jax.ShapeDtypeStruct((b, c + s, d), BF16),
        grid=(b, nt),
        in_specs=[pl.BlockSpec((None, tm, d), lambda i, t: (i, 0, 0)),
                  pl.BlockSpec((None, tm, d), lambda i, t: (i, jnp.maximum(t - 1, 0), 0)),
                  pl.BlockSpec((1, d), lambda i, t: (0, 0)),
                  _mod_spec(0, d), _mod_spec(1, d)],
        out_specs=pl.BlockSpec((None, tm, d), lambda i, t: (i, t, 0)),
        compiler_params=_cparams(("parallel", "arbitrary")),
        name="prenorm",
    )(ctx, x, g.reshape(1, d), mods, mods)


def _swap16(xs, first16):
    return jnp.where(first16, pltpu.roll(xs, LANES - 16, axis=1), pltpu.roll(xs, 16, axis=1))


def _proj_chunks(h_ref, w_ref, epilogue):
    n = h_ref.shape[0] // PROJ_ROWS
    rows = [slice(r * PROJ_ROWS, (r + 1) * PROJ_ROWS) for r in range(n)]
    mm = lambda r: jnp.dot(h_ref[rows[r], :], w_ref[...], preferred_element_type=F32)
    acc = mm(0)
    for r in range(n):
        nxt = mm(r + 1) if r + 1 < n else None
        epilogue(rows[r], acc)
        acc = nxt


def _proj_rope_kernel(h_ref, w_ref, cos_ref, sin_ref, o_ref):
    j = pl.program_id(0)
    scale = jnp.where(j == 0, ATT_HEAD_DIM ** -0.5 * math.log2(math.e), 1.0).astype(F32)
    lane = lax.broadcasted_iota(jnp.int32, (PROJ_ROWS, LANES), 1)
    first16 = (lane % 32) < 16

    def epilogue(rows, acc):
        cos = cos_ref[rows, :] * scale
        sin = sin_ref[rows, :] * scale
        for k in range(acc.shape[1] // LANES):
            xs = acc[:, k * LANES:(k + 1) * LANES]
            o_ref[rows, k * LANES:(k + 1) * LANES] = (xs * cos + _swap16(xs, first16) * sin).astype(o_ref.dtype)

    _proj_chunks(h_ref, w_ref, epilogue)


def _proj_plain_kernel(h_ref, w_ref, o_ref):
    def epilogue(rows, acc):
        o_ref[rows, :] = acc.astype(o_ref.dtype)

    _proj_chunks(h_ref, w_ref, epilogue)


def _proj_forget_kernel(h_ref, w_ref, lb_ref, k_ref, g_ref):
    lb = lb_ref[...]

    def epilogue(rows, z):
        e = jnp.exp(-jnp.abs(z))
        sig = jnp.where(z >= 0.0, 1.0, e) / (1.0 + e)
        f = lb + (1.0 - lb) * sig
        g_ref[rows, :] = jnp.where(f > 0.0, jnp.log(f), z)
        k_ref[rows, :] = (1.0 - f).astype(k_ref.dtype)

    _proj_chunks(h_ref, w_ref, epilogue)


def _proj_forget_nobound_kernel(h_ref, w_ref, k_ref, g_ref):
    def epilogue(rows, z):
        log_f = jnp.minimum(z, 0.0) - jnp.log(1.0 + jnp.exp(-jnp.abs(z)))
        g_ref[rows, :] = log_f
        k_ref[rows, :] = (1.0 - jnp.exp(log_f)).astype(k_ref.dtype)

    _proj_chunks(h_ref, w_ref, epilogue)


def _proj_gates_kernel(h_ref, w_ref, o_ref):
    j = pl.program_id(0)

    def epilogue(rows, z):
        sg = _sigmoid(z)
        o_ref[rows, :] = jnp.where(j == 0, z * sg, sg).astype(o_ref.dtype)

    _proj_chunks(h_ref, w_ref, epilogue)


def _project(kernel, h, w_bf, layer, first_split, n_splits, extra_inputs, extra_specs, out_dtypes, name):
    b, l, d = h.shape
    wcol = w_bf.shape[2] // N_SPLITS
    out_shape = [jax.ShapeDtypeStruct((b, l, n_splits * wcol), dt) for dt in out_dtypes]
    out_specs = [pl.BlockSpec((None, l, wcol), lambda j, i: (i, 0, j)) for _ in out_dtypes]
    res = pl.pallas_call(
        kernel,
        out_shape=out_shape,
        grid=(n_splits, b),
        in_specs=[pl.BlockSpec((None, l, d), lambda j, i: (i, 0, 0)),
                  pl.BlockSpec((None, d, wcol), lambda j, i: (layer, 0, first_split + j))] + extra_specs,
        out_specs=out_specs,
        compiler_params=_cparams(("arbitrary", "arbitrary")),
        name=name,
    )(h, w_bf, *extra_inputs)
    return res


def _attn_kernel(lam_ref, q_ref, k_ref, v_ref, g_ref, o_ref, *, n_ctx, ctx_tile, out_scale):
    lam = lam_ref[0]
    nt = (((1,), (1,)), ((), ()))
    heads = range(ATT_HEADS_PER_STEP)
    col = lambda hh: slice(hh * HEAD_W, (hh + 1) * HEAD_W)

    def attend(n_keys):
        scores = []
        for hh in heads:
            q = q_ref[:, col(hh)]
            k = k_ref[0:n_keys, col(hh)]
            lane = lax.broadcasted_iota(jnp.int32, q.shape, 1)
            zero = jnp.zeros_like(q)
            q1 = jnp.where(lane < ATT_HEAD_DIM, q, zero)
            q2 = jnp.where(lane < ATT_HEAD_DIM, zero, q)
            scores.append((lax.dot_general(q1, k, nt, preferred_element_type=F32),
                           lax.dot_general(q2, k, nt, preferred_element_type=F32)))
        for hh in heads:
            v = v_ref[0:n_keys, col(hh)]
            v1 = jnp.concatenate([v, jnp.ones_like(v)], axis=1)
            outs = []
            for sm in scores[hh]:
                p = jnp.exp2(sm - jnp.max(sm, axis=-1, keepdims=True))
                outs.append(jnp.dot(p.astype(BF16), v1, preferred_element_type=F32))
            (o1, o2) = outs
            l1, l2 = o1[:, HEAD_W:HEAD_W + 1], o2[:, HEAD_W:HEAD_W + 1]
            o = o1[:, :HEAD_W] * (1.0 / l1) - o2[:, :HEAD_W] * (lam / l2)
            o = o * lax.rsqrt(jnp.mean(o * o, axis=-1, keepdims=True) + EPS) * g_ref[...] * out_scale
            o_ref[:, col(hh)] = o.astype(o_ref.dtype)

    n_all = k_ref.shape[0]
    if ctx_tile:
        t = pl.program_id(2)

        @pl.when(t == 0)
        def _():
            attend(n_ctx)

        @pl.when(t > 0)
        def _():
            attend(n_all)
    else:
        attend(n_all)


def _attention(qk, vq, lam, g_subln, lam_init, n_ctx, with_ctx):
    b, l, _ = qk.shape
    hp = ATT_HEADS_PER_STEP
    ng = ATT_HEADS // hp
    tq = TOKEN_TILE
    off = 0 if with_ctx else n_ctx // tq
    n_out = l if with_ctx else l - n_ctx
    kern = functools.partial(_attn_kernel, n_ctx=n_ctx, ctx_tile=with_ctx, out_scale=1.0 - lam_init)
    return pl.pallas_call(
        kern,
        out_shape=jax.ShapeDtypeStruct((b, n_out, ATT_HEADS * HEAD_W), BF16),
        grid=(b, ng, n_out // tq),
        in_specs=[pl.BlockSpec(memory_space=pltpu.SMEM),
                  pl.BlockSpec((None, tq, hp * HEAD_W), lambda i, h, t: (i, t + off, h)),
                  pl.BlockSpec((None, l, hp * HEAD_W), lambda i, h, t: (i, 0, ng + h)),
                  pl.BlockSpec((None, l, hp * HEAD_W), lambda i, h, t: (i, 0, h)),
                  pl.BlockSpec((1, HEAD_W), lambda i, h, t: (0, 0))],
        out_specs=pl.BlockSpec((None, tq, hp * HEAD_W), lambda i, h, t: (i, t, h)),
        compiler_params=_cparams(("parallel", "arbitrary", "arbitrary")),
        name="diff_attention",
    )(lam.reshape(1), qk, qk, vq, g_subln.reshape(1, HEAD_W))


def _gla_tables(chunk):
    c = chunk
    levels = []
    b = c // 2
    while b >= 1:
        levels.append(b)
        b //= 2
    t = np.arange(c)
    u = np.arange(c)[None, :]

    def build(forward):
        mats, qrows, pmask = [], [], []
        tt = t[:, None]
        mats.append((u <= tt) if forward else (u >= tt))
        for b in levels:
            pair = t // (2 * b)
            late = (t % (2 * b)) >= b
            m = pair * 2 * b + b - 1
            if forward:
                is_q = late
                lo = np.where(is_q, m + 1, t + 1)[:, None]
                hi = np.where(is_q, t, m)[:, None]
            else:
                is_q = ~late
                lo = np.where(is_q, t, m + 1)[:, None]
                hi = np.where(is_q, m, t - 1)[:, None]
            if b < GLA_MIN_BCAST_BLOCK:
                mats.append((u >= lo) & (u <= hi))
            qrows.append(np.repeat(is_q[:, None], LANES, 1).astype(np.float32))
            pmask.append(((pair[:, None] == pair[None, :]) & is_q[:, None] & (~is_q)[None, :]).astype(np.float32))
        return np.concatenate(mats, 0).astype(np.float32), np.stack(qrows), np.stack(pmask)

    mf, qf, pf = build(True)
    mb, qb, pb = build(False)
    return levels, np.stack([mf, mb]), np.stack([qf, qb]), np.stack([pf, pb])


def _gla_kernel(q_ref, kf_ref, kb_ref, v_ref, gf_ref, gb_ref, gate_ref, gn_ref, m_ref, qrow_ref, pm_ref,
                o_ref, of_ref, ob_ref, sf_ref, sb_ref, *, levels, n_ctx_chunks, n_chunks, ctx_outputs):
    c = GLA_CHUNK
    u_n = GLA_GROUP
    n_levels = len(levels)
    sf_ref[...] = jnp.zeros_like(sf_ref)
    sb_ref[...] = jnp.zeros_like(sb_ref)
    tn = (((0,), (0,)), ((), ()))
    nt = (((1,), (1,)), ((), ()))

    def decays(direction, r0, g_ref):
        rows = [pl.ds(r0 + u * c, c) for u in range(u_n)]
        g = jnp.concatenate([g_ref[rw, :] for rw in rows], axis=1)
        g_hi = g.astype(BF16)
        r1 = g - g_hi.astype(F32)
        g_mid = r1.astype(BF16)
        g_lo = (r1 - g_mid.astype(F32)).astype(BF16)
        m = m_ref[direction]
        sums = (jnp.dot(m, g_hi, preferred_element_type=F32) + jnp.dot(m, g_mid, preferred_element_type=F32)
                + jnp.dot(m, g_lo, preferred_element_type=F32))
        run = sums[0:c, :]

        def to_row(blk, row):
            return jnp.exp2(jnp.abs(blk - jnp.broadcast_to(run[row:row + 1, :], blk.shape)) * (-LOG2_E))

        dec, n_direct = [], 0
        for b in levels:
            if b >= GLA_MIN_BCAST_BLOCK:
                parts = [to_row(run[p * 2 * b:(p + 1) * 2 * b, :], p * 2 * b + (b - 1 if direction == 0 else b))
                         for p in range(c // (2 * b))]
                dec.append(parts[0] if len(parts) == 1 else jnp.concatenate(parts, axis=0))
            else:
                n_direct += 1
                dec.append(jnp.exp(sums[n_direct * c:(n_direct + 1) * c, :]))
        dec.append(jnp.exp(run))
        dec.append(to_row(run, c - 1 if direction == 0 else 0))
        return dec

    def group_pair(r0s, outputs=True):
        k_refs, g_refs, s_refs, out_refs = (kf_ref, kb_ref), (gf_ref, gb_ref), (sf_ref, sb_ref), (of_ref, ob_ref)
        dec = [decays(d, r0s[d], g_refs[d]) for d in (0, 1)]
        work = []
        for d in (0, 1):
            for u in range(u_n):
                rw = pl.ds(r0s[d] + u * c, c)
                k = k_refs[d][rw, :].astype(F32)
                v = v_ref[rw, :]
                du = [e[:, u * LANES:(u + 1) * LANES] for e in dec[d]]
                q_dec, k_dec = du[n_levels], du[n_levels + 1]
                w = dict(d=d, rw=rw, v=v, k_state=(k * k_dec).astype(BF16),
                         total=q_dec[c - 1:c, :] if d == 0 else q_dec[0:1, :])
                if outputs:
                    q = q_ref[rw, :].astype(F32)
                    level_prods = []
                    for lv, b in enumerate(levels):
                        if b >= SUBLANES:
                            qk = jnp.concatenate([(q if ((r // b) % 2 == 1) == (d == 0) else k)[r:r + b, :]
                                                  for r in range(0, c, b)], axis=0)
                        else:
                            qk = jnp.where(qrow_ref[d, lv] > 0.5, q, k)
                        x = (qk * du[lv]).astype(BF16)
                        level_prods.append(lax.dot_general(x, x, nt, preferred_element_type=F32))
                    w.update(prods=level_prods, q_state=(q * q_dec).astype(BF16),
                             diag=jnp.sum(q * k, axis=-1, keepdims=True) * v.astype(F32))
                work.append(w)
        for w in work:
            if outputs:
                scores = jnp.zeros((c, c), F32)
                for lv in range(n_levels):
                    scores = jnp.where(pm_ref[w["d"], lv] > 0.5, w["prods"][lv], scores)
                w["intra"] = w["diag"] + jnp.dot(scores.astype(BF16), w["v"], preferred_element_type=F32)
            w["vk"] = lax.dot_general(w["v"], w["k_state"], tn, preferred_element_type=F32)
        for d in (0, 1):
            state_t = s_refs[d][...]
            mine = [w for w in work if w["d"] == d]
            for w in (mine if d == 0 else mine[::-1]):
                if outputs:
                    out_refs[d][w["rw"], :] = w["intra"] + lax.dot_general(
                        w["q_state"], state_t.astype(BF16), nt, preferred_element_type=F32)
                state_t = state_t * w["total"] + w["vk"]
            s_refs[d][...] = state_t

    n_ctx_groups = n_ctx_chunks // u_n
    n_groups = n_chunks // u_n

    def rows_of(i):
        top = n_ctx_chunks - i * u_n if i < n_ctx_groups else n_chunks - (i - n_ctx_groups) * u_n
        return i * u_n * c, (top - u_n) * c

    for i in range(n_groups):
        group_pair(rows_of(i), outputs=ctx_outputs or i >= n_ctx_groups)
    n_ctx_rows = n_ctx_chunks * c
    if ctx_outputs:
        out_rows = slice(0, n_chunks * c)
    else:
        out_rows = slice(n_ctx_rows, n_chunks * c)
        o_ref[0:n_ctx_rows, :] = jnp.zeros((n_ctx_rows, LANES), o_ref.dtype)
    o = of_ref[out_rows, :] + ob_ref[out_rows, :]
    o = o * lax.rsqrt(jnp.mean(o * o, axis=-1, keepdims=True) + EPS) * gn_ref[...]
    o_ref[out_rows, :] = (o * gate_ref[out_rows, :].astype(F32)).astype(o_ref.dtype)


def _recurrence(vq, kk, gg, ri, gates, g_rec, n_ctx, ctx_outputs):
    b, l, _ = vq.shape
    c = GLA_CHUNK
    levels, m_np, qrow_np, pm_np = _gla_tables(c)
    nh = REC_HEADS
    assert (n_ctx // c) % GLA_GROUP == 0 and ((l - n_ctx) // c) % GLA_GROUP == 0
    kern = functools.partial(_gla_kernel, levels=tuple(levels), n_ctx_chunks=n_ctx // c, n_chunks=l // c,
                             ctx_outputs=ctx_outputs)
    head = lambda off: pl.BlockSpec((None, l, LANES), lambda i, h: (i, 0, h + off))
    const = lambda a: pl.BlockSpec(a.shape, lambda i, h: (0,) * a.ndim)
    m_all = jnp.asarray(m_np, BF16)
    qrow = jnp.asarray(qrow_np, F32)
    pm = jnp.asarray(pm_np, F32)
    return pl.pallas_call(
        kern,
        out_shape=jax.ShapeDtypeStruct((b, l, nh * LANES), BF16),
        grid=(b, nh),
        in_specs=[head(nh), head(0), head(nh), head(0), head(0), head(nh), head(0),
                  pl.BlockSpec((1, LANES), lambda i, h: (0, 0)),
                  const(m_all), const(qrow), const(pm)],
        out_specs=head(0),
        scratch_shapes=[pltpu.VMEM((l, LANES), F32), pltpu.VMEM((l, LANES), F32),
                        pltpu.VMEM((LANES, LANES), F32), pltpu.VMEM((LANES, LANES), F32)],
        compiler_params=_cparams(("parallel", "arbitrary")),
        name="hgrn2_recurrence",
    )(vq, kk, kk, ri, gg, gg, gates, g_rec.reshape(1, LANES), m_all, qrow, pm)


def _route(logits, bias):
    scores = [_sigmoid(lg) for lg in logits]
    biased = [sc + bs for sc, bs in zip(scores, bias)]
    row = lambda a, e: a[e]
    epg = EXPERTS_PER_GROUP
    pairs = [(i, j) for i in range(epg) for j in range(i + 1, epg)]
    best_gs, grp = None, None
    for gi in range(N_GROUPS):
        gs = None
        for (i, j) in pairs:
            s = row(biased, gi * epg + i) + row(biased, gi * epg + j)
            gs = s if gs is None else jnp.maximum(gs, s)
        if best_gs is None:
            best_gs, grp = gs, jnp.zeros_like(gs, dtype=jnp.int32)
        else:
            better = gs > best_gs
            grp = jnp.where(better, gi, grp)
            best_gs = jnp.maximum(best_gs, gs)

    def in_group(a, j):
        out = row(a, j)
        for gi in range(1, N_GROUPS):
            out = jnp.where(grp == gi, row(a, gi * epg + j), out)
        return out

    u = [in_group(biased, j) for j in range(epg)]
    sc = [in_group(scores, j) for j in range(epg)]
    b1, i1 = u[0], jnp.zeros_like(grp)
    for j in range(1, epg):
        better = u[j] > b1
        i1 = jnp.where(better, j, i1)
        b1 = jnp.maximum(b1, u[j])
    i2 = jnp.where(i1 == 0, 1, 0)
    b2 = jnp.where(i1 == 0, u[1], u[0])
    for j in range(1, epg):
        better = (i1 != j) & (u[j] > b2)
        i2 = jnp.where(better, j, i2)
        b2 = jnp.where(better, u[j], b2)
    pick = lambda idx: sum(jnp.where(idx == j, sc[j], 0.0) for j in range(epg))
    w1, w2 = pick(i1), pick(i2)
    tot = w1 + w2
    w1, w2 = w1 / tot, w2 / tot
    la, lb2 = jnp.minimum(i1, i2), jnp.maximum(i1, i2)
    wa = jnp.where(i1 < i2, w1, w2)
    wb = jnp.where(i1 < i2, w2, w1)
    pair_idx = jnp.where(la == 0, lb2 - 1, jnp.where(la == 1, lb2 + 1, 5))
    cls = grp * 6 + pair_idx
    ea = grp * epg + la
    eb = grp * epg + lb2
    return [cls.astype(F32), ea.astype(F32), eb.astype(F32), wa, wb]


def _route_kernel(bias_ref, logits_ref, route_ref):
    rows = _route([logits_ref[e] for e in range(N_EXPERTS)], [bias_ref[e] for e in range(N_EXPERTS)])
    for k, r in enumerate(rows):
        route_ref[k] = r
    for k in range(len(rows), route_ref.shape[0]):
        route_ref[k] = jnp.zeros(route_ref.shape[1:], F32)


def _routing(logits_t, b_router):
    ne, n_tok = logits_t.shape
    w = n_tok // SUBLANES
    out = pl.pallas_call(
        _route_kernel,
        out_shape=jax.ShapeDtypeStruct((8, SUBLANES, w), F32),
        grid=(1,),
        in_specs=[pl.BlockSpec(memory_space=pltpu.SMEM),
                  pl.BlockSpec((ne, SUBLANES, w), lambda i: (0, 0, 0))],
        out_specs=pl.BlockSpec((8, SUBLANES, w), lambda i: (0, 0, 0)),
        compiler_params=_cparams(("arbitrary",)),
        name="route_top2",
    )(b_router.astype(F32), logits_t.reshape(ne, SUBLANES, w))
    return out.reshape(8, n_tok)


def _merge_kernel(*refs, k, ctx_source, ctx_mods):
    refs = list(refs)
    take = lambda n: [refs.pop(0) for _ in range(n)]
    ctx_ref = take(1)[0] if ctx_source else None
    x_refs, a_refs, r_refs, ga_refs, gr_refs = take(k), take(k), take(k), take(k), take(k)
    cmod_refs = take(3) if ctx_mods else None
    gt_ref, sh_ref, sc_ref, g2_ref, wa_ref, wr_ref, wo_ref, wrt_ref, xn_ref, h2_ref, logit_ref = refs
    tm = x_refs[0].shape[0]
    cat = lambda rs: rs[0][...] if len(rs) == 1 else jnp.concatenate([rf[...] for rf in rs], axis=0)

    def emit(first_is_ctx):
        xv = cat(([ctx_ref] + x_refs[1:]) if (first_is_ctx and ctx_source) else x_refs)

        def mod(ref, cref):
            if not (first_is_ctx and ctx_mods):
                return ref[...]
            rows = [jnp.broadcast_to(cref[...], (tm, cref.shape[-1]))]
            if k > 1:
                rows.append(jnp.broadcast_to(ref[...], ((k - 1) * tm, ref.shape[-1])))
            return rows[0] if k == 1 else jnp.concatenate(rows, axis=0)

        gt, sh, sc = (mod(rf, cmod_refs[n] if ctx_mods else None)
                      for n, rf in enumerate((gt_ref, sh_ref, sc_ref)))
        ua = jnp.dot(cat(a_refs), wa_ref[...], preferred_element_type=F32)
        ur = jnp.dot(cat(r_refs), wr_ref[...], preferred_element_type=F32)
        mixed = cat(ga_refs).astype(F32) * ua + cat(gr_refs).astype(F32) * ur
        y = jnp.dot(mixed.astype(BF16), wo_ref[...], preferred_element_type=F32)
        xn = xv + gt * y
        xn_ref[...] = xn
        h2 = _norm_mod(xn, g2_ref[...], sh, sc)
        h2_ref[...] = h2
        nt = (((1,), (1,)), ((), ()))
        h_hi = h2.astype(BF16)
        h_lo = (h2 - h_hi.astype(F32)).astype(BF16)
        w_hi = wrt_ref[0]
        w_lo = wrt_ref[1]
        logit_ref[...] = (lax.dot_general(w_hi, h_hi, nt, preferred_element_type=F32)
                          + lax.dot_general(w_lo, h_hi, nt, preferred_element_type=F32)
                          + lax.dot_general(w_hi, h_lo, nt, preferred_element_type=F32))

    if ctx_source or ctx_mods:
        t = pl.program_id(1)

        @pl.when(t == 0)
        def _():
            emit(True)

        @pl.when(t > 0)
        def _():
            emit(False)
    else:
        emit(False)


def _merge(res_inputs, a, r, gates, mods, g2, wa, wr, wo, layer, wrt, n_ctx, with_ctx):
    b, l_a, d = a.shape
    l = gates.shape[1]
    tm = TOKEN_TILE
    off = 0 if with_ctx else n_ctx // tm
    n_tiles = l // tm - off
    k = next(kk for kk in MERGE_TILES_PER_STEP if n_tiles % kk == 0)
    nt = n_tiles // k
    n_tok = b * n_tiles * tm
    full = lambda shape: pl.BlockSpec(shape, lambda i, t: (0,) * len(shape))
    tile = lambda o, j, col=0: pl.BlockSpec((None, tm, d), lambda i, t: (i, k * t + j + o, col))
    tiles = lambda o, col=0: [tile(o, j, col) for j in range(k)]
    ctx_source = len(res_inputs) == 2 and with_ctx
    if ctx_source:
        res_specs = [pl.BlockSpec((None, tm, d), lambda i, t: (i, 0, 0))] + [
            pl.BlockSpec((None, tm, d), lambda i, t, j=j: (i, jnp.maximum(k * t + j - 1, 0), 0)) for j in range(k)]
        res_args = [res_inputs[0]] + [res_inputs[1]] * k
    elif len(res_inputs) == 2:
        res_specs, res_args = tiles(0), [res_inputs[1]] * k
    else:
        res_specs, res_args = tiles(off), [res_inputs[0]] * k
    lat_mod = lambda piece: _mod_spec_lat(piece, d)
    ctx_mod = lambda piece: pl.BlockSpec((None, 1, d), lambda i, t: (2 * i, 0, piece))
    mod_specs = ([ctx_mod(2), ctx_mod(3), ctx_mod(4)] if with_ctx else []) + [lat_mod(2), lat_mod(3), lat_mod(4)]
    mod_args = [mods] * len(mod_specs)
    wspec = pl.BlockSpec((None, d, d), lambda i, t: (layer, 0, 0))
    a_off = off - (l - l_a) // tm
    kern = functools.partial(_merge_kernel, k=k, ctx_source=ctx_source, ctx_mods=with_ctx)
    return pl.pallas_call(
        kern,
        out_shape=[jax.ShapeDtypeStruct((b, n_tiles * tm, d), F32),
                   jax.ShapeDtypeStruct((n_tok, d), F32),
                   jax.ShapeDtypeStruct((N_EXPERTS, n_tok), F32)],
        grid=(b, nt),
        in_specs=res_specs + tiles(a_off) + tiles(off) + tiles(off, 1) + tiles(off, 2) + mod_specs + [
            full((1, d)), wspec, wspec, wspec, full((2, N_EXPERTS, d))],
        out_specs=[pl.BlockSpec((None, k * tm, d), lambda i, t: (i, t, 0)),
                   pl.BlockSpec((k * tm, d), lambda i, t: (i * nt + t, 0)),
                   pl.BlockSpec((N_EXPERTS, k * tm), lambda i, t: (0, i * nt + t))],
        compiler_params=_cparams(("parallel", "arbitrary")),
        name="merge_route",
    )(*res_args, *([a] * k), *([r] * k), *([gates] * k), *([gates] * k), *mod_args,
      g2.reshape(1, d), wa, wr, wo, wrt)


def _moe_kernel(tile_a_ref, tile_b_ref, n_act_ref, src_ref, dst_ref, h_hbm, wts_ref,
                wga_ref, wua_ref, wda_ref, wgb_ref, wub_ref, wdb_ref, out_hbm,
                xbuf, ybuf, sem_in, sem_out, *, n_tok):
    i = pl.program_id(0)
    tm = MOE_TILE
    n_act = n_act_ref[0]
    slot = i % 2

    def row_copies(tile, sl, gather, start):
        base = tile * tm

        def body(j, c):
            r0 = j * SUBLANES
            for u in range(SUBLANES):
                if gather:
                    cp = pltpu.make_async_copy(h_hbm.at[pl.ds(src_ref[base + r0 + u], 1), :],
                                               xbuf.at[sl, j, pl.ds(u, 1), :], sem_in.at[sl])
                else:
                    cp = pltpu.make_async_copy(ybuf.at[sl, j, pl.ds(u, 1), :],
                                               out_hbm.at[pl.ds(dst_ref[base + r0 + u], 1), :], sem_out.at[sl])
                if start:
                    cp.start()
                else:
                    cp.wait()
            return c

        lax.fori_loop(0, tm // SUBLANES, body, 0)

    @pl.when(i == 0)
    def _():
        ybuf[0] = jnp.zeros(ybuf.shape[1:], ybuf.dtype)

        def pad_fill(j, start):
            cp = pltpu.make_async_copy(ybuf.at[0, j], out_hbm.at[pl.ds(n_tok + j * SUBLANES, SUBLANES), :],
                                       sem_out.at[0])
            cp.start() if start else cp.wait()

        pl.loop(0, tm // SUBLANES)(functools.partial(pad_fill, start=True))
        pl.loop(0, tm // SUBLANES)(functools.partial(pad_fill, start=False))

        @pl.when(n_act > 0)
        def _():
            row_copies(0, 0, gather=True, start=True)

    @pl.when(i < n_act)
    def _():
        row_copies(i, slot, gather=True, start=False)

        @pl.when(i + 1 < n_act)
        def _():
            row_copies(i + 1, 1 - slot, gather=True, start=True)

        x = xbuf[slot].reshape(tm, xbuf.shape[-1]).astype(BF16)

        def expert(wg_ref, wu_ref, wd_ref):
            gate = jnp.dot(x, wg_ref[...], preferred_element_type=F32)
            up = jnp.dot(x, wu_ref[...], preferred_element_type=F32)
            hid = (gate * _sigmoid(gate) * up).astype(BF16)
            return jnp.dot(hid, wd_ref[...], preferred_element_type=F32)

        wts = wts_ref[...]
        y = wts[:, 0:1] * expert(wga_ref, wua_ref, wda_ref) + wts[:, 1:2] * expert(wgb_ref, wub_ref, wdb_ref)

        @pl.when(i >= 2)
        def _():
            row_copies(i - 2, slot, gather=False, start=False)

        ybuf[slot] = y.reshape(ybuf.shape[1:])
        row_copies(i, slot, gather=False, start=True)

        @pl.when(i == n_act - 1)
        def _():
            row_copies(i, slot, gather=False, start=False)

            @pl.when(i >= 1)
            def _():
                row_copies(i - 1, 1 - slot, gather=False, start=False)


def _pair_class_experts():
    ea, eb = [], []
    for g in range(N_GROUPS):
        for i in range(EXPERTS_PER_GROUP):
            for j in range(i + 1, EXPERTS_PER_GROUP):
                ea.append(g * EXPERTS_PER_GROUP + i)
                eb.append(g * EXPERTS_PER_GROUP + j)
    return np.asarray(ea, np.int32), np.asarray(eb, np.int32)


def _moe_plan(route, n_tok):
    tm = MOE_TILE
    cls = route[0].astype(jnp.int32)
    onehot = (cls[:, None] == jnp.arange(N_PAIR_CLASSES, dtype=jnp.int32)[None, :]).astype(jnp.int32)
    csum = jnp.cumsum(onehot, axis=0)
    counts = csum[-1]
    padded = ((counts + tm - 1) // tm) * tm
    pends = jnp.cumsum(padded)
    pstarts = pends - padded
    pos = jnp.sum(onehot * (csum - 1 + pstarts[None, :]), axis=1)
    n_tiles = n_tok // tm + N_PAIR_CLASSES
    rows = jnp.arange(n_tiles * tm, dtype=jnp.int32)
    pad = jnp.stack([(n_tok + rows % tm).astype(F32), jnp.zeros_like(rows, F32), jnp.zeros_like(rows, F32)], axis=1)
    plan = pad.at[pos].set(jnp.stack([jnp.arange(n_tok, dtype=jnp.int32).astype(F32), route[3], route[4]], axis=1))
    tok = plan[:, 0].astype(jnp.int32)
    wts = plan[:, 1:3]
    tile_rows = rows[::tm]
    n_act = (pends[-1] // tm).reshape(1)
    tile_cls = jnp.sum((jnp.minimum(tile_rows, pends[-1] - 1)[:, None] >= pends[None, :]).astype(jnp.int32), axis=1)
    tile_cls = jnp.minimum(tile_cls, N_PAIR_CLASSES - 1)
    ea_np, eb_np = _pair_class_experts()
    ea = jnp.asarray(ea_np)[tile_cls]
    eb = jnp.asarray(eb_np)[tile_cls]
    return ea, eb, n_act, jnp.minimum(tok, n_tok - 1), tok, wts, n_tiles


def _moe(h2, route, wg, wu, wd, layer):
    n_tok, d = h2.shape
    de = wg.shape[2]
    e0 = layer * N_EXPERTS
    tm = MOE_TILE
    ea, eb, n_act, src, dst, wts, n_tiles = _moe_plan(route, n_tok)
    wspec_a = lambda r, c: pl.BlockSpec((None, r, c), lambda i, ta, tb, na, sr, ds: (e0 + ta[i], 0, 0))
    wspec_b = lambda r, c: pl.BlockSpec((None, r, c), lambda i, ta, tb, na, sr, ds: (e0 + tb[i], 0, 0))
    grid_spec = pltpu.PrefetchScalarGridSpec(
        num_scalar_prefetch=5,
        grid=(n_tiles,),
        in_specs=[pl.BlockSpec(memory_space=pl.ANY),
                  pl.BlockSpec((tm, 2), lambda i, ta, tb, na, sr, ds: (i, 0)),
                  wspec_a(d, de), wspec_a(d, de), wspec_a(de, d),
                  wspec_b(d, de), wspec_b(d, de), wspec_b(de, d)],
        out_specs=pl.BlockSpec(memory_space=pl.ANY),
        scratch_shapes=[pltpu.VMEM((2, tm // SUBLANES, SUBLANES, d), F32),
                        pltpu.VMEM((2, tm // SUBLANES, SUBLANES, d), F32),
                        pltpu.SemaphoreType.DMA((2,)), pltpu.SemaphoreType.DMA((2,))],
    )
    return pl.pallas_call(
        functools.partial(_moe_kernel, n_tok=n_tok),
        out_shape=jax.ShapeDtypeStruct((n_tok + tm, d), F32),
        grid_spec=grid_spec,
        compiler_params=pltpu.CompilerParams(dimension_semantics=("arbitrary",), vmem_limit_bytes=VMEM_LIMIT,
                                             has_side_effects=True),
        name="moe_pairs",
    )(ea, eb, n_act, src, dst, h2, wts, wg, wu, wd, wg, wu, wd)


def _post_kernel(x_ref, y_ref, gt_ref, g_ref, sh_ref, sc_ref, xo_ref, h_ref):
    xn = x_ref[...] + gt_ref[...] * y_ref[...]
    xo_ref[...] = xn
    h_ref[...] = _norm_mod(xn, g_ref[...], sh_ref[...], sc_ref[...]).astype(h_ref.dtype)


def _final_kernel(x_ref, y_ref, gt_ref, g_ref, o_ref):
    xn = x_ref[...] + gt_ref[...] * y_ref[...]
    o_ref[...] = xn * lax.rsqrt(jnp.mean(xn * xn, axis=-1, keepdims=True) + EPS) * g_ref[...]


def _post(x_all, moe_out, mods, mods_next, g_next):
    b, l, d = x_all.shape
    tm = TOKEN_TILE
    nt = l // tm
    tile = pl.BlockSpec((None, tm, d), lambda i, t: (i, t, 0))
    return pl.pallas_call(
        _post_kernel,
        out_shape=[jax.ShapeDtypeStruct((b, l, d), F32), jax.ShapeDtypeStruct((b, l, d), BF16)],
        grid=(b, nt),
        in_specs=[tile, pl.BlockSpec((tm, d), lambda i, t: (i * nt + t, 0)),
                  _mod_spec(5, d), pl.BlockSpec((1, d), lambda i, t: (0, 0)), _mod_spec(0, d), _mod_spec(1, d)],
        out_specs=[tile, tile],
        compiler_params=_cparams(("parallel", "arbitrary")),
        name="moe_residual_prenorm",
    )(x_all, moe_out, mods, g_next.reshape(1, d), mods_next, mods_next)


def _final(x_lat, moe_out, mods, g_final):
    b, s, d = x_lat.shape
    tm = TOKEN_TILE
    nt = s // tm
    tile = pl.BlockSpec((None, tm, d), lambda i, t: (i, t, 0))
    return pl.pallas_call(
        _final_kernel,
        out_shape=jax.ShapeDtypeStruct((b, s, d), F32),
        grid=(b, nt),
        in_specs=[tile, pl.BlockSpec((tm, d), lambda i, t: (i * nt + t, 0)),
                  _mod_spec_lat(5, d), pl.BlockSpec((1, d), lambda i, t: (0, 0))],
        out_specs=tile,
        compiler_params=_cparams(("parallel", "arbitrary")),
        name="moe_residual_final_norm",
    )(x_lat, moe_out, mods, g_final.reshape(1, d))


def _rope_tables(n_ctx, n_lat):
    t = np.arange(n_lat)
    n_freq = ATT_HEAD_DIM // 4
    inv = (ROPE_BASE ** (-np.arange(n_freq, dtype=np.float32) / n_freq)).astype(np.float32)
    ang_r = (t // GRID_W).astype(np.float32)[:, None] * inv
    ang_c = (t % GRID_W).astype(np.float32)[:, None] * inv
    ang = np.concatenate([ang_r, ang_r, ang_c, ang_c], axis=1)
    ang = np.concatenate([ang, ang], axis=1)
    sign = np.where((np.arange(HEAD_W) % 32) < 16, -1.0, 1.0).astype(np.float32)
    cos = np.concatenate([np.ones((n_ctx, HEAD_W), np.float32), np.cos(ang)], axis=0)
    sin = np.concatenate([np.zeros((n_ctx, HEAD_W), np.float32), np.sin(ang) * sign], axis=0)
    return jnp.asarray(cos, F32), jnp.asarray(sin, F32)


def kernel(x, c, ctx, c_ctx, w_mod, b_mod, g_norm1, g_norm2, w_in, lambda_q1, lambda_k1, lambda_q2, lambda_k2,
           g_subln, lb_logits, g_rec_norm, w_br_attn, w_br_rec, w_out, w_router, b_router, w_gate, w_up, w_down,
           g_final):
    b, s, d = x.shape
    n_ctx = ctx.shape[1]
    depth = w_in.shape[0]
    assert n_ctx == TOKEN_TILE and s % TOKEN_TILE == 0 and s % GRID_W == 0
    assert w_in.shape[2] == N_SPLITS * d

    bp = -(-(b + 1) // 8) * 8
    c_all = jnp.concatenate([c, c_ctx[None], jnp.zeros((bp - b - 1, d), F32)], axis=0)
    mod_all = _modulation(c_all, w_mod, b_mod)
    mods = [jnp.stack([jnp.broadcast_to(mod_all[l, b], (b, 6 * d)), mod_all[l, :b]], axis=1)
            .reshape(2 * b, 1, 6 * d) for l in range(depth)]

    lbs = jnp.cumsum(jax.nn.softmax(lb_logits.astype(F32), axis=0), axis=0)
    lbs = lbs - lbs[:1]
    cos_t, sin_t = _rope_tables(n_ctx, s)
    wrt_f = jnp.transpose(w_router).astype(F32)
    wrt_hi = wrt_f.astype(BF16)
    wrt = jnp.stack([wrt_hi, (wrt_f - wrt_hi.astype(F32)).astype(BF16)])
    l_all = n_ctx + s
    full2 = lambda shape: pl.BlockSpec(shape, lambda j, i: (0, 0))

    w_in_bf, w_br_a_bf, w_br_r_bf, w_out_bf = (_to_bf16(w) for w in (w_in, w_br_attn, w_br_rec, w_out))
    w_gate_bf, w_up_bf, w_down_bf = (_to_bf16(w).reshape((-1,) + w.shape[2:]) for w in (w_gate, w_up, w_down))
    x_all = None
    out = None
    h = _prenorm(ctx, x, g_norm1[0], mods[0])
    for l in range(depth):
        last = l == depth - 1
        lam_init = 0.8 - 0.6 * math.exp(-0.3 * l)
        lam = (jnp.exp(jnp.sum(lambda_q1[l].astype(F32) * lambda_k1[l].astype(F32)))
               - jnp.exp(jnp.sum(lambda_q2[l].astype(F32) * lambda_k2[l].astype(F32))) + lam_init)
        (qk,) = _project(_proj_rope_kernel, h, w_in_bf, l, 0, 2, [cos_t, sin_t],
                         [full2((l_all, HEAD_W)), full2((l_all, HEAD_W))], [BF16], "proj_rope")
        (vq,) = _project(_proj_plain_kernel, h, w_in_bf, l, 2, 2, [], [], [BF16], "proj_v_rq")
        if l == 0:
            kk, gg = _project(_proj_forget_nobound_kernel, h, w_in_bf, l, 4, 2, [], [], [BF16, F32], "proj_forget0")
        else:
            kk, gg = _project(_proj_forget_kernel, h, w_in_bf, l, 4, 2, [lbs[l].reshape(1, d)],
                              [full2((1, d))], [BF16, F32], "proj_forget")
        (ri,) = _project(_proj_plain_kernel, h, w_in_bf, l, 6, 1, [], [], [BF16], "proj_ri")
        (gates,) = _project(_proj_gates_kernel, h, w_in_bf, l, 7, 3, [], [], [BF16], "proj_gates")

        a = _attention(qk, vq, lam, g_subln[l], lam_init, n_ctx, with_ctx=not last)
        r = _recurrence(vq, kk, gg, ri, gates, g_rec_norm[l], n_ctx, ctx_outputs=not last)

        res = (ctx, x) if l == 0 else (x_all,)
        xn, h2, logits_t = _merge(res, a, r, gates, mods[l], g_norm2[l], w_br_a_bf, w_br_r_bf, w_out_bf, l,
                                  wrt, n_ctx, with_ctx=not last)
        route = _routing(logits_t, b_router)
        moe_out = _moe(h2, route, w_gate_bf, w_up_bf, w_down_bf, l)
        if last:
            out = _final(xn, moe_out, mods[l], g_final)
        else:
            x_all, h = _post(xn, moe_out, mods[l], mods[l + 1], g_norm1[l + 1])
    return out
```

```python
import functools
import math

import numpy as np
import jax
import jax.numpy as jnp
from jax import lax
from jax.experimental import pallas as pl
from jax.experimental.pallas import tpu as pltpu

EPS = 1e-6
GRID_W = 64
ROPE_BASE = 10000.0
ATT_HEADS = 8
ATT_HEAD_DIM = 64
HEAD_W = 2 * ATT_HEAD_DIM
REC_HEADS = 8
N_EXPERTS = 16
N_GROUPS = 4
EXPERTS_PER_GROUP = N_EXPERTS // N_GROUPS
N_PAIR_CLASSES = N_GROUPS * 6
N_SPLITS = 10

LANES = 128
SUBLANES = 8
TOKEN_TILE = 256
PROJ_ROWS = 768
PROJ_BLOCK_BYTES = 40 * 1024 * 1024
ATT_HEADS_PER_STEP = 4
MERGE_TILES_PER_STEP = (4, 3, 2, 1)
GLA_CHUNK = 64
GLA_GROUP = 4
GLA_MIN_BCAST_BLOCK = 4
CAST_BLOCK_BYTES = 8 * 1024 * 1024
MOE_TILE = 256
VMEM_LIMIT = 56 * 1024 * 1024

F32 = jnp.float32
BF16 = jnp.bfloat16
LOG2_E = math.log2(math.e)


def _sigmoid(x):
    return 1.0 / (1.0 + jnp.exp(-x))


def _cparams(sem):
    return pltpu.CompilerParams(dimension_semantics=sem, vmem_limit_bytes=VMEM_LIMIT)


def _mod_kernel(c_ref, w_ref, b_ref, o_ref):
    c = c_ref[...]
    s = c * _sigmoid(c)
    o_ref[...] = jnp.dot(s, w_ref[...], precision=lax.Precision.HIGHEST,
                         preferred_element_type=F32) + b_ref[...]


def _modulation(c_all, w_mod, b_mod):
    depth, d, n = w_mod.shape
    bp = c_all.shape[0]
    tn = 1024
    return pl.pallas_call(
        _mod_kernel,
        out_shape=jax.ShapeDtypeStruct((depth, bp, n), F32),
        grid=(depth, n // tn),
        in_specs=[pl.BlockSpec((bp, d), lambda l, j: (0, 0)),
                  pl.BlockSpec((None, d, tn), lambda l, j: (l, 0, j)),
                  pl.BlockSpec((None, 1, tn), lambda l, j: (l, 0, j))],
        out_specs=pl.BlockSpec((None, bp, tn), lambda l, j: (l, 0, j)),
        compiler_params=_cparams(("arbitrary", "arbitrary")),
        name="adaln_mod",
    )(c_all, w_mod, b_mod.reshape(depth, 1, n))


def _cast_kernel(x_ref, o_ref):
    o_ref[...] = x_ref[...].astype(o_ref.dtype)


def _to_bf16(w):
    shape = w.shape
    cols = shape[-1]
    rows = w.size // cols
    tr = rows
    while tr * cols * 4 > CAST_BLOCK_BYTES and tr % 16 == 0:
        tr //= 2
    out = pl.pallas_call(
        _cast_kernel,
        out_shape=jax.ShapeDtypeStruct((rows, cols), BF16),
        grid=(rows // tr,),
        in_specs=[pl.BlockSpec((tr, cols), lambda i: (i, 0))],
        out_specs=pl.BlockSpec((tr, cols), lambda i: (i, 0)),
        compiler_params=_cparams(("arbitrary",)),
        name="weights_to_bf16",
    )(w.reshape(rows, cols))
    return out.reshape(shape)


def _norm_mod(x, g, sh, sc):
    y = x * lax.rsqrt(jnp.mean(x * x, axis=-1, keepdims=True) + EPS) * g
    return y * (1.0 + sc) + sh


def _mod_spec(piece, d):
    return pl.BlockSpec((None, 1, d), lambda b, t: (2 * b + jnp.minimum(t, 1), 0, piece))


def _mod_spec_lat(piece, d):
    return pl.BlockSpec((None, 1, d), lambda b, t: (2 * b + 1, 0, piece))


def _prenorm_kernel(ctx_ref, x_ref, g_ref, sh_ref, sc_ref, h_ref):
    t = pl.program_id(1)

    def emit(xv):
        h_ref[...] = _norm_mod(xv, g_ref[...], sh_ref[...], sc_ref[...]).astype(BF16)

    @pl.when(t == 0)
    def _():
        emit(ctx_ref[...])

    @pl.when(t > 0)
    def _():
        emit(x_ref[...])


def _prenorm(ctx, x, g, mods):
    b, c, d = ctx.shape
    s = x.shape[1]
    tm = TOKEN_TILE
    nt = (c + s) // tm
    return pl.pallas_call(
        _prenorm_kernel,
        out_shape=jax.ShapeDtypeStruct((b, c + s, d), BF16),
        grid=(b, nt),
        in_specs=[pl.BlockSpec((None, tm, d), lambda i, t: (i, 0, 0)),
                  pl.BlockSpec((None, tm, d), lambda i, t: (i, jnp.maximum(t - 1, 0), 0)),
                  pl.BlockSpec((1, d), lambda i, t: (0, 0)),
                  _mod_spec(0, d), _mod_spec(1, d)],
        out_specs=pl.BlockSpec((None, tm, d), lambda i, t: (i, t, 0)),
        compiler_params=_cparams(("parallel", "arbitrary")),
        name="prenorm",
    )(ctx, x, g.reshape(1, d), mods, mods)


def _swap16(xs, first16):
    return jnp.where(first16, pltpu.roll(xs, LANES - 16, axis=1), pltpu.roll(xs, 16, axis=1))


def _proj_chunks(h_ref, w_ref, epilogue):
    n = h_ref.shape[1] // PROJ_ROWS
    chunks = [(bi, slice(r * PROJ_ROWS, (r + 1) * PROJ_ROWS)) for bi in range(h_ref.shape[0]) for r in range(n)]
    mm = lambda ch: jnp.dot(h_ref[ch[0], ch[1], :], w_ref[...], preferred_element_type=F32)
    acc = mm(chunks[0])
    for ci, ch in enumerate(chunks):
        nxt = mm(chunks[ci + 1]) if ci + 1 < len(chunks) else None
        epilogue(ch[0], ch[1], acc)
        acc = nxt


def _proj_rope_kernel(h_ref, w_ref, cos_ref, sin_ref, o_ref):
    j = pl.program_id(0)
    scale = jnp.where(j == 0, ATT_HEAD_DIM ** -0.5 * math.log2(math.e), 1.0).astype(F32)
    lane = lax.broadcasted_iota(jnp.int32, (PROJ_ROWS, LANES), 1)
    first16 = (lane % 32) < 16

    def epilogue(bi, rows, acc):
        cos = cos_ref[rows, :] * scale
        sin = sin_ref[rows, :] * scale
        for k in range(acc.shape[1] // LANES):
            xs = acc[:, k * LANES:(k + 1) * LANES]
            o_ref[bi, rows, k * LANES:(k + 1) * LANES] = (xs * cos + _swap16(xs, first16) * sin).astype(o_ref.dtype)

    _proj_chunks(h_ref, w_ref, epilogue)


def _proj_plain_kernel(h_ref, w_ref, o_ref):
    def epilogue(bi, rows, acc):
        o_ref[bi, rows, :] = acc.astype(o_ref.dtype)

    _proj_chunks(h_ref, w_ref, epilogue)


def _proj_forget_kernel(h_ref, w_ref, lb_ref, k_ref, g_ref):
    lb = lb_ref[...]

    def epilogue(bi, rows, z):
        e = jnp.exp(-jnp.abs(z))
        sig = jnp.where(z >= 0.0, 1.0, e) / (1.0 + e)
        f = lb + (1.0 - lb) * sig
        g_ref[bi, rows, :] = jnp.where(f > 0.0, jnp.log(f), z)
        k_ref[bi, rows, :] = (1.0 - f).astype(k_ref.dtype)

    _proj_chunks(h_ref, w_ref, epilogue)


def _proj_forget_nobound_kernel(h_ref, w_ref, k_ref, g_ref):
    def epilogue(bi, rows, z):
        log_f = jnp.minimum(z, 0.0) - jnp.log(1.0 + jnp.exp(-jnp.abs(z)))
        g_ref[bi, rows, :] = log_f
        k_ref[bi, rows, :] = (1.0 - jnp.exp(log_f)).astype(k_ref.dtype)

    _proj_chunks(h_ref, w_ref, epilogue)


def _proj_gates_kernel(h_ref, w_ref, o_ref):
    j = pl.program_id(0)

    def epilogue(bi, rows, z):
        sg = _sigmoid(z)
        o_ref[bi, rows, :] = jnp.where(j == 0, z * sg, sg).astype(o_ref.dtype)

    _proj_chunks(h_ref, w_ref, epilogue)


def _project(kernel, h, w_bf, layer, first_split, n_splits, extra_inputs, extra_specs, out_dtypes, name):
    b, l, d = h.shape
    wcol = w_bf.shape[2] // N_SPLITS
    out_bytes = sum(jnp.dtype(dt).itemsize for dt in out_dtypes)
    bps = next(n for n in (2, 1) if b % n == 0 and n * l * (d * 2 + wcol * out_bytes) * 2 <= PROJ_BLOCK_BYTES)
    out_shape = [jax.ShapeDtypeStruct((b, l, n_splits * wcol), dt) for dt in out_dtypes]
    out_specs = [pl.BlockSpec((bps, l, wcol), lambda j, i: (i, 0, j)) for _ in out_dtypes]
    res = pl.pallas_call(
        kernel,
        out_shape=out_shape,
        grid=(n_splits, b // bps),
        in_specs=[pl.BlockSpec((bps, l, d), lambda j, i: (i, 0, 0)),
                  pl.BlockSpec((None, d, wcol), lambda j, i: (layer, 0, first_split + j))] + extra_specs,
        out_specs=out_specs,
        compiler_params=_cparams(("arbitrary", "arbitrary")),
        name=name,
    )(h, w_bf, *extra_inputs)
    return res


def _attn_kernel(lam_ref, q_ref, k_ref, v_ref, g_ref, o_ref, *, n_ctx, ctx_tile, out_scale):
    lam = lam_ref[0]
    nt = (((1,), (1,)), ((), ()))
    heads = range(ATT_HEADS_PER_STEP)
    col = lambda hh: slice(hh * HEAD_W, (hh + 1) * HEAD_W)

    def attend(n_keys):
        scores = []
        for hh in heads:
            q = q_ref[:, col(hh)]
            k = k_ref[0:n_keys, col(hh)]
            lane = lax.broadcasted_iota(jnp.int32, q.shape, 1)
            zero = jnp.zeros_like(q)
            q1 = jnp.where(lane < ATT_HEAD_DIM, q, zero)
            q2 = jnp.where(lane < ATT_HEAD_DIM, zero, q)
            scores.append((lax.dot_general(q1, k, nt, preferred_element_type=F32),
                           lax.dot_general(q2, k, nt, preferred_element_type=F32)))
        for hh in heads:
            v = v_ref[0:n_keys, col(hh)]
            v1 = jnp.concatenate([v, jnp.ones_like(v)], axis=1)
            outs = []
            for sm in scores[hh]:
                p = jnp.exp2(sm - jnp.max(sm, axis=-1, keepdims=True))
                outs.append(jnp.dot(p.astype(BF16), v1, preferred_element_type=F32))
            (o1, o2) = outs
            l1, l2 = o1[:, HEAD_W:HEAD_W + 1], o2[:, HEAD_W:HEAD_W + 1]
            o = o1[:, :HEAD_W] * (1.0 / l1) - o2[:, :HEAD_W] * (lam / l2)
            o = o * lax.rsqrt(jnp.mean(o * o, axis=-1, keepdims=True) + EPS) * g_ref[...] * out_scale
            o_ref[:, col(hh)] = o.astype(o_ref.dtype)

    n_all = k_ref.shape[0]
    if ctx_tile:
        t = pl.program_id(2)

        @pl.when(t == 0)
        def _():
            attend(n_ctx)

        @pl.when(t > 0)
        def _():
            attend(n_all)
    else:
        attend(n_all)


def _attention(qk, vq, lam, g_subln, lam_init, n_ctx, with_ctx):
    b, l, _ = qk.shape
    hp = ATT_HEADS_PER_STEP
    ng = ATT_HEADS // hp
    tq = TOKEN_TILE
    off = 0 if with_ctx else n_ctx // tq
    n_out = l if with_ctx else l - n_ctx
    kern = functools.partial(_attn_kernel, n_ctx=n_ctx, ctx_tile=with_ctx, out_scale=1.0 - lam_init)
    return pl.pallas_call(
        kern,
        out_shape=jax.ShapeDtypeStruct((b, n_out, ATT_HEADS * HEAD_W), BF16),
        grid=(b, ng, n_out // tq),
        in_specs=[pl.BlockSpec(memory_space=pltpu.SMEM),
                  pl.BlockSpec((None, tq, hp * HEAD_W), lambda i, h, t: (i, t + off, h)),
                  pl.BlockSpec((None, l, hp * HEAD_W), lambda i, h, t: (i, 0, ng + h)),
                  pl.BlockSpec((None, l, hp * HEAD_W), lambda i, h, t: (i, 0, h)),
                  pl.BlockSpec((1, HEAD_W), lambda i, h, t: (0, 0))],
        out_specs=pl.BlockSpec((None, tq, hp * HEAD_W), lambda i, h, t: (i, t, h)),
        compiler_params=_cparams(("parallel", "arbitrary", "arbitrary")),
        name="diff_attention",
    )(lam.reshape(1), qk, qk, vq, g_subln.reshape(1, HEAD_W))


def _gla_tables(chunk):
    c = chunk
    levels = []
    b = c // 2
    while b >= 1:
        levels.append(b)
        b //= 2
    t = np.arange(c)
    u = np.arange(c)[None, :]

    def build(forward):
        mats, qrows, pmask = [], [], []
        tt = t[:, None]
        mats.append((u <= tt) if forward else (u >= tt))
        for b in levels:
            pair = t // (2 * b)
            late = (t % (2 * b)) >= b
            m = pair * 2 * b + b - 1
            if forward:
                is_q = late
                lo = np.where(is_q, m + 1, t + 1)[:, None]
                hi = np.where(is_q, t, m)[:, None]
            else:
                is_q = ~late
                lo = np.where(is_q, t, m + 1)[:, None]
                hi = np.where(is_q, m, t - 1)[:, None]
            if b < GLA_MIN_BCAST_BLOCK:
                mats.append((u >= lo) & (u <= hi))
            qrows.append(np.repeat(is_q[:, None], LANES, 1).astype(np.float32))
            pmask.append(((pair[:, None] == pair[None, :]) & is_q[:, None] & (~is_q)[None, :]).astype(np.float32))
        return np.concatenate(mats, 0).astype(np.float32), np.stack(qrows), np.stack(pmask)

    mf, qf, pf = build(True)
    mb, qb, pb = build(False)
    return levels, np.stack([mf, mb]), np.stack([qf, qb]), np.stack([pf, pb])


def _gla_kernel(q_ref, kf_ref, kb_ref, v_ref, gf_ref, gb_ref, gate_ref, gn_ref, m_ref, qrow_ref, pm_ref,
                o_ref, of_ref, ob_ref, sf_ref, sb_ref, *, levels, n_ctx_chunks, n_chunks, ctx_outputs):
    c = GLA_CHUNK
    u_n = GLA_GROUP
    n_levels = len(levels)
    sf_ref[...] = jnp.zeros_like(sf_ref)
    sb_ref[...] = jnp.zeros_like(sb_ref)
    tn = (((0,), (0,)), ((), ()))
    nt = (((1,), (1,)), ((), ()))

    def decays(direction, r0, g_ref):
        rows = [pl.ds(r0 + u * c, c) for u in range(u_n)]
        g = jnp.concatenate([g_ref[rw, :] for rw in rows], axis=1)
        g_hi = g.astype(BF16)
        r1 = g - g_hi.astype(F32)
        g_mid = r1.astype(BF16)
        g_lo = (r1 - g_mid.astype(F32)).astype(BF16)
        m = m_ref[direction]
        sums = (jnp.dot(m, g_hi, preferred_element_type=F32) + jnp.dot(m, g_mid, preferred_element_type=F32)
                + jnp.dot(m, g_lo, preferred_element_type=F32))
        run = sums[0:c, :]

        def to_row(blk, row):
            return jnp.exp2(jnp.abs(blk - jnp.broadcast_to(run[row:row + 1, :], blk.shape)) * (-LOG2_E))

        dec, n_direct = [], 0
        for b in levels:
            if b >= GLA_MIN_BCAST_BLOCK:
                parts = [to_row(run[p * 2 * b:(p + 1) * 2 * b, :], p * 2 * b + (b - 1 if direction == 0 else b))
                         for p in range(c // (2 * b))]
                dec.append(parts[0] if len(parts) == 1 else jnp.concatenate(parts, axis=0))
            else:
                n_direct += 1
                dec.append(jnp.exp(sums[n_direct * c:(n_direct + 1) * c, :]))
        dec.append(jnp.exp(run))
        dec.append(to_row(run, c - 1 if direction == 0 else 0))
        return dec

    def group_pair(r0s, outputs=True):
        k_refs, g_refs, s_refs, out_refs = (kf_ref, kb_ref), (gf_ref, gb_ref), (sf_ref, sb_ref), (of_ref, ob_ref)
        dec = [decays(d, r0s[d], g_refs[d]) for d in (0, 1)]
        work = []
        for d in (0, 1):
            for u in range(u_n):
                rw = pl.ds(r0s[d] + u * c, c)
                k = k_refs[d][rw, :].astype(F32)
                v = v_ref[rw, :]
                du = [e[:, u * LANES:(u + 1) * LANES] for e in dec[d]]
                q_dec, k_dec = du[n_levels], du[n_levels + 1]
                w = dict(d=d, rw=rw, v=v, k_state=(k * k_dec).astype(BF16),
                         total=q_dec[c - 1:c, :] if d == 0 else q_dec[0:1, :])
                if outputs:
                    q = q_ref[rw, :].astype(F32)
                    level_prods = []
                    for lv, b in enumerate(levels):
                        if b >= SUBLANES:
                            qk = jnp.concatenate([(q if ((r // b) % 2 == 1) == (d == 0) else k)[r:r + b, :]
                                                  for r in range(0, c, b)], axis=0)
                        else:
                            qk = jnp.where(qrow_ref[d, lv] > 0.5, q, k)
                        x = (qk * du[lv]).astype(BF16)
                        level_prods.append(lax.dot_general(x, x, nt, preferred_element_type=F32))
                    w.update(prods=level_prods, q_state=(q * q_dec).astype(BF16),
                             diag=jnp.sum(q * k, axis=-1, keepdims=True) * v.astype(F32))
                work.append(w)
        for w in work:
            if outputs:
                scores = jnp.zeros((c, c), F32)
                for lv in range(n_levels):
                    scores = jnp.where(pm_ref[w["d"], lv] > 0.5, w["prods"][lv], scores)
                w["intra"] = w["diag"] + jnp.dot(scores.astype(BF16), w["v"], preferred_element_type=F32)
            w["vk"] = lax.dot_general(w["v"], w["k_state"], tn, preferred_element_type=F32)
        for d in (0, 1):
            state_t = s_refs[d][...]
            mine = [w for w in work if w["d"] == d]
            for w in (mine if d == 0 else mine[::-1]):
                if outputs:
                    out_refs[d][w["rw"], :] = w["intra"] + lax.dot_general(
                        w["q_state"], state_t.astype(BF16), nt, preferred_element_type=F32)
                state_t = state_t * w["total"] + w["vk"]
            s_refs[d][...] = state_t

    n_ctx_groups = n_ctx_chunks // u_n
    n_groups = n_chunks // u_n

    def rows_of(i):
        top = n_ctx_chunks - i * u_n if i < n_ctx_groups else n_chunks - (i - n_ctx_groups) * u_n
        return i * u_n * c, (top - u_n) * c

    for i in range(n_groups):
        group_pair(rows_of(i), outputs=ctx_outputs or i >= n_ctx_groups)
    n_ctx_rows = n_ctx_chunks * c
    if ctx_outputs:
        out_rows = slice(0, n_chunks * c)
    else:
        out_rows = slice(n_ctx_rows, n_chunks * c)
        o_ref[0:n_ctx_rows, :] = jnp.zeros((n_ctx_rows, LANES), o_ref.dtype)
    o = of_ref[out_rows, :] + ob_ref[out_rows, :]
    o = o * lax.rsqrt(jnp.mean(o * o, axis=-1, keepdims=True) + EPS) * gn_ref[...]
    o_ref[out_rows, :] = (o * gate_ref[out_rows, :].astype(F32)).astype(o_ref.dtype)


def _recurrence(vq, kk, gg, ri, gates, g_rec, n_ctx, ctx_outputs):
    b, l, _ = vq.shape
    c = GLA_CHUNK
    levels, m_np, qrow_np, pm_np = _gla_tables(c)
    nh = REC_HEADS
    assert (n_ctx // c) % GLA_GROUP == 0 and ((l - n_ctx) // c) % GLA_GROUP == 0
    kern = functools.partial(_gla_kernel, levels=tuple(levels), n_ctx_chunks=n_ctx // c, n_chunks=l // c,
                             ctx_outputs=ctx_outputs)
    head = lambda off: pl.BlockSpec((None, l, LANES), lambda i, h: (i, 0, h + off))
    const = lambda a: pl.BlockSpec(a.shape, lambda i, h: (0,) * a.ndim)
    m_all = jnp.asarray(m_np, BF16)
    qrow = jnp.asarray(qrow_np, F32)
    pm = jnp.asarray(pm_np, F32)
    return pl.pallas_call(
        kern,
        out_shape=jax.ShapeDtypeStruct((b, l, nh * LANES), BF16),
        grid=(b, nh),
        in_specs=[head(nh), head(0), head(nh), head(0), head(0), head(nh), head(0),
                  pl.BlockSpec((1, LANES), lambda i, h: (0, 0)),
                  const(m_all), const(qrow), const(pm)],
        out_specs=head(0),
        scratch_shapes=[pltpu.VMEM((l, LANES), F32), pltpu.VMEM((l, LANES), F32),
                        pltpu.VMEM((LANES, LANES), F32), pltpu.VMEM((LANES, LANES), F32)],
        compiler_params=_cparams(("parallel", "arbitrary")),
        name="hgrn2_recurrence",
    )(vq, kk, kk, ri, gg, gg, gates, g_rec.reshape(1, LANES), m_all, qrow, pm)


def _route(logits, bias):
    scores = [_sigmoid(lg) for lg in logits]
    biased = [sc + bs for sc, bs in zip(scores, bias)]
    row = lambda a, e: a[e]
    epg = EXPERTS_PER_GROUP
    pairs = [(i, j) for i in range(epg) for j in range(i + 1, epg)]
    best_gs, grp = None, None
    for gi in range(N_GROUPS):
        gs = None
        for (i, j) in pairs:
            s = row(biased, gi * epg + i) + row(biased, gi * epg + j)
            gs = s if gs is None else jnp.maximum(gs, s)
        if best_gs is None:
            best_gs, grp = gs, jnp.zeros_like(gs, dtype=jnp.int32)
        else:
            better = gs > best_gs
            grp = jnp.where(better, gi, grp)
            best_gs = jnp.maximum(best_gs, gs)

    def in_group(a, j):
        out = row(a, j)
        for gi in range(1, N_GROUPS):
            out = jnp.where(grp == gi, row(a, gi * epg + j), out)
        return out

    u = [in_group(biased, j) for j in range(epg)]
    sc = [in_group(scores, j) for j in range(epg)]
    b1, i1 = u[0], jnp.zeros_like(grp)
    for j in range(1, epg):
        better = u[j] > b1
        i1 = jnp.where(better, j, i1)
        b1 = jnp.maximum(b1, u[j])
    i2 = jnp.where(i1 == 0, 1, 0)
    b2 = jnp.where(i1 == 0, u[1], u[0])
    for j in range(1, epg):
        better = (i1 != j) & (u[j] > b2)
        i2 = jnp.where(better, j, i2)
        b2 = jnp.where(better, u[j], b2)
    pick = lambda idx: sum(jnp.where(idx == j, sc[j], 0.0) for j in range(epg))
    w1, w2 = pick(i1), pick(i2)
    tot = w1 + w2
    w1, w2 = w1 / tot, w2 / tot
    la, lb2 = jnp.minimum(i1, i2), jnp.maximum(i1, i2)
    wa = jnp.where(i1 < i2, w1, w2)
    wb = jnp.where(i1 < i2, w2, w1)
    pair_idx = jnp.where(la == 0, lb2 - 1, jnp.where(la == 1, lb2 + 1, 5))
    cls = grp * 6 + pair_idx
    ea = grp * epg + la
    eb = grp * epg + lb2
    return [cls.astype(F32), ea.astype(F32), eb.astype(F32), wa, wb]


def _route_kernel(bias_ref, logits_ref, route_ref):
    rows = _route([logits_ref[e] for e in range(N_EXPERTS)], [bias_ref[e] for e in range(N_EXPERTS)])
    for k, r in enumerate(rows):
        route_ref[k] = r
    for k in range(len(rows), route_ref.shape[0]):
        route_ref[k] = jnp.zeros(route_ref.shape[1:], F32)


def _routing(logits_t, b_router):
    ne, n_tok = logits_t.shape
    w = n_tok // SUBLANES
    out = pl.pallas_call(
        _route_kernel,
        out_shape=jax.ShapeDtypeStruct((8, SUBLANES, w), F32),
        grid=(1,),
        in_specs=[pl.BlockSpec(memory_space=pltpu.SMEM),
                  pl.BlockSpec((ne, SUBLANES, w), lambda i: (0, 0, 0))],
        out_specs=pl.BlockSpec((8, SUBLANES, w), lambda i: (0, 0, 0)),
        compiler_params=_cparams(("arbitrary",)),
        name="route_top2",
    )(b_router.astype(F32), logits_t.reshape(ne, SUBLANES, w))
    return out.reshape(8, n_tok)


def _merge_kernel(*refs, k, ctx_source, ctx_mods):
    refs = list(refs)
    take = lambda n: [refs.pop(0) for _ in range(n)]
    ctx_ref = take(1)[0] if ctx_source else None
    x_refs, a_refs, r_refs, ga_refs, gr_refs = take(k), take(k), take(k), take(k), take(k)
    cmod_refs = take(3) if ctx_mods else None
    gt_ref, sh_ref, sc_ref, g2_ref, wa_ref, wr_ref, wo_ref, wrt_ref, xn_ref, h2_ref, logit_ref = refs
    tm = x_refs[0].shape[0]
    cat = lambda rs: rs[0][...] if len(rs) == 1 else jnp.concatenate([rf[...] for rf in rs], axis=0)

    def emit(first_is_ctx):
        xv = cat(([ctx_ref] + x_refs[1:]) if (first_is_ctx and ctx_source) else x_refs)

        def mod(ref, cref):
            if not (first_is_ctx and ctx_mods):
                return ref[...]
            rows = [jnp.broadcast_to(cref[...], (tm, cref.shape[-1]))]
            if k > 1:
                rows.append(jnp.broadcast_to(ref[...], ((k - 1) * tm, ref.shape[-1])))
            return rows[0] if k == 1 else jnp.concatenate(rows, axis=0)

        gt, sh, sc = (mod(rf, cmod_refs[n] if ctx_mods else None)
                      for n, rf in enumerate((gt_ref, sh_ref, sc_ref)))
        ua = jnp.dot(cat(a_refs), wa_ref[...], preferred_element_type=F32)
        ur = jnp.dot(cat(r_refs), wr_ref[...], preferred_element_type=F32)
        mixed = cat(ga_refs).astype(F32) * ua + cat(gr_refs).astype(F32) * ur
        y = jnp.dot(mixed.astype(BF16), wo_ref[...], preferred_element_type=F32)
        xn = xv + gt * y
        xn_ref[...] = xn
        h2 = _norm_mod(xn, g2_ref[...], sh, sc)
        h2_ref[...] = h2
        nt = (((1,), (1,)), ((), ()))
        h_hi = h2.astype(BF16)
        h_lo = (h2 - h_hi.astype(F32)).astype(BF16)
        w_hi = wrt_ref[0]
        w_lo = wrt_ref[1]
        logit_ref[...] = (lax.dot_general(w_hi, h_hi, nt, preferred_element_type=F32)
                          + lax.dot_general(w_lo, h_hi, nt, preferred_element_type=F32)
                          + lax.dot_general(w_hi, h_lo, nt, preferred_element_type=F32))

    if ctx_source or ctx_mods:
        t = pl.program_id(1)

        @pl.when(t == 0)
        def _():
            emit(True)

        @pl.when(t > 0)
        def _():
            emit(False)
    else:
        emit(False)


def _merge(res_inputs, a, r, gates, mods, g2, wa, wr, wo, layer, wrt, n_ctx, with_ctx):
    b, l_a, d = a.shape
    l = gates.shape[1]
    tm = TOKEN_TILE
    off = 0 if with_ctx else n_ctx // tm
    n_tiles = l // tm - off
    k = next(kk for kk in MERGE_TILES_PER_STEP if n_tiles % kk == 0)
    nt = n_tiles // k
    n_tok = b * n_tiles * tm
    full = lambda shape: pl.BlockSpec(shape, lambda i, t: (0,) * len(shape))
    tile = lambda o, j, col=0: pl.BlockSpec((None, tm, d), lambda i, t: (i, k * t + j + o, col))
    tiles = lambda o, col=0: [tile(o, j, col) for j in range(k)]
    ctx_source = len(res_inputs) == 2 and with_ctx
    if ctx_source:
        res_specs = [pl.BlockSpec((None, tm, d), lambda i, t: (i, 0, 0))] + [
            pl.BlockSpec((None, tm, d), lambda i, t, j=j: (i, jnp.maximum(k * t + j - 1, 0), 0)) for j in range(k)]
        res_args = [res_inputs[0]] + [res_inputs[1]] * k
    elif len(res_inputs) == 2:
        res_specs, res_args = tiles(0), [res_inputs[1]] * k
    else:
        res_specs, res_args = tiles(off), [res_inputs[0]] * k
    lat_mod = lambda piece: _mod_spec_lat(piece, d)
    ctx_mod = lambda piece: pl.BlockSpec((None, 1, d), lambda i, t: (2 * i, 0, piece))
    mod_specs = ([ctx_mod(2), ctx_mod(3), ctx_mod(4)] if with_ctx else []) + [lat_mod(2), lat_mod(3), lat_mod(4)]
    mod_args = [mods] * len(mod_specs)
    wspec = pl.BlockSpec((None, d, d), lambda i, t: (layer, 0, 0))
    a_off = off - (l - l_a) // tm
    kern = functools.partial(_merge_kernel, k=k, ctx_source=ctx_source, ctx_mods=with_ctx)
    return pl.pallas_call(
        kern,
        out_shape=[jax.ShapeDtypeStruct((b, n_tiles * tm, d), F32),
                   jax.ShapeDtypeStruct((n_tok, d), F32),
                   jax.ShapeDtypeStruct((N_EXPERTS, n_tok), F32)],
        grid=(b, nt),
        in_specs=res_specs + tiles(a_off) + tiles(off) + tiles(off, 1) + tiles(off, 2) + mod_specs + [
            full((1, d)), wspec, wspec, wspec, full((2, N_EXPERTS, d))],
        out_specs=[pl.BlockSpec((None, k * tm, d), lambda i, t: (i, t, 0)),
                   pl.BlockSpec((k * tm, d), lambda i, t: (i * nt + t, 0)),
                   pl.BlockSpec((N_EXPERTS, k * tm), lambda i, t: (0, i * nt + t))],
        compiler_params=_cparams(("parallel", "arbitrary")),
        name="merge_route",
    )(*res_args, *([a] * k), *([r] * k), *([gates] * k), *([gates] * k), *mod_args,
      g2.reshape(1, d), wa, wr, wo, wrt)


def _moe_kernel(tile_a_ref, tile_b_ref, n_act_ref, src_ref, dst_ref, h_hbm, wts_ref,
                wga_ref, wua_ref, wda_ref, wgb_ref, wub_ref, wdb_ref, out_hbm,
                xbuf, ybuf, sem_in, sem_out, *, n_tok):
    i = pl.program_id(0)
    tm = MOE_TILE
    n_act = n_act_ref[0]
    slot = i % 2

    def row_copies(tile, sl, gather, start):
        base = tile * tm

        def body(j, c):
            r0 = j * SUBLANES
            for u in range(SUBLANES):
                if gather:
                    cp = pltpu.make_async_copy(h_hbm.at[pl.ds(src_ref[base + r0 + u], 1), :],
                                               xbuf.at[sl, j, pl.ds(u, 1), :], sem_in.at[sl])
                else:
                    cp = pltpu.make_async_copy(ybuf.at[sl, j, pl.ds(u, 1), :],
                                               out_hbm.at[pl.ds(dst_ref[base + r0 + u], 1), :], sem_out.at[sl])
                if start:
                    cp.start()
                else:
                    cp.wait()
            return c

        lax.fori_loop(0, tm // SUBLANES, body, 0)

    @pl.when(i == 0)
    def _():
        ybuf[0] = jnp.zeros(ybuf.shape[1:], ybuf.dtype)

        def pad_fill(j, start):
            cp = pltpu.make_async_copy(ybuf.at[0, j], out_hbm.at[pl.ds(n_tok + j * SUBLANES, SUBLANES), :],
                                       sem_out.at[0])
            cp.start() if start else cp.wait()

        pl.loop(0, tm // SUBLANES)(functools.partial(pad_fill, start=True))
        pl.loop(0, tm // SUBLANES)(functools.partial(pad_fill, start=False))

        @pl.when(n_act > 0)
        def _():
            row_copies(0, 0, gather=True, start=True)

    @pl.when(i < n_act)
    def _():
        row_copies(i, slot, gather=True, start=False)

        @pl.when(i + 1 < n_act)
        def _():
            row_copies(i + 1, 1 - slot, gather=True, start=True)

        x = xbuf[slot].reshape(tm, xbuf.shape[-1]).astype(BF16)

        def expert(wg_ref, wu_ref, wd_ref):
            gate = jnp.dot(x, wg_ref[...], preferred_element_type=F32)
            up = jnp.dot(x, wu_ref[...], preferred_element_type=F32)
            hid = (gate * _sigmoid(gate) * up).astype(BF16)
            return jnp.dot(hid, wd_ref[...], preferred_element_type=F32)

        wts = wts_ref[...]
        y = wts[:, 0:1] * expert(wga_ref, wua_ref, wda_ref) + wts[:, 1:2] * expert(wgb_ref, wub_ref, wdb_ref)

        @pl.when(i >= 2)
        def _():
            row_copies(i - 2, slot, gather=False, start=False)

        ybuf[slot] = y.reshape(ybuf.shape[1:])
        row_copies(i, slot, gather=False, start=True)

        @pl.when(i == n_act - 1)
        def _():
            row_copies(i, slot, gather=False, start=False)

            @pl.when(i >= 1)
            def _():
                row_copies(i - 1, 1 - slot, gather=False, start=False)


def _pair_class_experts():
    ea, eb = [], []
    for g in range(N_GROUPS):
        for i in range(EXPERTS_PER_GROUP):
            for j in range(i + 1, EXPERTS_PER_GROUP):
                ea.append(g * EXPERTS_PER_GROUP + i)
                eb.append(g * EXPERTS_PER_GROUP + j)
    return np.asarray(ea, np.int32), np.asarray(eb, np.int32)


def _moe_plan(route, n_tok):
    tm = MOE_TILE
    cls = route[0].astype(jnp.int32)
    onehot = (cls[:, None] == jnp.arange(N_PAIR_CLASSES, dtype=jnp.int32)[None, :]).astype(jnp.int32)
    csum = jnp.cumsum(onehot, axis=0)
    counts = csum[-1]
    padded = ((counts + tm - 1) // tm) * tm
    pends = jnp.cumsum(padded)
    pstarts = pends - padded
    pos = jnp.sum(onehot * (csum - 1 + pstarts[None, :]), axis=1)
    n_tiles = n_tok // tm + N_PAIR_CLASSES
    rows = jnp.arange(n_tiles * tm, dtype=jnp.int32)
    pad = jnp.stack([(n_tok + rows % tm).astype(F32), jnp.zeros_like(rows, F32), jnp.zeros_like(rows, F32)], axis=1)
    plan = pad.at[pos].set(jnp.stack([jnp.arange(n_tok, dtype=jnp.int32).astype(F32), route[3], route[4]], axis=1))
    tok = plan[:, 0].astype(jnp.int32)
    wts = plan[:, 1:3]
    tile_rows = rows[::tm]
    n_act = (pends[-1] // tm).reshape(1)
    tile_cls = jnp.sum((jnp.minimum(tile_rows, pends[-1] - 1)[:, None] >= pends[None, :]).astype(jnp.int32), axis=1)
    tile_cls = jnp.minimum(tile_cls, N_PAIR_CLASSES - 1)
    ea_np, eb_np = _pair_class_experts()
    ea = jnp.asarray(ea_np)[tile_cls]
    eb = jnp.asarray(eb_np)[tile_cls]
    return ea, eb, n_act, jnp.minimum(tok, n_tok - 1), tok, wts, n_tiles


def _moe(h2, route, wg, wu, wd, layer):
    n_tok, d = h2.shape
    de = wg.shape[2]
    e0 = layer * N_EXPERTS
    tm = MOE_TILE
    ea, eb, n_act, src, dst, wts, n_tiles = _moe_plan(route, n_tok)
    wspec_a = lambda r, c: pl.BlockSpec((None, r, c), lambda i, ta, tb, na, sr, ds: (e0 + ta[i], 0, 0))
    wspec_b = lambda r, c: pl.BlockSpec((None, r, c), lambda i, ta, tb, na, sr, ds: (e0 + tb[i], 0, 0))
    grid_spec = pltpu.PrefetchScalarGridSpec(
        num_scalar_prefetch=5,
        grid=(n_tiles,),
        in_specs=[pl.BlockSpec(memory_space=pl.ANY),
                  pl.BlockSpec((tm, 2), lambda i, ta, tb, na, sr, ds: (i, 0)),
                  wspec_a(d, de), wspec_a(d, de), wspec_a(de, d),
                  wspec_b(d, de), wspec_b(d, de), wspec_b(de, d)],
        out_specs=pl.BlockSpec(memory_space=pl.ANY),
        scratch_shapes=[pltpu.VMEM((2, tm // SUBLANES, SUBLANES, d), F32),
                        pltpu.VMEM((2, tm // SUBLANES, SUBLANES, d), F32),
                        pltpu.SemaphoreType.DMA((2,)), pltpu.SemaphoreType.DMA((2,))],
    )
    return pl.pallas_call(
        functools.partial(_moe_kernel, n_tok=n_tok),
        out_shape=jax.ShapeDtypeStruct((n_tok + tm, d), F32),
        grid_spec=grid_spec,
        compiler_params=pltpu.CompilerParams(dimension_semantics=("arbitrary",), vmem_limit_bytes=VMEM_LIMIT,
                                             has_side_effects=True),
        name="moe_pairs",
    )(ea, eb, n_act, src, dst, h2, wts, wg, wu, wd, wg, wu, wd)


def _post_kernel(x_ref, y_ref, gtc_ref, gt_ref, g_ref, shc_ref, sh_ref, scc_ref, sc_ref, xo_ref, h_ref, *, n_ctx):
    t = pl.program_id(1)
    rows, d = x_ref.shape

    def emit(first_is_ctx):
        def mod(ref, cref):
            if not first_is_ctx or n_ctx >= rows:
                return (cref if first_is_ctx else ref)[...]
            return jnp.concatenate([jnp.broadcast_to(cref[...], (n_ctx, d)),
                                    jnp.broadcast_to(ref[...], (rows - n_ctx, d))], axis=0)

        xn = x_ref[...] + mod(gt_ref, gtc_ref) * y_ref[...]
        xo_ref[...] = xn
        h_ref[...] = _norm_mod(xn, g_ref[...], mod(sh_ref, shc_ref), mod(sc_ref, scc_ref)).astype(h_ref.dtype)

    @pl.when(t == 0)
    def _():
        emit(True)

    @pl.when(t > 0)
    def _():
        emit(False)


def _final_kernel(x_ref, y_ref, gt_ref, g_ref, o_ref):
    xn = x_ref[...] + gt_ref[...] * y_ref[...]
    o_ref[...] = xn * lax.rsqrt(jnp.mean(xn * xn, axis=-1, keepdims=True) + EPS) * g_ref[...]


def _post(x_all, moe_out, mods, mods_next, g_next, n_ctx):
    b, l, d = x_all.shape
    tm = next(t for t in (3 * TOKEN_TILE, 2 * TOKEN_TILE, TOKEN_TILE) if l % t == 0 and t >= n_ctx)
    nt = l // tm
    tile = pl.BlockSpec((None, tm, d), lambda i, t: (i, t, 0))
    ctx_mod = lambda piece: pl.BlockSpec((None, 1, d), lambda i, t: (2 * i, 0, piece))
    lat_mod = lambda piece: _mod_spec_lat(piece, d)
    return pl.pallas_call(
        functools.partial(_post_kernel, n_ctx=n_ctx),
        out_shape=[jax.ShapeDtypeStruct((b, l, d), F32), jax.ShapeDtypeStruct((b, l, d), BF16)],
        grid=(b, nt),
        in_specs=[tile, pl.BlockSpec((tm, d), lambda i, t: (i * nt + t, 0)),
                  ctx_mod(5), lat_mod(5), pl.BlockSpec((1, d), lambda i, t: (0, 0)),
                  ctx_mod(0), lat_mod(0), ctx_mod(1), lat_mod(1)],
        out_specs=[tile, tile],
        compiler_params=_cparams(("parallel", "arbitrary")),
        name="moe_residual_prenorm",
    )(x_all, moe_out, mods, mods, g_next.reshape(1, d), mods_next, mods_next, mods_next, mods_next)


def _final(x_lat, moe_out, mods, g_final):
    b, s, d = x_lat.shape
    tm = next(t for t in (4 * TOKEN_TILE, 2 * TOKEN_TILE, TOKEN_TILE) if s % t == 0)
    nt = s // tm
    tile = pl.BlockSpec((None, tm, d), lambda i, t: (i, t, 0))
    return pl.pallas_call(
        _final_kernel,
        out_shape=jax.ShapeDtypeStruct((b, s, d), F32),
        grid=(b, nt),
        in_specs=[tile, pl.BlockSpec((tm, d), lambda i, t: (i * nt + t, 0)),
                  _mod_spec_lat(5, d), pl.BlockSpec((1, d), lambda i, t: (0, 0))],
        out_specs=tile,
        compiler_params=_cparams(("parallel", "arbitrary")),
        name="moe_residual_final_norm",
    )(x_lat, moe_out, mods, g_final.reshape(1, d))


def _rope_tables(n_ctx, n_lat):
    t = np.arange(n_lat)
    n_freq = ATT_HEAD_DIM // 4
    inv = (ROPE_BASE ** (-np.arange(n_freq, dtype=np.float32) / n_freq)).astype(np.float32)
    ang_r = (t // GRID_W).astype(np.float32)[:, None] * inv
    ang_c = (t % GRID_W).astype(np.float32)[:, None] * inv
    ang = np.concatenate([ang_r, ang_r, ang_c, ang_c], axis=1)
    ang = np.concatenate([ang, ang], axis=1)
    sign = np.where((np.arange(HEAD_W) % 32) < 16, -1.0, 1.0).astype(np.float32)
    cos = np.concatenate([np.ones((n_ctx, HEAD_W), np.float32), np.cos(ang)], axis=0)
    sin = np.concatenate([np.zeros((n_ctx, HEAD_W), np.float32), np.sin(ang) * sign], axis=0)
    return jnp.asarray(cos, F32), jnp.asarray(sin, F32)


def kernel(x, c, ctx, c_ctx, w_mod, b_mod, g_norm1, g_norm2, w_in, lambda_q1, lambda_k1, lambda_q2, lambda_k2,
           g_subln, lb_logits, g_rec_norm, w_br_attn, w_br_rec, w_out, w_router, b_router, w_gate, w_up, w_down,
           g_final):
    b, s, d = x.shape
    n_ctx = ctx.shape[1]
    depth = w_in.shape[0]
    assert n_ctx == TOKEN_TILE and s % TOKEN_TILE == 0 and s % GRID_W == 0
    assert w_in.shape[2] == N_SPLITS * d

    bp = -(-(b + 1) // 8) * 8
    c_all = jnp.concatenate([c, c_ctx[None], jnp.zeros((bp - b - 1, d), F32)], axis=0)
    mod_all = _modulation(c_all, w_mod, b_mod)
    mods = [jnp.stack([jnp.broadcast_to(mod_all[l, b], (b, 6 * d)), mod_all[l, :b]], axis=1)
            .reshape(2 * b, 1, 6 * d) for l in range(depth)]

    lbs = jnp.cumsum(jax.nn.softmax(lb_logits.astype(F32), axis=0), axis=0)
    lbs = lbs - lbs[:1]
    cos_t, sin_t = _rope_tables(n_ctx, s)
    wrt_f = jnp.transpose(w_router).astype(F32)
    wrt_hi = wrt_f.astype(BF16)
    wrt = jnp.stack([wrt_hi, (wrt_f - wrt_hi.astype(F32)).astype(BF16)])
    l_all = n_ctx + s
    full2 = lambda shape: pl.BlockSpec(shape, lambda j, i: (0, 0))

    w_in_bf, w_br_a_bf, w_br_r_bf, w_out_bf = (_to_bf16(w) for w in (w_in, w_br_attn, w_br_rec, w_out))
    w_gate_bf, w_up_bf, w_down_bf = (_to_bf16(w).reshape((-1,) + w.shape[2:]) for w in (w_gate, w_up, w_down))
    x_all = None
    out = None
    h = _prenorm(ctx, x, g_norm1[0], mods[0])
    for l in range(depth):
        last = l == depth - 1
        lam_init = 0.8 - 0.6 * math.exp(-0.3 * l)
        lam = (jnp.exp(jnp.sum(lambda_q1[l].astype(F32) * lambda_k1[l].astype(F32)))
               - jnp.exp(jnp.sum(lambda_q2[l].astype(F32) * lambda_k2[l].astype(F32))) + lam_init)
        (qk,) = _project(_proj_rope_kernel, h, w_in_bf, l, 0, 2, [cos_t, sin_t],
                         [full2((l_all, HEAD_W)), full2((l_all, HEAD_W))], [BF16], "proj_rope")
        (vq,) = _project(_proj_plain_kernel, h, w_in_bf, l, 2, 2, [], [], [BF16], "proj_v_rq")
        if l == 0:
            kk, gg = _project(_proj_forget_nobound_kernel, h, w_in_bf, l, 4, 2, [], [], [BF16, F32], "proj_forget0")
        else:
            kk, gg = _project(_proj_forget_kernel, h, w_in_bf, l, 4, 2, [lbs[l].reshape(1, d)],
                              [full2((1, d))], [BF16, F32], "proj_forget")
        (ri,) = _project(_proj_plain_kernel, h, w_in_bf, l, 6, 1, [], [], [BF16], "proj_ri")
        (gates,) = _project(_proj_gates_kernel, h, w_in_bf, l, 7, 3, [], [], [BF16], "proj_gates")

        a = _attention(qk, vq, lam, g_subln[l], lam_init, n_ctx, with_ctx=not last)
        r = _recurrence(vq, kk, gg, ri, gates, g_rec_norm[l], n_ctx, ctx_outputs=not last)

        res = (ctx, x) if l == 0 else (x_all,)
        xn, h2, logits_t = _merge(res, a, r, gates, mods[l], g_norm2[l], w_br_a_bf, w_br_r_bf, w_out_bf, l,
                                  wrt, n_ctx, with_ctx=not last)
        route = _routing(logits_t, b_router)
        moe_out = _moe(h2, route, w_gate_bf, w_up_bf, w_down_bf, l)
        if last:
            out = _final(xn, moe_out, mods[l], g_final)
        else:
            x_all, h = _post(xn, moe_out, mods[l], mods[l + 1], g_norm1[l + 1], n_ctx)
    return out
```

```python
import functools
import math

import numpy as np
import jax
import jax.numpy as jnp
from jax import lax
from jax.experimental import pallas as pl
from jax.experimental.pallas import tpu as pltpu

EPS = 1e-6
GRID_W = 64
ROPE_BASE = 10000.0
ATT_HEADS = 8
ATT_HEAD_DIM = 64
HEAD_W = 2 * ATT_HEAD_DIM
REC_HEADS = 8
N_EXPERTS = 16
N_GROUPS = 4
EXPERTS_PER_GROUP = N_EXPERTS // N_GROUPS
N_PAIR_CLASSES = N_GROUPS * 6
N_SPLITS = 10

LANES = 128
SUBLANES = 8
TOKEN_TILE = 256
PROJ_ROWS = 768
PROJ_BLOCK_BYTES = 40 * 1024 * 1024
ATT_HEADS_PER_STEP = 4
MERGE_TILES_PER_STEP = (4, 3, 2, 1)
GLA_CHUNK = 64
GLA_GROUP = 4
GLA_MIN_BCAST_BLOCK = 4
CAST_BLOCK_BYTES = 8 * 1024 * 1024
MOE_TILE = 256
VMEM_LIMIT = 56 * 1024 * 1024

F32 = jnp.float32
BF16 = jnp.bfloat16
LOG2_E = math.log2(math.e)


def _sigmoid(x):
    return 1.0 / (1.0 + jnp.exp(-x))


def _cparams(sem):
    return pltpu.CompilerParams(dimension_semantics=sem, vmem_limit_bytes=VMEM_LIMIT)


def _mod_kernel(c_ref, w_ref, b_ref, o_ref):
    c = c_ref[...]
    s = c * _sigmoid(c)
    o_ref[...] = jnp.dot(s, w_ref[...], precision=lax.Precision.HIGHEST,
                         preferred_element_type=F32) + b_ref[...]


def _modulation(c_all, w_mod, b_mod):
    depth, d, n = w_mod.shape
    bp = c_all.shape[0]
    tn = 1024
    return pl.pallas_call(
        _mod_kernel,
        out_shape=jax.ShapeDtypeStruct((depth, bp, n), F32),
        grid=(depth, n // tn),
        in_specs=[pl.BlockSpec((bp, d), lambda l, j: (0, 0)),
                  pl.BlockSpec((None, d, tn), lambda l, j: (l, 0, j)),
                  pl.BlockSpec((None, 1, tn), lambda l, j: (l, 0, j))],
        out_specs=pl.BlockSpec((None, bp, tn), lambda l, j: (l, 0, j)),
        compiler_params=_cparams(("arbitrary", "arbitrary")),
        name="adaln_mod",
    )(c_all, w_mod, b_mod.reshape(depth, 1, n))


def _cast_kernel(x_ref, o_ref):
    o_ref[...] = x_ref[...].astype(o_ref.dtype)


def _to_bf16(w):
    shape = w.shape
    cols = shape[-1]
    rows = w.size // cols
    tr = rows
    while tr * cols * 4 > CAST_BLOCK_BYTES and tr % 16 == 0:
        tr //= 2
    out = pl.pallas_call(
        _cast_kernel,
        out_shape=jax.ShapeDtypeStruct((rows, cols), BF16),
        grid=(rows // tr,),
        in_specs=[pl.BlockSpec((tr, cols), lambda i: (i, 0))],
        out_specs=pl.BlockSpec((tr, cols), lambda i: (i, 0)),
        compiler_params=_cparams(("arbitrary",)),
        name="weights_to_bf16",
    )(w.reshape(rows, cols))
    return out.reshape(shape)


def _norm_mod(x, g, sh, sc):
    y = x * lax.rsqrt(jnp.mean(x * x, axis=-1, keepdims=True) + EPS) * g
    return y * (1.0 + sc) + sh


def _mod_spec(piece, d):
    return pl.BlockSpec((None, 1, d), lambda b, t: (2 * b + jnp.minimum(t, 1), 0, piece))


def _mod_spec_lat(piece, d):
    return pl.BlockSpec((None, 1, d), lambda b, t: (2 * b + 1, 0, piece))


def _prenorm_kernel(ctx_ref, x_ref, g_ref, sh_ref, sc_ref, h_ref):
    t = pl.program_id(1)

    def emit(xv):
        h_ref[...] = _norm_mod(xv, g_ref[...], sh_ref[...], sc_ref[...]).astype(BF16)

    @pl.when(t == 0)
    def _():
        emit(ctx_ref[...])

    @pl.when(t > 0)
    def _():
        emit(x_ref[...])


def _prenorm(ctx, x, g, mods):
    b, c, d = ctx.shape
    s = x.shape[1]
    tm = TOKEN_TILE
    nt = (c + s) // tm
    return pl.pallas_call(
        _prenorm_kernel,
        out_shape=jax.ShapeDtypeStruct((b, c + s, d), BF16),
        grid=(b, nt),
        in_specs=[pl.BlockSpec((None, tm, d), lambda i, t: (i, 0, 0)),
                  pl.BlockSpec((None, tm, d), lambda i, t: (i, jnp.maximum(t - 1, 0), 0)),
                  pl.BlockSpec((1, d), lambda i, t: (0, 0)),
                  _mod_spec(0, d), _mod_spec(1, d)],
        out_specs=pl.BlockSpec((None, tm, d), lambda i, t: (i, t, 0)),
        compiler_params=_cparams(("parallel", "arbitrary")),
        name="prenorm",
    )(ctx, x, g.reshape(1, d), mods, mods)


def _swap16(xs, first16):
    return jnp.where(first16, pltpu.roll(xs, LANES - 16, axis=1), pltpu.roll(xs, 16, axis=1))


def _proj_chunks(h_ref, w_ref, epilogue):
    n = h_ref.shape[1] // PROJ_ROWS
    chunks = [(bi, slice(r * PROJ_ROWS, (r + 1) * PROJ_ROWS)) for bi in range(h_ref.shape[0]) for r in range(n)]
    mm = lambda ch: jnp.dot(h_ref[ch[0], ch[1], :], w_ref[...], preferred_element_type=F32)
    acc = mm(chunks[0])
    for ci, ch in enumerate(chunks):
        nxt = mm(chunks[ci + 1]) if ci + 1 < len(chunks) else None
        epilogue(ch[0], ch[1], acc)
        acc = nxt


def _proj_rope_kernel(h_ref, w_ref, cos_ref, sin_ref, o_ref):
    j = pl.program_id(0)
    scale = jnp.where(j == 0, ATT_HEAD_DIM ** -0.5 * math.log2(math.e), 1.0).astype(F32)
    lane = lax.broadcasted_iota(jnp.int32, (PROJ_ROWS, LANES), 1)
    first16 = (lane % 32) < 16

    def epilogue(bi, rows, acc):
        cos = cos_ref[rows, :] * scale
        sin = sin_ref[rows, :] * scale
        for k in range(acc.shape[1] // LANES):
            xs = acc[:, k * LANES:(k + 1) * LANES]
            o_ref[bi, rows, k * LANES:(k + 1) * LANES] = (xs * cos + _swap16(xs, first16) * sin).astype(o_ref.dtype)

    _proj_chunks(h_ref, w_ref, epilogue)


def _proj_plain_kernel(h_ref, w_ref, o_ref):
    def epilogue(bi, rows, acc):
        o_ref[bi, rows, :] = acc.astype(o_ref.dtype)

    _proj_chunks(h_ref, w_ref, epilogue)


def _proj_forget_kernel(h_ref, w_ref, lb_ref, k_ref, g_ref):
    lb = lb_ref[...]

    def epilogue(bi, rows, z):
        e = jnp.exp(-jnp.abs(z))
        sig = jnp.where(z >= 0.0, 1.0, e) / (1.0 + e)
        f = lb + (1.0 - lb) * sig
        g_ref[bi, rows, :] = jnp.where(f > 0.0, jnp.log(f), z)
        k_ref[bi, rows, :] = (1.0 - f).astype(k_ref.dtype)

    _proj_chunks(h_ref, w_ref, epilogue)


def _proj_forget_nobound_kernel(h_ref, w_ref, k_ref, g_ref):
    def epilogue(bi, rows, z):
        log_f = jnp.minimum(z, 0.0) - jnp.log(1.0 + jnp.exp(-jnp.abs(z)))
        g_ref[bi, rows, :] = log_f
        k_ref[bi, rows, :] = (1.0 - jnp.exp(log_f)).astype(k_ref.dtype)

    _proj_chunks(h_ref, w_ref, epilogue)


def _proj_gates_kernel(h_ref, w_ref, o_ref):
    j = pl.program_id(0)

    def epilogue(bi, rows, z):
        sg = _sigmoid(z)
        o_ref[bi, rows, :] = jnp.where(j == 0, z * sg, sg).astype(o_ref.dtype)

    _proj_chunks(h_ref, w_ref, epilogue)


def _project(kernel, h, w_bf, layer, first_split, n_splits, extra_inputs, extra_specs, out_dtypes, name):
    b, l, d = h.shape
    wcol = w_bf.shape[2] // N_SPLITS
    out_bytes = sum(jnp.dtype(dt).itemsize for dt in out_dtypes)
    bps = next(n for n in (2, 1) if b % n == 0 and n * l * (d * 2 + wcol * out_bytes) * 2 <= PROJ_BLOCK_BYTES)
    out_shape = [jax.ShapeDtypeStruct((b, l, n_splits * wcol), dt) for dt in out_dtypes]
    out_specs = [pl.BlockSpec((bps, l, wcol), lambda j, i: (i, 0, j)) for _ in out_dtypes]
    res = pl.pallas_call(
        kernel,
        out_shape=out_shape,
        grid=(n_splits, b // bps),
        in_specs=[pl.BlockSpec((bps, l, d), lambda j, i: (i, 0, 0)),
                  pl.BlockSpec((None, d, wcol), lambda j, i: (layer, 0, first_split + j))] + extra_specs,
        out_specs=out_specs,
        compiler_params=_cparams(("arbitrary", "arbitrary")),
        name=name,
    )(h, w_bf, *extra_inputs)
    return res


def _attn_kernel(lam_ref, q_ref, k_ref, v_ref, g_ref, o_ref, *, n_ctx, ctx_tile, out_scale):
    lam = lam_ref[0]
    nt = (((1,), (1,)), ((), ()))
    heads = range(ATT_HEADS_PER_STEP)
    col = lambda hh: slice(hh * HEAD_W, (hh + 1) * HEAD_W)

    def attend(n_keys):
        scores = []
        for hh in heads:
            q = q_ref[:, col(hh)]
            k = k_ref[0:n_keys, col(hh)]
            lane = lax.broadcasted_iota(jnp.int32, q.shape, 1)
            zero = jnp.zeros_like(q)
            q1 = jnp.where(lane < ATT_HEAD_DIM, q, zero)
            q2 = jnp.where(lane < ATT_HEAD_DIM, zero, q)
            scores.append((lax.dot_general(q1, k, nt, preferred_element_type=F32),
                           lax.dot_general(q2, k, nt, preferred_element_type=F32)))
        for hh in heads:
            v = v_ref[0:n_keys, col(hh)]
            v1 = jnp.concatenate([v, jnp.ones_like(v)], axis=1)
            outs = []
            for sm in scores[hh]:
                p = jnp.exp2(sm - jnp.max(sm, axis=-1, keepdims=True))
                outs.append(jnp.dot(p.astype(BF16), v1, preferred_element_type=F32))
            (o1, o2) = outs
            l1, l2 = o1[:, HEAD_W:HEAD_W + 1], o2[:, HEAD_W:HEAD_W + 1]
            o = o1[:, :HEAD_W] * (1.0 / l1) - o2[:, :HEAD_W] * (lam / l2)
            o = o * lax.rsqrt(jnp.mean(o * o, axis=-1, keepdims=True) + EPS) * g_ref[...] * out_scale
            o_ref[:, col(hh)] = o.astype(o_ref.dtype)

    n_all = k_ref.shape[0]
    if ctx_tile:
        t = pl.program_id(2)

        @pl.when(t == 0)
        def _():
            attend(n_ctx)

        @pl.when(t > 0)
        def _():
            attend(n_all)
    else:
        attend(n_all)


def _attention(qk, vq, lam, g_subln, lam_init, n_ctx, with_ctx):
    b, l, _ = qk.shape
    hp = ATT_HEADS_PER_STEP
    ng = ATT_HEADS // hp
    tq = TOKEN_TILE
    off = 0 if with_ctx else n_ctx // tq
    n_out = l if with_ctx else l - n_ctx
    kern = functools.partial(_attn_kernel, n_ctx=n_ctx, ctx_tile=with_ctx, out_scale=1.0 - lam_init)
    return pl.pallas_call(
        kern,
        out_shape=jax.ShapeDtypeStruct((b, n_out, ATT_HEADS * HEAD_W), BF16),
        grid=(b, ng, n_out // tq),
        in_specs=[pl.BlockSpec(memory_space=pltpu.SMEM),
                  pl.BlockSpec((None, tq, hp * HEAD_W), lambda i, h, t: (i, t + off, h)),
                  pl.BlockSpec((None, l, hp * HEAD_W), lambda i, h, t: (i, 0, ng + h)),
                  pl.BlockSpec((None, l, hp * HEAD_W), lambda i, h, t: (i, 0, h)),
                  pl.BlockSpec((1, HEAD_W), lambda i, h, t: (0, 0))],
        out_specs=pl.BlockSpec((None, tq, hp * HEAD_W), lambda i, h, t: (i, t, h)),
        compiler_params=_cparams(("parallel", "arbitrary", "arbitrary")),
        name="diff_attention",
    )(lam.reshape(1), qk, qk, vq, g_subln.reshape(1, HEAD_W))


def _gla_tables(chunk):
    c = chunk
    levels = []
    b = c // 2
    while b >= 1:
        levels.append(b)
        b //= 2
    t = np.arange(c)
    u = np.arange(c)[None, :]

    def build(forward):
        mats, qrows, pmask = [], [], []
        tt = t[:, None]
        mats.append((u <= tt) if forward else (u >= tt))
        for b in levels:
            pair = t // (2 * b)
            late = (t % (2 * b)) >= b
            m = pair * 2 * b + b - 1
            if forward:
                is_q = late
                lo = np.where(is_q, m + 1, t + 1)[:, None]
                hi = np.where(is_q, t, m)[:, None]
            else:
                is_q = ~late
                lo = np.where(is_q, t, m + 1)[:, None]
                hi = np.where(is_q, m, t - 1)[:, None]
            if b < GLA_MIN_BCAST_BLOCK:
                mats.append((u >= lo) & (u <= hi))
            qrows.append(np.repeat(is_q[:, None], LANES, 1).astype(np.float32))
            pmask.append(((pair[:, None] == pair[None, :]) & is_q[:, None] & (~is_q)[None, :]).astype(np.float32))
        return np.concatenate(mats, 0).astype(np.float32), np.stack(qrows), np.stack(pmask)

    mf, qf, pf = build(True)
    mb, qb, pb = build(False)
    return levels, np.stack([mf, mb]), np.stack([qf, qb]), np.stack([pf, pb])


def _gla_kernel(q_ref, kf_ref, kb_ref, v_ref, gf_ref, gb_ref, gate_ref, gn_ref, m_ref, qrow_ref, pm_ref,
                o_ref, of_ref, ob_ref, sf_ref, sb_ref, *, levels, n_ctx_chunks, n_chunks, ctx_outputs):
    c = GLA_CHUNK
    u_n = GLA_GROUP
    n_levels = len(levels)
    sf_ref[...] = jnp.zeros_like(sf_ref)
    sb_ref[...] = jnp.zeros_like(sb_ref)
    tn = (((0,), (0,)), ((), ()))
    nt = (((1,), (1,)), ((), ()))

    def decays(direction, r0, g_ref):
        rows = [pl.ds(r0 + u * c, c) for u in range(u_n)]
        g = jnp.concatenate([g_ref[rw, :] for rw in rows], axis=1)
        g_hi = g.astype(BF16)
        g_mid = (g - g_hi.astype(F32)).astype(BF16)
        m = m_ref[direction]
        sums = jnp.dot(m, g_hi, preferred_element_type=F32) + jnp.dot(m, g_mid, preferred_element_type=F32)
        run = sums[0:c, :]

        def to_row(blk, row):
            return jnp.exp2(jnp.abs(blk - jnp.broadcast_to(run[row:row + 1, :], blk.shape)) * (-LOG2_E))

        dec, n_direct = [], 0
        for b in levels:
            if b >= GLA_MIN_BCAST_BLOCK:
                parts = [to_row(run[p * 2 * b:(p + 1) * 2 * b, :], p * 2 * b + (b - 1 if direction == 0 else b))
                         for p in range(c // (2 * b))]
                dec.append(parts[0] if len(parts) == 1 else jnp.concatenate(parts, axis=0))
            else:
                n_direct += 1
                dec.append(jnp.exp(sums[n_direct * c:(n_direct + 1) * c, :]))
        dec.append(jnp.exp(run))
        dec.append(to_row(run, c - 1 if direction == 0 else 0))
        return dec

    def group_pair(r0s, outputs=True):
        k_refs, g_refs, s_refs, out_refs = (kf_ref, kb_ref), (gf_ref, gb_ref), (sf_ref, sb_ref), (of_ref, ob_ref)
        dec = [decays(d, r0s[d], g_refs[d]) for d in (0, 1)]
        work = []
        for d in (0, 1):
            for u in range(u_n):
                rw = pl.ds(r0s[d] + u * c, c)
                k = k_refs[d][rw, :].astype(F32)
                v = v_ref[rw, :]
                du = [e[:, u * LANES:(u + 1) * LANES] for e in dec[d]]
                q_dec, k_dec = du[n_levels], du[n_levels + 1]
                w = dict(d=d, rw=rw, v=v, k_state=(k * k_dec).astype(BF16),
                         total=q_dec[c - 1:c, :] if d == 0 else q_dec[0:1, :])
                if outputs:
                    q = q_ref[rw, :].astype(F32)
                    level_prods = []
                    for lv, b in enumerate(levels):
                        if b >= SUBLANES:
                            qk = jnp.concatenate([(q if ((r // b) % 2 == 1) == (d == 0) else k)[r:r + b, :]
                                                  for r in range(0, c, b)], axis=0)
                        else:
                            qk = jnp.where(qrow_ref[d, lv] > 0.5, q, k)
                        x = (qk * du[lv]).astype(BF16)
                        level_prods.append(lax.dot_general(x, x, nt, preferred_element_type=F32))
                    w.update(prods=level_prods, q_state=(q * q_dec).astype(BF16),
                             diag=jnp.sum(q * k, axis=-1, keepdims=True) * v.astype(F32))
                work.append(w)
        for w in work:
            if outputs:
                scores = jnp.zeros((c, c), F32)
                for lv in range(n_levels):
                    scores = jnp.where(pm_ref[w["d"], lv] > 0.5, w["prods"][lv], scores)
                w["intra"] = w["diag"] + jnp.dot(scores.astype(BF16), w["v"], preferred_element_type=F32)
            w["vk"] = lax.dot_general(w["v"], w["k_state"], tn, preferred_element_type=F32)
        for d in (0, 1):
            state_t = s_refs[d][...]
            mine = [w for w in work if w["d"] == d]
            for w in (mine if d == 0 else mine[::-1]):
                if outputs:
                    out_refs[d][w["rw"], :] = w["intra"] + lax.dot_general(
                        w["q_state"], state_t.astype(BF16), nt, preferred_element_type=F32)
                state_t = state_t * w["total"] + w["vk"]
            s_refs[d][...] = state_t

    n_ctx_groups = n_ctx_chunks // u_n
    n_groups = n_chunks // u_n

    def rows_of(i):
        top = n_ctx_chunks - i * u_n if i < n_ctx_groups else n_chunks - (i - n_ctx_groups) * u_n
        return i * u_n * c, (top - u_n) * c

    for i in range(n_groups):
        group_pair(rows_of(i), outputs=ctx_outputs or i >= n_ctx_groups)
    n_ctx_rows = n_ctx_chunks * c
    if ctx_outputs:
        out_rows = slice(0, n_chunks * c)
    else:
        out_rows = slice(n_ctx_rows, n_chunks * c)
        o_ref[0:n_ctx_rows, :] = jnp.zeros((n_ctx_rows, LANES), o_ref.dtype)
    o = of_ref[out_rows, :] + ob_ref[out_rows, :]
    o = o * lax.rsqrt(jnp.mean(o * o, axis=-1, keepdims=True) + EPS) * gn_ref[...]
    o_ref[out_rows, :] = (o * gate_ref[out_rows, :].astype(F32)).astype(o_ref.dtype)


def _recurrence(vq, kk, gg, ri, gates, g_rec, n_ctx, ctx_outputs):
    b, l, _ = vq.shape
    c = GLA_CHUNK
    levels, m_np, qrow_np, pm_np = _gla_tables(c)
    nh = REC_HEADS
    assert (n_ctx // c) % GLA_GROUP == 0 and ((l - n_ctx) // c) % GLA_GROUP == 0
    kern = functools.partial(_gla_kernel, levels=tuple(levels), n_ctx_chunks=n_ctx // c, n_chunks=l // c,
                             ctx_outputs=ctx_outputs)
    head = lambda off: pl.BlockSpec((None, l, LANES), lambda i, h: (i, 0, h + off))
    const = lambda a: pl.BlockSpec(a.shape, lambda i, h: (0,) * a.ndim)
    m_all = jnp.asarray(m_np, BF16)
    qrow = jnp.asarray(qrow_np, F32)
    pm = jnp.asarray(pm_np, F32)
    return pl.pallas_call(
        kern,
        out_shape=jax.ShapeDtypeStruct((b, l, nh * LANES), BF16),
        grid=(b, nh),
        in_specs=[head(nh), head(0), head(nh), head(0), head(0), head(nh), head(0),
                  pl.BlockSpec((1, LANES), lambda i, h: (0, 0)),
                  const(m_all), const(qrow), const(pm)],
        out_specs=head(0),
        scratch_shapes=[pltpu.VMEM((l, LANES), F32), pltpu.VMEM((l, LANES), F32),
                        pltpu.VMEM((LANES, LANES), F32), pltpu.VMEM((LANES, LANES), F32)],
        compiler_params=_cparams(("parallel", "arbitrary")),
        name="hgrn2_recurrence",
    )(vq, kk, kk, ri, gg, gg, gates, g_rec.reshape(1, LANES), m_all, qrow, pm)


def _route(logits, bias):
    scores = [_sigmoid(lg) for lg in logits]
    biased = [sc + bs for sc, bs in zip(scores, bias)]
    row = lambda a, e: a[e]
    epg = EXPERTS_PER_GROUP
    pairs = [(i, j) for i in range(epg) for j in range(i + 1, epg)]
    best_gs, grp = None, None
    for gi in range(N_GROUPS):
        gs = None
        for (i, j) in pairs:
            s = row(biased, gi * epg + i) + row(biased, gi * epg + j)
            gs = s if gs is None else jnp.maximum(gs, s)
        if best_gs is None:
            best_gs, grp = gs, jnp.zeros_like(gs, dtype=jnp.int32)
        else:
            better = gs > best_gs
            grp = jnp.where(better, gi, grp)
            best_gs = jnp.maximum(best_gs, gs)

    def in_group(a, j):
        out = row(a, j)
        for gi in range(1, N_GROUPS):
            out = jnp.where(grp == gi, row(a, gi * epg + j), out)
        return out

    u = [in_group(biased, j) for j in range(epg)]
    sc = [in_group(scores, j) for j in range(epg)]
    b1, i1 = u[0], jnp.zeros_like(grp)
    for j in range(1, epg):
        better = u[j] > b1
        i1 = jnp.where(better, j, i1)
        b1 = jnp.maximum(b1, u[j])
    i2 = jnp.where(i1 == 0, 1, 0)
    b2 = jnp.where(i1 == 0, u[1], u[0])
    for j in range(1, epg):
        better = (i1 != j) & (u[j] > b2)
        i2 = jnp.where(better, j, i2)
        b2 = jnp.where(better, u[j], b2)
    pick = lambda idx: sum(jnp.where(idx == j, sc[j], 0.0) for j in range(epg))
    w1, w2 = pick(i1), pick(i2)
    tot = w1 + w2
    w1, w2 = w1 / tot, w2 / tot
    la, lb2 = jnp.minimum(i1, i2), jnp.maximum(i1, i2)
    wa = jnp.where(i1 < i2, w1, w2)
    wb = jnp.where(i1 < i2, w2, w1)
    pair_idx = jnp.where(la == 0, lb2 - 1, jnp.where(la == 1, lb2 + 1, 5))
    cls = grp * 6 + pair_idx
    ea = grp * epg + la
    eb = grp * epg + lb2
    return [cls.astype(F32), ea.astype(F32), eb.astype(F32), wa, wb]


def _route_kernel(bias_ref, logits_ref, route_ref):
    rows = _route([logits_ref[e] for e in range(N_EXPERTS)], [bias_ref[e] for e in range(N_EXPERTS)])
    for k, r in enumerate(rows):
        route_ref[k] = r
    for k in range(len(rows), route_ref.shape[0]):
        route_ref[k] = jnp.zeros(route_ref.shape[1:], F32)


def _routing(logits_t, b_router):
    ne, n_tok = logits_t.shape
    w = n_tok // SUBLANES
    out = pl.pallas_call(
        _route_kernel,
        out_shape=jax.ShapeDtypeStruct((8, SUBLANES, w), F32),
        grid=(1,),
        in_specs=[pl.BlockSpec(memory_space=pltpu.SMEM),
                  pl.BlockSpec((ne, SUBLANES, w), lambda i: (0, 0, 0))],
        out_specs=pl.BlockSpec((8, SUBLANES, w), lambda i: (0, 0, 0)),
        compiler_params=_cparams(("arbitrary",)),
        name="route_top2",
    )(b_router.astype(F32), logits_t.reshape(ne, SUBLANES, w))
    return out.reshape(8, n_tok)


def _merge_kernel(*refs, k, ctx_source, ctx_mods):
    refs = list(refs)
    take = lambda n: [refs.pop(0) for _ in range(n)]
    ctx_ref = take(1)[0] if ctx_source else None
    x_refs, a_refs, r_refs, ga_refs, gr_refs = take(k), take(k), take(k), take(k), take(k)
    cmod_refs = take(3) if ctx_mods else None
    gt_ref, sh_ref, sc_ref, g2_ref, wa_ref, wr_ref, wo_ref, wrt_ref, xn_ref, h2_ref, logit_ref = refs
    tm = x_refs[0].shape[0]
    cat = lambda rs: rs[0][...] if len(rs) == 1 else jnp.concatenate([rf[...] for rf in rs], axis=0)

    def emit(first_is_ctx):
        xv = cat(([ctx_ref] + x_refs[1:]) if (first_is_ctx and ctx_source) else x_refs)

        def mod(ref, cref):
            if not (first_is_ctx and ctx_mods):
                return ref[...]
            rows = [jnp.broadcast_to(cref[...], (tm, cref.shape[-1]))]
            if k > 1:
                rows.append(jnp.broadcast_to(ref[...], ((k - 1) * tm, ref.shape[-1])))
            return rows[0] if k == 1 else jnp.concatenate(rows, axis=0)

        gt, sh, sc = (mod(rf, cmod_refs[n] if ctx_mods else None)
                      for n, rf in enumerate((gt_ref, sh_ref, sc_ref)))
        ua = jnp.dot(cat(a_refs), wa_ref[...], preferred_element_type=F32)
        ur = jnp.dot(cat(r_refs), wr_ref[...], preferred_element_type=F32)
        mixed = cat(ga_refs).astype(F32) * ua + cat(gr_refs).astype(F32) * ur
        y = jnp.dot(mixed.astype(BF16), wo_ref[...], preferred_element_type=F32)
        xn = xv + gt * y
        xn_ref[...] = xn
        h2 = _norm_mod(xn, g2_ref[...], sh, sc)
        h2_ref[...] = h2
        nt = (((1,), (1,)), ((), ()))
        h_hi = h2.astype(BF16)
        h_lo = (h2 - h_hi.astype(F32)).astype(BF16)
        w_hi = wrt_ref[0]
        w_lo = wrt_ref[1]
        logit_ref[...] = (lax.dot_general(w_hi, h_hi, nt, preferred_element_type=F32)
                          + lax.dot_general(w_lo, h_hi, nt, preferred_element_type=F32)
                          + lax.dot_general(w_hi, h_lo, nt, preferred_element_type=F32))

    if ctx_source or ctx_mods:
        t = pl.program_id(1)

        @pl.when(t == 0)
        def _():
            emit(True)

        @pl.when(t > 0)
        def _():
            emit(False)
    else:
        emit(False)


def _merge(res_inputs, a, r, gates, mods, g2, wa, wr, wo, layer, wrt, n_ctx, with_ctx):
    b, l_a, d = a.shape
    l = gates.shape[1]
    tm = TOKEN_TILE
    off = 0 if with_ctx else n_ctx // tm
    n_tiles = l // tm - off
    k = next(kk for kk in MERGE_TILES_PER_STEP if n_tiles % kk == 0)
    nt = n_tiles // k
    n_tok = b * n_tiles * tm
    full = lambda shape: pl.BlockSpec(shape, lambda i, t: (0,) * len(shape))
    tile = lambda o, j, col=0: pl.BlockSpec((None, tm, d), lambda i, t: (i, k * t + j + o, col))
    tiles = lambda o, col=0: [tile(o, j, col) for j in range(k)]
    ctx_source = len(res_inputs) == 2 and with_ctx
    if ctx_source:
        res_specs = [pl.BlockSpec((None, tm, d), lambda i, t: (i, 0, 0))] + [
            pl.BlockSpec((None, tm, d), lambda i, t, j=j: (i, jnp.maximum(k * t + j - 1, 0), 0)) for j in range(k)]
        res_args = [res_inputs[0]] + [res_inputs[1]] * k
    elif len(res_inputs) == 2:
        res_specs, res_args = tiles(0), [res_inputs[1]] * k
    else:
        res_specs, res_args = tiles(off), [res_inputs[0]] * k
    lat_mod = lambda piece: _mod_spec_lat(piece, d)
    ctx_mod = lambda piece: pl.BlockSpec((None, 1, d), lambda i, t: (2 * i, 0, piece))
    mod_specs = ([ctx_mod(2), ctx_mod(3), ctx_mod(4)] if with_ctx else []) + [lat_mod(2), lat_mod(3), lat_mod(4)]
    mod_args = [mods] * len(mod_specs)
    wspec = pl.BlockSpec((None, d, d), lambda i, t: (layer, 0, 0))
    a_off = off - (l - l_a) // tm
    kern = functools.partial(_merge_kernel, k=k, ctx_source=ctx_source, ctx_mods=with_ctx)
    return pl.pallas_call(
        kern,
        out_shape=[jax.ShapeDtypeStruct((b, n_tiles * tm, d), F32),
                   jax.ShapeDtypeStruct((n_tok, d), F32),
                   jax.ShapeDtypeStruct((N_EXPERTS, n_tok), F32)],
        grid=(b, nt),
        in_specs=res_specs + tiles(a_off) + tiles(off) + tiles(off, 1) + tiles(off, 2) + mod_specs + [
            full((1, d)), wspec, wspec, wspec, full((2, N_EXPERTS, d))],
        out_specs=[pl.BlockSpec((None, k * tm, d), lambda i, t: (i, t, 0)),
                   pl.BlockSpec((k * tm, d), lambda i, t: (i * nt + t, 0)),
                   pl.BlockSpec((N_EXPERTS, k * tm), lambda i, t: (0, i * nt + t))],
        compiler_params=_cparams(("parallel", "arbitrary")),
        name="merge_route",
    )(*res_args, *([a] * k), *([r] * k), *([gates] * k), *([gates] * k), *mod_args,
      g2.reshape(1, d), wa, wr, wo, wrt)


def _moe_kernel(tile_a_ref, tile_b_ref, n_act_ref, src_ref, dst_ref, h_hbm, wts_ref,
                wga_ref, wua_ref, wda_ref, wgb_ref, wub_ref, wdb_ref, out_hbm,
                xbuf, ybuf, sem_in, sem_out, *, n_tok):
    i = pl.program_id(0)
    tm = MOE_TILE
    n_act = n_act_ref[0]
    slot = i % 2

    def row_copies(tile, sl, gather, start):
        base = tile * tm

        def body(j, c):
            r0 = j * SUBLANES
            for u in range(SUBLANES):
                if gather:
                    cp = pltpu.make_async_copy(h_hbm.at[pl.ds(src_ref[base + r0 + u], 1), :],
                                               xbuf.at[sl, j, pl.ds(u, 1), :], sem_in.at[sl])
                else:
                    cp = pltpu.make_async_copy(ybuf.at[sl, j, pl.ds(u, 1), :],
                                               out_hbm.at[pl.ds(dst_ref[base + r0 + u], 1), :], sem_out.at[sl])
                if start:
                    cp.start()
                else:
                    cp.wait()
            return c

        lax.fori_loop(0, tm // SUBLANES, body, 0)

    @pl.when(i == 0)
    def _():
        ybuf[0] = jnp.zeros(ybuf.shape[1:], ybuf.dtype)

        def pad_fill(j, start):
            cp = pltpu.make_async_copy(ybuf.at[0, j], out_hbm.at[pl.ds(n_tok + j * SUBLANES, SUBLANES), :],
                                       sem_out.at[0])
            cp.start() if start else cp.wait()

        pl.loop(0, tm // SUBLANES)(functools.partial(pad_fill, start=True))
        pl.loop(0, tm // SUBLANES)(functools.partial(pad_fill, start=False))

        @pl.when(n_act > 0)
        def _():
            row_copies(0, 0, gather=True, start=True)

    @pl.when(i < n_act)
    def _():
        row_copies(i, slot, gather=True, start=False)

        @pl.when(i + 1 < n_act)
        def _():
            row_copies(i + 1, 1 - slot, gather=True, start=True)

        x = xbuf[slot].reshape(tm, xbuf.shape[-1]).astype(BF16)

        def expert(wg_ref, wu_ref, wd_ref):
            gate = jnp.dot(x, wg_ref[...], preferred_element_type=F32)
            up = jnp.dot(x, wu_ref[...], preferred_element_type=F32)
            hid = (gate * _sigmoid(gate) * up).astype(BF16)
            return jnp.dot(hid, wd_ref[...], preferred_element_type=F32)

        wts = wts_ref[...]
        y = wts[:, 0:1] * expert(wga_ref, wua_ref, wda_ref) + wts[:, 1:2] * expert(wgb_ref, wub_ref, wdb_ref)

        @pl.when(i >= 2)
        def _():
            row_copies(i - 2, slot, gather=False, start=False)

        ybuf[slot] = y.reshape(ybuf.shape[1:])
        row_copies(i, slot, gather=False, start=True)

        @pl.when(i == n_act - 1)
        def _():
            row_copies(i, slot, gather=False, start=False)

            @pl.when(i >= 1)
            def _():
                row_copies(i - 1, 1 - slot, gather=False, start=False)


def _pair_class_experts():
    ea, eb = [], []
    for g in range(N_GROUPS):
        for i in range(EXPERTS_PER_GROUP):
            for j in range(i + 1, EXPERTS_PER_GROUP):
                ea.append(g * EXPERTS_PER_GROUP + i)
                eb.append(g * EXPERTS_PER_GROUP + j)
    return np.asarray(ea, np.int32), np.asarray(eb, np.int32)


def _moe_plan(route, n_tok):
    tm = MOE_TILE
    cls = route[0].astype(jnp.int32)
    onehot = (cls[:, None] == jnp.arange(N_PAIR_CLASSES, dtype=jnp.int32)[None, :]).astype(jnp.int32)
    csum = jnp.cumsum(onehot, axis=0)
    counts = csum[-1]
    padded = ((counts + tm - 1) // tm) * tm
    pends = jnp.cumsum(padded)
    pstarts = pends - padded
    pos = jnp.sum(onehot * (csum - 1 + pstarts[None, :]), axis=1)
    n_tiles = n_tok // tm + N_PAIR_CLASSES
    rows = jnp.arange(n_tiles * tm, dtype=jnp.int32)
    pad = jnp.stack([(n_tok + rows % tm).astype(F32), jnp.zeros_like(rows, F32), jnp.zeros_like(rows, F32)], axis=1)
    plan = pad.at[pos].set(jnp.stack([jnp.arange(n_tok, dtype=jnp.int32).astype(F32), route[3], route[4]], axis=1))
    tok = plan[:, 0].astype(jnp.int32)
    wts = plan[:, 1:3]
    tile_rows = rows[::tm]
    n_act = (pends[-1] // tm).reshape(1)
    tile_cls = jnp.sum((jnp.minimum(tile_rows, pends[-1] - 1)[:, None] >= pends[None, :]).astype(jnp.int32), axis=1)
    tile_cls = jnp.minimum(tile_cls, N_PAIR_CLASSES - 1)
    ea_np, eb_np = _pair_class_experts()
    ea = jnp.asarray(ea_np)[tile_cls]
    eb = jnp.asarray(eb_np)[tile_cls]
    return ea, eb, n_act, jnp.minimum(tok, n_tok - 1), tok, wts, n_tiles


def _moe(h2, route, wg, wu, wd, layer):
    n_tok, d = h2.shape
    de = wg.shape[2]
    e0 = layer * N_EXPERTS
    tm = MOE_TILE
    ea, eb, n_act, src, dst, wts, n_tiles = _moe_plan(route, n_tok)
    wspec_a = lambda r, c: pl.BlockSpec((None, r, c), lambda i, ta, tb, na, sr, ds: (e0 + ta[i], 0, 0))
    wspec_b = lambda r, c: pl.BlockSpec((None, r, c), lambda i, ta, tb, na, sr, ds: (e0 + tb[i], 0, 0))
    grid_spec = pltpu.PrefetchScalarGridSpec(
        num_scalar_prefetch=5,
        grid=(n_tiles,),
        in_specs=[pl.BlockSpec(memory_space=pl.ANY),
                  pl.BlockSpec((tm, 2), lambda i, ta, tb, na, sr, ds: (i, 0)),
                  wspec_a(d, de), wspec_a(d, de), wspec_a(de, d),
                  wspec_b(d, de), wspec_b(d, de), wspec_b(de, d)],
        out_specs=pl.BlockSpec(memory_space=pl.ANY),
        scratch_shapes=[pltpu.VMEM((2, tm // SUBLANES, SUBLANES, d), F32),
                        pltpu.VMEM((2, tm // SUBLANES, SUBLANES, d), F32),
                        pltpu.SemaphoreType.DMA((2,)), pltpu.SemaphoreType.DMA((2,))],
    )
    return pl.pallas_call(
        functools.partial(_moe_kernel, n_tok=n_tok),
        out_shape=jax.ShapeDtypeStruct((n_tok + tm, d), F32),
        grid_spec=grid_spec,
        compiler_params=pltpu.CompilerParams(dimension_semantics=("arbitrary",), vmem_limit_bytes=VMEM_LIMIT,
                                             has_side_effects=True),
        name="moe_pairs",
    )(ea, eb, n_act, src, dst, h2, wts, wg, wu, wd, wg, wu, wd)


def _post_kernel(x_ref, y_ref, gtc_ref, gt_ref, g_ref, shc_ref, sh_ref, scc_ref, sc_ref, xo_ref, h_ref, *, n_ctx):
    t = pl.program_id(1)
    rows, d = x_ref.shape

    def emit(first_is_ctx):
        def mod(ref, cref):
            if not first_is_ctx or n_ctx >= rows:
                return (cref if first_is_ctx else ref)[...]
            return jnp.concatenate([jnp.broadcast_to(cref[...], (n_ctx, d)),
                                    jnp.broadcast_to(ref[...], (rows - n_ctx, d))], axis=0)

        xn = x_ref[...] + mod(gt_ref, gtc_ref) * y_ref[...]
        xo_ref[...] = xn
        h_ref[...] = _norm_mod(xn, g_ref[...], mod(sh_ref, shc_ref), mod(sc_ref, scc_ref)).astype(h_ref.dtype)

    @pl.when(t == 0)
    def _():
        emit(True)

    @pl.when(t > 0)
    def _():
        emit(False)


def _final_kernel(x_ref, y_ref, gt_ref, g_ref, o_ref):
    xn = x_ref[...] + gt_ref[...] * y_ref[...]
    o_ref[...] = xn * lax.rsqrt(jnp.mean(xn * xn, axis=-1, keepdims=True) + EPS) * g_ref[...]


def _post(x_all, moe_out, mods, mods_next, g_next, n_ctx):
    b, l, d = x_all.shape
    tm = next(t for t in (3 * TOKEN_TILE, 2 * TOKEN_TILE, TOKEN_TILE) if l % t == 0 and t >= n_ctx)
    nt = l // tm
    tile = pl.BlockSpec((None, tm, d), lambda i, t: (i, t, 0))
    ctx_mod = lambda piece: pl.BlockSpec((None, 1, d), lambda i, t: (2 * i, 0, piece))
    lat_mod = lambda piece: _mod_spec_lat(piece, d)
    return pl.pallas_call(
        functools.partial(_post_kernel, n_ctx=n_ctx),
        out_shape=[jax.ShapeDtypeStruct((b, l, d), F32), jax.ShapeDtypeStruct((b, l, d), BF16)],
        grid=(b, nt),
        in_specs=[tile, pl.BlockSpec((tm, d), lambda i, t: (i * nt + t, 0)),
                  ctx_mod(5), lat_mod(5), pl.BlockSpec((1, d), lambda i, t: (0, 0)),
                  ctx_mod(0), lat_mod(0), ctx_mod(1), lat_mod(1)],
        out_specs=[tile, tile],
        compiler_params=_cparams(("parallel", "arbitrary")),
        name="moe_residual_prenorm",
    )(x_all, moe_out, mods, mods, g_next.reshape(1, d), mods_next, mods_next, mods_next, mods_next)


def _final(x_lat, moe_out, mods, g_final):
    b, s, d = x_lat.shape
    tm = next(t for t in (4 * TOKEN_TILE, 2 * TOKEN_TILE, TOKEN_TILE) if s % t == 0)
    nt = s // tm
    tile = pl.BlockSpec((None, tm, d), lambda i, t: (i, t, 0))
    return pl.pallas_call(
        _final_kernel,
        out_shape=jax.ShapeDtypeStruct((b, s, d), F32),
        grid=(b, nt),
        in_specs=[tile, pl.BlockSpec((tm, d), lambda i, t: (i * nt + t, 0)),
                  _mod_spec_lat(5, d), pl.BlockSpec((1, d), lambda i, t: (0, 0))],
        out_specs=tile,
        compiler_params=_cparams(("parallel", "arbitrary")),
        name="moe_residual_final_norm",
    )(x_lat, moe_out, mods, g_final.reshape(1, d))


def _rope_tables(n_ctx, n_lat):
    t = np.arange(n_lat)
    n_freq = ATT_HEAD_DIM // 4
    inv = (ROPE_BASE ** (-np.arange(n_freq, dtype=np.float32) / n_freq)).astype(np.float32)
    ang_r = (t // GRID_W).astype(np.float32)[:, None] * inv
    ang_c = (t % GRID_W).astype(np.float32)[:, None] * inv
    ang = np.concatenate([ang_r, ang_r, ang_c, ang_c], axis=1)
    ang = np.concatenate([ang, ang], axis=1)
    sign = np.where((np.arange(HEAD_W) % 32) < 16, -1.0, 1.0).astype(np.float32)
    cos = np.concatenate([np.ones((n_ctx, HEAD_W), np.float32), np.cos(ang)], axis=0)
    sin = np.concatenate([np.zeros((n_ctx, HEAD_W), np.float32), np.sin(ang) * sign], axis=0)
    return jnp.asarray(cos, F32), jnp.asarray(sin, F32)


def kernel(x, c, ctx, c_ctx, w_mod, b_mod, g_norm1, g_norm2, w_in, lambda_q1, lambda_k1, lambda_q2, lambda_k2,
           g_subln, lb_logits, g_rec_norm, w_br_attn, w_br_rec, w_out, w_router, b_router, w_gate, w_up, w_down,
           g_final):
    b, s, d = x.shape
    n_ctx = ctx.shape[1]
    depth = w_in.shape[0]
    assert n_ctx == TOKEN_TILE and s % TOKEN_TILE == 0 and s % GRID_W == 0
    assert w_in.shape[2] == N_SPLITS * d

    bp = -(-(b + 1) // 8) * 8
    c_all = jnp.concatenate([c, c_ctx[None], jnp.zeros((bp - b - 1, d), F32)], axis=0)
    mod_all = _modulation(c_all, w_mod, b_mod)
    mods = [jnp.stack([jnp.broadcast_to(mod_all[l, b], (b, 6 * d)), mod_all[l, :b]], axis=1)
            .reshape(2 * b, 1, 6 * d) for l in range(depth)]

    lbs = jnp.cumsum(jax.nn.softmax(lb_logits.astype(F32), axis=0), axis=0)
    lbs = lbs - lbs[:1]
    cos_t, sin_t = _rope_tables(n_ctx, s)
    wrt_f = jnp.transpose(w_router).astype(F32)
    wrt_hi = wrt_f.astype(BF16)
    wrt = jnp.stack([wrt_hi, (wrt_f - wrt_hi.astype(F32)).astype(BF16)])
    l_all = n_ctx + s
    full2 = lambda shape: pl.BlockSpec(shape, lambda j, i: (0, 0))

    w_in_bf, w_br_a_bf, w_br_r_bf, w_out_bf = (_to_bf16(w) for w in (w_in, w_br_attn, w_br_rec, w_out))
    w_gate_bf, w_up_bf, w_down_bf = (_to_bf16(w).reshape((-1,) + w.shape[2:]) for w in (w_gate, w_up, w_down))
    x_all = None
    out = None
    h = _prenorm(ctx, x, g_norm1[0], mods[0])
    for l in range(depth):
        last = l == depth - 1
        lam_init = 0.8 - 0.6 * math.exp(-0.3 * l)
        lam = (jnp.exp(jnp.sum(lambda_q1[l].astype(F32) * lambda_k1[l].astype(F32)))
               - jnp.exp(jnp.sum(lambda_q2[l].astype(F32) * lambda_k2[l].astype(F32))) + lam_init)
        (qk,) = _project(_proj_rope_kernel, h, w_in_bf, l, 0, 2, [cos_t, sin_t],
                         [full2((l_all, HEAD_W)), full2((l_all, HEAD_W))], [BF16], "proj_rope")
        (vq,) = _project(_proj_plain_kernel, h, w_in_bf, l, 2, 2, [], [], [BF16], "proj_v_rq")
        if l == 0:
            kk, gg = _project(_proj_forget_nobound_kernel, h, w_in_bf, l, 4, 2, [], [], [BF16, F32], "proj_forget0")
        else:
            kk, gg = _project(_proj_forget_kernel, h, w_in_bf, l, 4, 2, [lbs[l].reshape(1, d)],
                              [full2((1, d))], [BF16, F32], "proj_forget")
        (ri,) = _project(_proj_plain_kernel, h, w_in_bf, l, 6, 1, [], [], [BF16], "proj_ri")
        (gates,) = _project(_proj_gates_kernel, h, w_in_bf, l, 7, 3, [], [], [BF16], "proj_gates")

        a = _attention(qk, vq, lam, g_subln[l], lam_init, n_ctx, with_ctx=not last)
        r = _recurrence(vq, kk, gg, ri, gates, g_rec_norm[l], n_ctx, ctx_outputs=not last)

        res = (ctx, x) if l == 0 else (x_all,)
        xn, h2, logits_t = _merge(res, a, r, gates, mods[l], g_norm2[l], w_br_a_bf, w_br_r_bf, w_out_bf, l,
                                  wrt, n_ctx, with_ctx=not last)
        route = _routing(logits_t, b_router)
        moe_out = _moe(h2, route, w_gate_bf, w_up_bf, w_down_bf, l)
        if last:
            out = _final(xn, moe_out, mods[l], g_final)
        else:
            x_all, h = _post(xn, moe_out, mods[l], mods[l + 1], g_norm1[l + 1], n_ctx)
    return out
```
